```python
import math
import jax, jax.numpy as jnp
from jax import lax
import numpy as np


D_MODEL = 1024
BATCH = 8
SEQ = 2048
DEPTH = 4
DEC_BATCH = 128
DEC_SEQ = 1
PAST_LEN = 2048
PAGE_SIZE = 128

N_MIXERS = 3
N_LAYERS_A = (DEPTH + 2) // 3
N_LAYERS_B = (DEPTH + 1) // 3
N_LAYERS_C = DEPTH // 3
EPS = 1e-6

A_CHUNK = 128
A_HALF = 3 * D_MODEL
A_GROUPS = 8
A_GROUP_DIM = A_HALF // A_GROUPS

GLA_HEADS = 4
GLA_DK = D_MODEL // 2 // GLA_HEADS
GLA_DV = D_MODEL // GLA_HEADS
GLA_GATE_RANK = 16
GLA_GATE_NORM = 16.0
GLA_CHUNK = 64
GLA_PROJ = 2 * GLA_HEADS * GLA_DK + 2 * GLA_HEADS * GLA_DV + GLA_GATE_RANK

MOBA_HEADS = 16
MOBA_DH = D_MODEL // MOBA_HEADS
MOBA_BLOCK = 256
MOBA_TOPK = 3
MOBA_QBLK = 8
REL_BUCKETS = 32
REL_MAX_DIST = 128

D_FF = -(-8 * D_MODEL // (3 * 256)) * 256

kernel_name = "interleaved_sgu_gla_moba_decoder_step"


def rms_norm(x, g):
    x32 = x.astype(jnp.float32)
    y = x32 * lax.rsqrt(jnp.mean(x32 * x32, axis=-1, keepdims=True) + EPS)
    return (y * g.astype(jnp.float32)).astype(x.dtype)


def layer_norm(x, g, b):
    x32 = x.astype(jnp.float32)
    mu = jnp.mean(x32, axis=-1, keepdims=True)
    xc = x32 - mu
    y = xc * lax.rsqrt(jnp.mean(xc * xc, axis=-1, keepdims=True) + EPS)
    return (y * g.astype(jnp.float32) + b.astype(jnp.float32)).astype(x.dtype)


def swiglu(h, w_up, w_down):
    g, u = jnp.split(h @ w_up, 2, axis=-1)
    return (jax.nn.silu(g) * u) @ w_down


def chunk_spatial_mix(v, w_s, b_s):
    bsz, L, G, Dg = v.shape
    n = -(-L // A_CHUNK)
    vp = jnp.pad(v, ((0, 0), (0, n * A_CHUNK - L), (0, 0), (0, 0))).reshape(bsz, n, A_CHUNK, G, Dg)
    causal = jnp.tril(jnp.ones((A_CHUNK, A_CHUNK), dtype=bool))
    w = jnp.where(causal[None], w_s, 0.0).astype(v.dtype)
    s = jnp.einsum('gij,bnjgd->bnigd', w, vp) + b_s.T[None, None, :, :, None].astype(v.dtype)
    return s.reshape(bsz, n * A_CHUNK, G, Dg)[:, :L]


def sgu_mixer(h, w_in, ln_g, ln_b, w_s, b_s, w_out):
    bsz, L, _ = h.shape
    u, v = jnp.split(jax.nn.gelu(h @ w_in, approximate=False), 2, axis=-1)
    v = layer_norm(v, ln_g, ln_b)
    s = chunk_spatial_mix(v.reshape(bsz, L, A_GROUPS, A_GROUP_DIM), w_s, b_s).reshape(bsz, L, A_HALF)
    return (u * s) @ w_out, v


def gla_recurrence(q, k, v, log_a, s0):
    bsz, L, H, _ = q.shape
    c = min(GLA_CHUNK, L)
    n = -(-L // c)
    pad = n * c - L

    def to_chunks(t):
        t = jnp.pad(t.astype(jnp.float32), ((0, 0), (0, pad), (0, 0), (0, 0)))
        return t.reshape(bsz, n, c, H, t.shape[-1]).transpose(1, 0, 2, 3, 4)

    causal = jnp.tril(jnp.ones((c, c), dtype=bool))[None, :, :, None, None]

    def step(S, inp):
        qc, kc, vc, ac = inp
        cum = jnp.cumsum(ac, axis=1)
        o_inter = jnp.einsum('bthk,bhkv->bthv', qc * jnp.exp(cum), S)
        decay = jnp.exp(jnp.where(causal, cum[:, :, None] - cum[:, None, :], -jnp.inf))
        att = jnp.einsum('bthk,bshk,btshk->bhts', qc, kc, decay)
        o_intra = jnp.einsum('bhts,bshv->bthv', att, vc)
        last = cum[:, -1]
        k_dec = kc * jnp.exp(last[:, None] - cum)
        S_new = jnp.exp(last)[..., None] * S + jnp.einsum('bshk,bshv->bhkv', k_dec, vc)
        return S_new, o_inter + o_intra

    s_fin, o = lax.scan(step, s0.astype(jnp.float32),
                        (to_chunks(q), to_chunks(k), to_chunks(v), to_chunks(log_a)))
    o = o.transpose(1, 0, 2, 3, 4).reshape(bsz, n * c, H, v.shape[-1])[:, :L]
    return o, s_fin


def gla_mixer(h, s0, w_in, w_gate, b_gate, gn, w_out):
    bsz, L, _ = h.shape
    dq, dv = GLA_HEADS * GLA_DK, GLA_HEADS * GLA_DV
    q, k, v, g, gl = jnp.split(h @ w_in, [dq, 2 * dq, 2 * dq + dv, 2 * dq + 2 * dv], axis=-1)
    log_a = jax.nn.log_sigmoid((gl @ w_gate + b_gate).astype(jnp.float32)) / GLA_GATE_NORM
    q = q.reshape(bsz, L, GLA_HEADS, GLA_DK) * (GLA_DK ** -0.5)
    k = k.reshape(bsz, L, GLA_HEADS, GLA_DK)
    v = v.reshape(bsz, L, GLA_HEADS, GLA_DV)
    log_a = log_a.reshape(bsz, L, GLA_HEADS, GLA_DK)
    o, s_fin = gla_recurrence(q, k, v, log_a, s0)
    o = o * lax.rsqrt(jnp.mean(o * o, axis=-1, keepdims=True) + EPS) * gn.astype(jnp.float32)
    o = o.reshape(bsz, L, dv).astype(h.dtype) * jax.nn.silu(g)
    return o @ w_out, s_fin.astype(s0.dtype)


def t5_bucket(dist):
    max_exact = REL_BUCKETS // 2
    n = jnp.maximum(dist, 0)
    nf = jnp.maximum(n, max_exact).astype(jnp.float32)
    large = max_exact + (jnp.log(nf / max_exact) / math.log(REL_MAX_DIST / max_exact)
                         * (REL_BUCKETS - max_exact)).astype(jnp.int32)
    return jnp.where(n < max_exact, n, jnp.minimum(large, REL_BUCKETS - 1))


def moba_qkv(h, w_in):
    bsz, L, _ = h.shape
    q, k, v = jnp.split(h @ w_in, 3, axis=-1)
    return (q.reshape(bsz, L, MOBA_HEADS, MOBA_DH), k.reshape(bsz, L, MOBA_HEADS, MOBA_DH),
            v.reshape(bsz, L, MOBA_HEADS, MOBA_DH))


def moba_attention(q, k, v, q_pos, rel_bias):
    bsz, Lq, H, Dh = q.shape
    T = k.shape[1]
    nb = -(-T // MOBA_BLOCK)
    padt = nb * MOBA_BLOCK - T
    if padt:
        k = jnp.pad(k, ((0, 0), (0, padt), (0, 0), (0, 0)))
        v = jnp.pad(v, ((0, 0), (0, padt), (0, 0), (0, 0)))
    kb = k.reshape(bsz, nb, MOBA_BLOCK, H, Dh)
    vb = v.reshape(bsz, nb, MOBA_BLOCK, H, Dh)
    k_mean = jnp.mean(kb, axis=2, dtype=jnp.float32)
    kk = min(MOBA_TOPK, nb)
    qb = min(MOBA_QBLK, Lq)
    nq = -(-Lq // qb)
    padq = nq * qb - Lq
    qs = jnp.pad(q, ((0, 0), (0, padq), (0, 0), (0, 0))).reshape(bsz, nq, qb, H, Dh).transpose(1, 0, 2, 3, 4)
    ps = jnp.pad(q_pos, (0, padq), mode='edge').reshape(nq, qb)
    b_ix = jnp.arange(bsz)[:, None, None, None]
    h_ix = jnp.arange(H)[None, :, None, None]
    offs = jnp.arange(MOBA_BLOCK, dtype=jnp.int32)
    bias_tab = rel_bias.astype(jnp.float32)
    scale = Dh ** -0.5

    def step(inp):
        qc, pc = inp
        own = pc // MOBA_BLOCK
        gate = jnp.einsum('bqhd,bnhd->bhqn', qc, k_mean, preferred_element_type=jnp.float32)
        past = jnp.arange(nb)[None, :] < own[:, None]
        gate = jnp.where(past[None, None], gate, -jnp.inf)
        _, top = lax.top_k(gate, kk)
        idx = jnp.concatenate([top.astype(jnp.int32),
                               jnp.broadcast_to(own[None, None, :, None], (bsz, H, qb, 1)).astype(jnp.int32)], axis=-1)
        slot_ok = jnp.concatenate([jnp.arange(kk)[None, :] < own[:, None],
                                   jnp.ones((qb, 1), dtype=bool)], axis=-1)
        ksel = kb[b_ix, idx, :, h_ix]
        vsel = vb[b_ix, idx, :, h_ix]
        dist = pc[None, None, :, None, None] - (idx[..., None] * MOBA_BLOCK + offs)
        bias = bias_tab[t5_bucket(dist), h_ix[..., None]]
        logits = jnp.einsum('bqhd,bhqtsd->bhqts', qc, ksel, preferred_element_type=jnp.float32) * scale + bias
        mask = slot_ok[None, None, :, :, None] & (dist >= 0)
        logits = jnp.where(mask, logits, -jnp.inf)
        p = jax.nn.softmax(logits.reshape(bsz, H, qb, -1), axis=-1).reshape(logits.shape)
        return jnp.einsum('bhqts,bhqtsd->bqhd', p.astype(vsel.dtype), vsel)

    o = lax.map(step, (qs, ps))
    return o.transpose(1, 0, 2, 3, 4).reshape(bsz, nq * qb, H, Dh)[:, :Lq]


def gather_paged(cache_layer, page_table, new, pad):
    n_dec = page_table.shape[0]
    past = cache_layer[page_table].reshape(n_dec, -1, MOBA_HEADS, MOBA_DH).astype(new.dtype)
    zeros = jnp.zeros((n_dec, pad, MOBA_HEADS, MOBA_DH), new.dtype)
    return jnp.concatenate([past, new, zeros], axis=1)


def setup_inputs(seed: int = 0) -> dict:
    key = jax.random.key(seed)
    ks = jax.random.split(key, 32)
    f32 = jnp.float32

    def nrm(k, shape, scale):
        return jax.random.normal(k, shape, f32) * scale

    n_pages = PAST_LEN // PAGE_SIZE
    n_pool = (DEC_BATCH * n_pages * 5) // 4
    page_table = jax.random.permutation(ks[4], n_pool)[:DEC_BATCH * n_pages].reshape(DEC_BATCH, n_pages).astype(jnp.int32)
    return {
        'x_prompt': nrm(ks[0], (BATCH, SEQ, D_MODEL), 1.0),
        'x_sample': nrm(ks[1], (DEC_BATCH, DEC_SEQ, D_MODEL), 1.0),
        'cache_k': nrm(ks[2], (N_LAYERS_C, n_pool, PAGE_SIZE, MOBA_HEADS, MOBA_DH), 1.0),
        'cache_v': nrm(ks[3], (N_LAYERS_C, n_pool, PAGE_SIZE, MOBA_HEADS, MOBA_DH), 1.0),
        'page_table': page_table,
        'state_gla': nrm(ks[5], (N_LAYERS_B, DEC_BATCH, GLA_HEADS, GLA_DK, GLA_DV), 0.5),
        'norm_mix': 1.0 + nrm(ks[6], (DEPTH, D_MODEL), 0.02),
        'norm_ffn': 1.0 + nrm(ks[7], (DEPTH, D_MODEL), 0.02),
        'norm_final': 1.0 + nrm(ks[8], (D_MODEL,), 0.02),
        'w_in_a': nrm(ks[9], (N_LAYERS_A, D_MODEL, 2 * A_HALF), D_MODEL ** -0.5),
        'ln_a_g': 1.0 + nrm(ks[10], (N_LAYERS_A, A_HALF), 0.02),
        'ln_a_b': nrm(ks[11], (N_LAYERS_A, A_HALF), 0.02),
        'w_s_a': nrm(ks[12], (N_LAYERS_A, A_GROUPS, A_CHUNK, A_CHUNK), A_CHUNK ** -0.5),
        'b_s_a': 1.0 + nrm(ks[13], (N_LAYERS_A, A_GROUPS, A_CHUNK), 0.1),
        'w_out_a': nrm(ks[14], (N_LAYERS_A, A_HALF, D_MODEL), A_HALF ** -0.5),
        'w_in_b': nrm(ks[15], (N_LAYERS_B, D_MODEL, GLA_PROJ), D_MODEL ** -0.5),
        'w_gate_b': nrm(ks[16], (N_LAYERS_B, GLA_GATE_RANK, GLA_HEADS * GLA_DK), GLA_GATE_RANK ** -0.5),
        'b_gate_b': nrm(ks[17], (N_LAYERS_B, GLA_HEADS * GLA_DK), 0.1),
        'gn_b': 1.0 + nrm(ks[18], (N_LAYERS_B, GLA_DV), 0.02),
        'w_out_b': nrm(ks[19], (N_LAYERS_B, GLA_HEADS * GLA_DV, D_MODEL), (GLA_HEADS * GLA_DV) ** -0.5),
        'w_in_c': nrm(ks[20], (N_LAYERS_C, D_MODEL, 3 * MOBA_HEADS * MOBA_DH), D_MODEL ** -0.5),
        'w_out_c': nrm(ks[21], (N_LAYERS_C, MOBA_HEADS * MOBA_DH, D_MODEL), (MOBA_HEADS * MOBA_DH) ** -0.5),
        'rel_bias': nrm(ks[22], (REL_BUCKETS, MOBA_HEADS), 0.5),
        'w_up': nrm(ks[23], (DEPTH, D_MODEL, 2 * D_FF), D_MODEL ** -0.5),
        'w_down': nrm(ks[24], (DEPTH, D_FF, D_MODEL), D_FF ** -0.5),
    }


def reference(x_prompt, x_sample, cache_k, cache_v, page_table, state_gla,
              norm_mix, norm_ffn, norm_final,
              w_in_a, ln_a_g, ln_a_b, w_s_a, b_s_a, w_out_a,
              w_in_b, w_gate_b, b_gate_b, gn_b, w_out_b,
              w_in_c, w_out_c, rel_bias,
              w_up, w_down):
    past_len = page_table.shape[1] * PAGE_SIZE
    n_pr, l_pr = x_prompt.shape[0], x_prompt.shape[1]
    n_dec, l_dec = x_sample.shape[0], x_sample.shape[1]
    pos_p = jnp.arange(l_pr, dtype=jnp.int32)
    pos_s = past_len + jnp.arange(l_dec, dtype=jnp.int32)
    xp, xs = x_prompt, x_sample
    k_p, v_p, k_s, v_s, gla_p, gla_s, sgu_s = [], [], [], [], [], [], []
    for i in range(DEPTH):
        kind, j = i % N_MIXERS, i // N_MIXERS
        hp = rms_norm(xp, norm_mix[i])
        hs = rms_norm(xs, norm_mix[i])
        if kind == 0:
            mp, _ = sgu_mixer(hp, w_in_a[j], ln_a_g[j], ln_a_b[j], w_s_a[j], b_s_a[j], w_out_a[j])
            ms, v_rows = sgu_mixer(hs, w_in_a[j], ln_a_g[j], ln_a_b[j], w_s_a[j], b_s_a[j], w_out_a[j])
            sgu_s.append(v_rows)
        elif kind == 1:
            s_zero = jnp.zeros((n_pr, GLA_HEADS, GLA_DK, GLA_DV), xp.dtype)
            mp, sp = gla_mixer(hp, s_zero, w_in_b[j], w_gate_b[j], b_gate_b[j], gn_b[j], w_out_b[j])
            ms, ss = gla_mixer(hs, state_gla[j], w_in_b[j], w_gate_b[j], b_gate_b[j], gn_b[j], w_out_b[j])
            gla_p.append(sp)
            gla_s.append(ss)
        else:
            qp, kp, vp = moba_qkv(hp, w_in_c[j])
            op = moba_attention(qp, kp, vp, pos_p, rel_bias)
            qs, kn, vn = moba_qkv(hs, w_in_c[j])
            pad = (-(past_len + l_dec)) % MOBA_BLOCK
            os_ = moba_attention(qs, gather_paged(cache_k[j], page_table, kn, pad),
                                 gather_paged(cache_v[j], page_table, vn, pad), pos_s, rel_bias)
            mp = op.reshape(n_pr, l_pr, -1) @ w_out_c[j]
            ms = os_.reshape(n_dec, l_dec, -1) @ w_out_c[j]
            k_p.append(kp)
            v_p.append(vp)
            k_s.append(kn)
            v_s.append(vn)
        xp = xp + mp
        xs = xs + ms
        xp = xp + swiglu(rms_norm(xp, norm_ffn[i]), w_up[i], w_down[i])
        xs = xs + swiglu(rms_norm(xs, norm_ffn[i]), w_up[i], w_down[i])
    y_prompt = rms_norm(xp, norm_final)
    y_sample = rms_norm(xs, norm_final)
    return (y_prompt, y_sample, jnp.stack(k_p), jnp.stack(v_p), jnp.stack(k_s), jnp.stack(v_s),
            jnp.stack(gla_p), jnp.stack(gla_s), jnp.stack(sgu_s))
```

```python
import functools
import math

import numpy as np
import jax
import jax.numpy as jnp
from jax import lax
from jax.experimental import pallas as pl
from jax.experimental.pallas import tpu as pltpu

F32 = jnp.float32
BF16 = jnp.bfloat16

D_MODEL = 1024
EPS = 1e-6
NEG = -1e30

LANES = 128
SUBLANES = 8
MXU_DIM = 256
VMEM_LIMIT = 56 * 1024 * 1024

A_CHUNK = 128
A_HALF = 3 * D_MODEL
A_GROUPS = 8
A_GROUP_DIM = A_HALF // A_GROUPS
A_PANEL = 2 * A_GROUP_DIM
A_NPANEL = A_HALF // A_PANEL

GLA_HEADS = 4
GLA_DK = 128
GLA_DV = 256
GLA_GATE_RANK = 16
GLA_GATE_NORM = 16.0
GLA_C = 256

MOBA_HEADS = 16
MOBA_DH = 64
MOBA_BLOCK = 256
MOBA_TOPK = 3
PAGE_SIZE = 128
REL_BUCKETS = 32
REL_MAX_DIST = 128

D_FF = 2816
FF_CHUNK = MXU_DIM
FF_NCHUNK = D_FF // FF_CHUNK

NT_DIMS = (((1,), (1,)), ((), ()))
TN_DIMS = (((0,), (0,)), ((), ()))


def _cparams(sem):
    return pltpu.CompilerParams(dimension_semantics=sem, vmem_limit_bytes=VMEM_LIMIT)


def _resident(shape):
    nd = len(shape)
    return pl.BlockSpec(shape, lambda *_: (0,) * nd, pipeline_mode=pl.Buffered(1))


def _rms(x, g):
    return x * lax.rsqrt(jnp.mean(x * x, axis=-1, keepdims=True) + EPS) * g


def _gelu(x):
    return 0.5 * x * (1.0 + lax.erf(x * (1.0 / math.sqrt(2.0))))


def _silu(x):
    return x * jax.nn.sigmoid(x)


def _dot(a, b):
    return jnp.dot(a, b, preferred_element_type=F32)


def _ffn_body(*refs, has_proj, final):
    refs = list(refs)
    x_ref = refs.pop(0)
    a_ref = wo_ref = gf_ref = None
    if has_proj:
        a_ref = refs.pop(0)
        wo_ref = refs.pop(0)
    g_ref = refs.pop(0)
    wup_ref = refs.pop(0)
    wdn_ref = refs.pop(0)
    if final:
        gf_ref = refs.pop(0)
    o_ref = refs.pop(0)

    x = x_ref[...]
    if has_proj:
        x = x + _dot(a_ref[...].astype(BF16), wo_ref[...])
    h = _rms(x, g_ref[...]).astype(BF16)
    o_ref[...] = x

    def step(c, carry):
        gu = _dot(h, wup_ref[c])
        act = (_silu(gu[:, :FF_CHUNK]) * gu[:, FF_CHUNK:]).astype(BF16)
        o_ref[...] += _dot(act, wdn_ref[c])
        return carry

    lax.fori_loop(0, FF_NCHUNK, step, 0)
    if final:
        o_ref[...] = _rms(o_ref[...], gf_ref[...])


def _ffn(x, g, wup, wdn, tm, a=None, wo=None, gfinal=None):
    rows = x.shape[0]
    row_spec = pl.BlockSpec((tm, D_MODEL), lambda i: (i, 0))
    args = [x]
    specs = [row_spec]
    if a is not None:
        args += [a, wo]
        specs += [row_spec, _resident(wo.shape)]
    args += [g, wup, wdn]
    specs += [_resident(g.shape), _resident(wup.shape), _resident(wdn.shape)]
    if gfinal is not None:
        args.append(gfinal)
        specs.append(_resident(gfinal.shape))
    return pl.pallas_call(
        functools.partial(_ffn_body, has_proj=a is not None, final=gfinal is not None),
        grid=(rows // tm,),
        in_specs=specs,
        out_specs=row_spec,
        out_shape=jax.ShapeDtypeStruct((rows, D_MODEL), F32),
        compiler_params=_cparams(("parallel",)),
        name="ffn",
    )(*args)


def _sgu_body(*refs, tm, emit_v):
    (x_ref, g_ref, wu_ref, wv_ref, lng_ref, lnb_ref, wm_ref, sb_ref, wout_ref) = refs[:9]
    if emit_v:
        o_ref, vout_ref, vbuf, sbuf = refs[9:]
    else:
        o_ref, vbuf, sbuf = refs[9:]
        vout_ref = None

    x = x_ref[...]
    h = _rms(x, g_ref[...]).astype(BF16)

    rsum = jnp.zeros((tm, 1), F32)
    for p in range(A_NPANEL):
        v = _gelu(_dot(h, wv_ref[p]))
        vbuf[p] = v
        rsum = rsum + jnp.sum(v, axis=-1, keepdims=True)
    mean = rsum * (1.0 / A_HALF)
    ssq = jnp.zeros((tm, 1), F32)
    for p in range(A_NPANEL):
        d = vbuf[p] - mean
        ssq = ssq + jnp.sum(d * d, axis=-1, keepdims=True)
    rstd = lax.rsqrt(ssq * (1.0 / A_HALF) + EPS)

    acc = x
    for p in range(A_NPANEL):
        vn = (vbuf[p] - mean) * rstd * lng_ref[p] + lnb_ref[p]
        if emit_v:
            vout_ref[:, p * A_PANEL:(p + 1) * A_PANEL] = vn
        vnb = vn.astype(BF16)
        for c in range(tm // A_CHUNK):
            for gg in range(2):
                grp = 2 * p + gg
                rs = slice(c * A_CHUNK, (c + 1) * A_CHUNK)
                cs = slice(gg * A_GROUP_DIM, (gg + 1) * A_GROUP_DIM)
                sbuf[rs, cs] = _dot(wm_ref[grp], vnb[rs, cs]) + sb_ref[grp]
        u = _gelu(_dot(h, wu_ref[p]))
        acc = acc + _dot((u * sbuf[...]).astype(BF16), wout_ref[p])
    o_ref[...] = acc


def _sgu(x, g, wu, wv, lng, lnb, wm, sb, wout, tm, emit_v):
    rows = x.shape[0]
    row_spec = pl.BlockSpec((tm, D_MODEL), lambda i: (i, 0))
    weights = [g, wu, wv, lng, lnb, wm, sb, wout]
    out_shape = [jax.ShapeDtypeStruct((rows, D_MODEL), F32)]
    out_specs = [row_spec]
    if emit_v:
        out_shape.append(jax.ShapeDtypeStruct((rows, A_HALF), F32))
        out_specs.append(pl.BlockSpec((tm, A_HALF), lambda i: (i, 0)))
    return pl.pallas_call(
        functools.partial(_sgu_body, tm=tm, emit_v=emit_v),
        grid=(rows // tm,),
        in_specs=[row_spec] + [_resident(w.shape) for w in weights],
        out_specs=out_specs,
        out_shape=out_shape,
        scratch_shapes=[pltpu.VMEM((A_NPANEL, tm, A_PANEL), F32), pltpu.VMEM((tm, A_PANEL), F32)],
        compiler_params=_cparams(("parallel",)),
        name="sgu",
    )(x, *weights)


def _log_sigmoid(z):
    return jnp.minimum(z, 0.0) - jnp.log1p(jnp.exp(-jnp.abs(z)))


def _gla_intra(qh, kh, cum, same_block):
    c = qh.shape[0]
    r = lax.broadcasted_iota(jnp.int32, (c, 1), 0)
    a = jnp.zeros((c, c), F32)
    m = c
    while m >= 2:
        half = m // 2
        pos = r & (m - 1)
        upper = pos >= half
        if m >= 2 * SUBLANES:
            pieces = []
            for b in range(c // m):
                row = b * m + half - 1
                pieces.append(jnp.broadcast_to(cum[row:row + 1, :], (m, cum.shape[1])))
            ref_q = pieces[0] if len(pieces) == 1 else jnp.concatenate(pieces, axis=0)
            ref_k = ref_q
        else:
            ref_q = cum
            for d in range(1, half + 1):
                ref_q = jnp.where(pos - (half - 1) == d, pltpu.roll(cum, d, 0), ref_q)
            ref_k = cum
            for o in range(1, half):
                ref_k = jnp.where((half - 1) - pos == o, pltpu.roll(cum, c - o, 0), ref_k)
        e_q = jnp.where(upper, cum - ref_q, 0.0)
        f_k = jnp.where(upper, 0.0, ref_k - cum)
        qm = jnp.where(upper, qh * jnp.exp(e_q), 0.0).astype(BF16)
        km = jnp.where(upper, 0.0, kh * jnp.exp(f_k)).astype(BF16)
        am = lax.dot_general(qm, km, NT_DIMS, preferred_element_type=F32)
        if m < c:
            am = jnp.where(same_block[m], am, 0.0)
        a = a + am
        m = half
    return a


def _gla_prompt_body(x_ref, gm_ref, wq_ref, wk_ref, wv_ref, wg_ref, wgl_ref, wgate_ref, bgate_ref,
                     gn_ref, wo_ref, o_ref, st_ref, s_scr):
    ci = pl.program_id(1)
    c = GLA_C

    @pl.when(ci == 0)
    def _():
        s_scr[...] = jnp.zeros_like(s_scr)

    x = x_ref[...]
    h = _rms(x, gm_ref[...]).astype(BF16)
    q = _dot(h, wq_ref[...]) * (GLA_DK ** -0.5)
    k = _dot(h, wk_ref[...])
    v = _dot(h, wv_ref[...])
    g = _dot(h, wg_ref[...])
    gl = _dot(h, wgl_ref[...])
    z = _dot(gl.astype(BF16), wgate_ref[...]) + bgate_ref[...]
    la = _log_sigmoid(z) * (1.0 / GLA_GATE_NORM)

    rowi = lax.broadcasted_iota(jnp.int32, (c, c), 0)
    coli = lax.broadcasted_iota(jnp.int32, (c, c), 1)
    tril = (coli <= rowi).astype(F32)
    cum = jnp.dot(tril, la, precision=lax.Precision.HIGHEST, preferred_element_type=F32)

    same_block = {}
    m = c // 2
    while m >= 2:
        sh = int(math.log2(m))
        same_block[m] = (rowi >> sh) == (coli >> sh)
        m //= 2

    outs = []
    for hd in range(GLA_HEADS):
        ks = slice(hd * GLA_DK, (hd + 1) * GLA_DK)
        vs = slice(hd * GLA_DV, (hd + 1) * GLA_DV)
        qh, kh, cumh, vh = q[:, ks], k[:, ks], cum[:, ks], v[:, vs]
        vhb = vh.astype(BF16)
        total = cumh[c - 1:c, :]
        st = s_scr[hd]
        o = lax.dot_general((qh * jnp.exp(cumh)).astype(BF16), st.astype(BF16), NT_DIMS,
                            preferred_element_type=F32)
        a = _gla_intra(qh, kh, cumh, same_block)
        o = o + _dot(a.astype(BF16), vhb)
        o = o + jnp.sum(qh * kh, axis=-1, keepdims=True) * vh
        kdec = (kh * jnp.exp(total - cumh)).astype(BF16)
        st_new = st * jnp.exp(total) + lax.dot_general(vhb, kdec, TN_DIMS,
                                                       preferred_element_type=F32)
        s_scr[hd] = st_new
        o = o * lax.rsqrt(jnp.mean(o * o, axis=-1, keepdims=True) + EPS) * gn_ref[...]
        outs.append(o)
    y = jnp.concatenate(outs, axis=1) * _silu(g)
    o_ref[...] = x + _dot(y.astype(BF16), wo_ref[...])

    @pl.when(ci == pl.num_programs(1) - 1)
    def _():
        for hd in range(GLA_HEADS):
            st_ref[0, hd] = s_scr[hd].T


def _gla_prompt(x, gm, wq, wk, wv, wg, wgl, wgate, bgate, gn, wo, n_seq, seq_len):
    n_chunk = seq_len // GLA_C
    weights = [gm, wq, wk, wv, wg, wgl, wgate, bgate, gn, wo]
    row_spec = pl.BlockSpec((GLA_C, D_MODEL), lambda b, c: (b * n_chunk + c, 0))
    return pl.pallas_call(
        _gla_prompt_body,
        grid=(n_seq, n_chunk),
        in_specs=[row_spec] + [_resident(w.shape) for w in weights],
        out_specs=[row_spec,
                   pl.BlockSpec((1, GLA_HEADS, GLA_DK, GLA_DV), lambda b, c: (b, 0, 0, 0))],
        out_shape=[jax.ShapeDtypeStruct(x.shape, F32),
                   jax.ShapeDtypeStruct((n_seq, GLA_HEADS, GLA_DK, GLA_DV), F32)],
        scratch_shapes=[pltpu.VMEM((GLA_HEADS, GLA_DV, GLA_DK), F32)],
        compiler_params=_cparams(("parallel", "arbitrary")),
        name="gla_prompt",
    )(x, *weights)


def _gla_sample_proj_body(x_ref, gm_ref, wq_ref, wk_ref, wv_ref, wg_ref, wgl_ref, wgate_ref,
                          bgate_ref, q_ref, k_ref, a_ref, v_ref, g_ref):
    h = _rms(x_ref[...], gm_ref[...]).astype(BF16)
    q_ref[...] = _dot(h, wq_ref[...]) * (GLA_DK ** -0.5)
    k_ref[...] = _dot(h, wk_ref[...])
    v_ref[...] = _dot(h, wv_ref[...])
    g_ref[...] = _dot(h, wg_ref[...])
    gl = _dot(h, wgl_ref[...])
    z = _dot(gl.astype(BF16), wgate_ref[...]) + bgate_ref[...]
    a_ref[...] = jnp.exp(_log_sigmoid(z) * (1.0 / GLA_GATE_NORM))


GLA_SEQ_PER_STEP = SUBLANES


def _gla_sample_rec_body(s_ref, qc_ref, kc_ref, ac_ref, v_ref, so_ref, o_ref):
    for j in range(GLA_SEQ_PER_STEP):
        for hd in range(GLA_HEADS):
            ks = slice(hd * GLA_DK, (hd + 1) * GLA_DK)
            vs = slice(hd * GLA_DV, (hd + 1) * GLA_DV)
            acol = ac_ref[0, ks, j:j + 1]
            kcol = kc_ref[0, ks, j:j + 1]
            qcol = qc_ref[0, ks, j:j + 1]
            vrow = v_ref[j:j + 1, vs]
            s_new = s_ref[j, hd] * acol + kcol * vrow
            so_ref[j, hd] = s_new
            o_ref[j:j + 1, vs] = jnp.sum(s_new * qcol, axis=0, keepdims=True)


def _gla_sample_out_body(x_ref, o_ref, g_ref, gn_ref, wo_ref, y_ref):
    outs = []
    for hd in range(GLA_HEADS):
        o = o_ref[:, hd * GLA_DV:(hd + 1) * GLA_DV]
        outs.append(o * lax.rsqrt(jnp.mean(o * o, axis=-1, keepdims=True) + EPS) * gn_ref[...])
    y = jnp.concatenate(outs, axis=1) * _silu(g_ref[...])
    y_ref[...] = x_ref[...] + _dot(y.astype(BF16), wo_ref[...])


def _gla_sample(x, state, gm, wq, wk, wv, wg, wgl, wgate, bgate, gn, wo):
    n = x.shape[0]
    dq = GLA_HEADS * GLA_DK
    weights = [gm, wq, wk, wv, wg, wgl, wgate, bgate]
    q, k, a, v, g = pl.pallas_call(
        _gla_sample_proj_body,
        out_shape=[jax.ShapeDtypeStruct((n, dq), F32)] * 3
        + [jax.ShapeDtypeStruct((n, D_MODEL), F32)] * 2,
        compiler_params=pltpu.CompilerParams(vmem_limit_bytes=VMEM_LIMIT),
        name="gla_sample_proj",
    )(x, *weights)

    nstep = n // GLA_SEQ_PER_STEP

    def cols(t):
        return t.reshape(nstep, GLA_SEQ_PER_STEP, dq).transpose(0, 2, 1)

    col_spec = pl.BlockSpec((1, dq, GLA_SEQ_PER_STEP), lambda i: (i, 0, 0))
    st_spec = pl.BlockSpec((GLA_SEQ_PER_STEP, GLA_HEADS, GLA_DK, GLA_DV), lambda i: (i, 0, 0, 0))
    row_spec = pl.BlockSpec((GLA_SEQ_PER_STEP, D_MODEL), lambda i: (i, 0))
    s_new, o = pl.pallas_call(
        _gla_sample_rec_body,
        grid=(nstep,),
        in_specs=[st_spec, col_spec, col_spec, col_spec, row_spec],
        out_specs=[st_spec, row_spec],
        out_shape=[jax.ShapeDtypeStruct(state.shape, F32), jax.ShapeDtypeStruct((n, D_MODEL), F32)],
        compiler_params=_cparams(("parallel",)),
        name="gla_sample_rec",
    )(state, cols(q), cols(k), cols(a), v)

    y = pl.pallas_call(
        _gla_sample_out_body,
        out_shape=jax.ShapeDtypeStruct((n, D_MODEL), F32),
        compiler_params=pltpu.CompilerParams(vmem_limit_bytes=VMEM_LIMIT),
        name="gla_sample_out",
    )(x, o, g, gn, wo)
    return y, s_new


def _t5_bucket_np(dist):
    max_exact = REL_BUCKETS // 2
    n = np.maximum(dist, 0)
    nf = np.maximum(n, max_exact).astype(np.float32)
    large = max_exact + (np.log(nf / np.float32(max_exact)) / np.float32(math.log(REL_MAX_DIST / max_exact))
                         * np.float32(REL_BUCKETS - max_exact)).astype(np.int32)
    return np.where(n < max_exact, n, np.minimum(large, REL_BUCKETS - 1)).astype(np.int32)


def _bias_tables_body(rb_ref, bd_ref, b1_ref, bdec_ref, od_ref, o1_ref, odec_ref):
    h = pl.program_id(0)
    td, t1, tdec = bd_ref[...], b1_ref[...], bdec_ref[...]
    ad = jnp.zeros(td.shape, F32)
    a1 = jnp.zeros(t1.shape, F32)
    adec = jnp.zeros(tdec.shape, F32)
    for b in range(REL_BUCKETS):
        val = rb_ref[b, h]
        ad = jnp.where(td == b, val, ad)
        a1 = jnp.where(t1 == b, val, a1)
        adec = jnp.where(tdec == b, val, adec)
    od_ref[0] = ad
    o1_ref[0] = a1
    odec_ref[0] = adec


def _bias_tables(rel_bias, past_len):
    kc = np.arange(MOBA_BLOCK)[:, None]
    qr = np.arange(MOBA_BLOCK)[None, :]
    bd = _t5_bucket_np(qr - kc)
    b1 = _t5_bucket_np(MOBA_BLOCK + qr - kc)
    nblk = past_len // MOBA_BLOCK
    kpos = np.arange(nblk)[:, None] * MOBA_BLOCK + np.arange(MOBA_BLOCK)[None, :]
    bdec = _t5_bucket_np(past_len - kpos)
    full = lambda a: pl.BlockSpec(a.shape, lambda h: (0,) * a.ndim)
    return pl.pallas_call(
        _bias_tables_body,
        grid=(MOBA_HEADS,),
        in_specs=[pl.BlockSpec(memory_space=pltpu.SMEM), full(bd), full(b1), full(bdec)],
        out_specs=[pl.BlockSpec((1,) + bd.shape, lambda h: (h, 0, 0)),
                   pl.BlockSpec((1,) + b1.shape, lambda h: (h, 0, 0)),
                   pl.BlockSpec((1,) + bdec.shape, lambda h: (h, 0, 0))],
        out_shape=[jax.ShapeDtypeStruct((MOBA_HEADS,) + bd.shape, F32),
                   jax.ShapeDtypeStruct((MOBA_HEADS,) + b1.shape, F32),
                   jax.ShapeDtypeStruct((MOBA_HEADS,) + bdec.shape, F32)],
        compiler_params=_cparams(("parallel",)),
        name="moba_bias_tables",
    )(rel_bias, jnp.asarray(bd), jnp.asarray(b1), jnp.asarray(bdec))


def _moba_qkv_body(x_ref, g_ref, w_ref, q_ref, k_ref, v_ref):
    h = _rms(x_ref[...], g_ref[...]).astype(BF16)
    qkv = _dot(h, w_ref[...])
    q_ref[...] = qkv[:, :D_MODEL] * (MOBA_DH ** -0.5)
    k_ref[...] = qkv[:, D_MODEL:2 * D_MODEL]
    v_ref[...] = qkv[:, 2 * D_MODEL:]


def _moba_qkv(x, g, w, tm):
    rows = x.shape[0]
    row_spec = pl.BlockSpec((tm, D_MODEL), lambda i: (i, 0))
    return pl.pallas_call(
        _moba_qkv_body,
        grid=(rows // tm,),
        in_specs=[row_spec, _resident(g.shape), _resident(w.shape)],
        out_specs=[row_spec] * 3,
        out_shape=[jax.ShapeDtypeStruct((rows, D_MODEL), F32)] * 3,
        compiler_params=_cparams(("parallel",)),
        name="moba_qkv",
    )(x, g, w)


def _topk_select(gate, valid, axis):
    n = gate.shape[axis]
    idx = lax.broadcasted_iota(jnp.int32, gate.shape, axis)
    cnt = jnp.zeros(gate.shape, jnp.int32)
    for m in range(n):
        gm = lax.slice_in_dim(gate, m, m + 1, axis=axis)
        beats = (gm > gate) | ((gm == gate) & (m < idx))
        cnt = cnt + jnp.where(beats, valid(m), 0)
    return jnp.where(cnt < MOBA_TOPK, 1.0, 0.0)


def _moba_prompt_body(rb_ref, q_ref, k_ref, v_ref, bd_ref, b1_ref, o_ref,
                      kb_scr, vt_scr, km_scr, neg_scr, m_scr, l_scr, acc_scr, ot_scr):
    hp = pl.program_id(1)
    qi = pl.program_id(2)
    nblk = kb_scr.shape[0]
    blk = MOBA_BLOCK

    @pl.when(qi == 0)
    def _():
        for n in range(nblk):
            kt = k_ref[n * blk:(n + 1) * blk, :]
            kb_scr[n] = kt.astype(BF16)
            km_scr[n:n + 1, :] = jnp.mean(kt, axis=0, keepdims=True)
            vt_scr[n] = v_ref[n * blk:(n + 1) * blk, :].T.astype(BF16)

    q2 = q_ref[...]
    lane = lax.broadcasted_iota(jnp.int32, (1, 2 * MOBA_DH), 1)
    ki_idx = lax.broadcasted_iota(jnp.int32, (blk, blk), 0)
    qi_idx = lax.broadcasted_iota(jnp.int32, (blk, blk), 1)
    blk_idx = lax.broadcasted_iota(jnp.int32, (nblk, blk), 0)

    for hh in range(2):
        in_head = (lane < MOBA_DH) if hh == 0 else (lane >= MOBA_DH)
        qh = jnp.where(in_head, q2, 0.0)
        qhb = qh.astype(BF16)
        far_bias = rb_ref[REL_BUCKETS - 1, 2 * hp + hh]

        kmh = jnp.where(in_head, km_scr[...], 0.0)
        gate = lax.dot_general(kmh, qh, NT_DIMS, precision=lax.Precision.HIGHEST,
                               preferred_element_type=F32)
        keep = _topk_select(gate, lambda m: (m < qi).astype(jnp.int32), 0)
        neg_scr[...] = jnp.where((keep > 0.0) & (blk_idx < qi), 0.0, NEG)

        s = lax.dot_general(kb_scr[qi], qhb, NT_DIMS, preferred_element_type=F32) + bd_ref[hh]
        s = jnp.where(ki_idx <= qi_idx, s, NEG)
        m0 = jnp.max(s, axis=0, keepdims=True)
        p = jnp.exp(s - m0)
        m_scr[...] = m0
        l_scr[...] = jnp.sum(p, axis=0, keepdims=True)
        acc_scr[...] = _dot(vt_scr[qi], p.astype(BF16))

        def past_tile(ki, bias):
            s = lax.dot_general(kb_scr[ki], qhb, NT_DIMS, preferred_element_type=F32)
            s = s + bias + neg_scr[pl.ds(ki, 1), :]
            m_old = m_scr[...]
            m_new = jnp.maximum(m_old, jnp.max(s, axis=0, keepdims=True))
            alpha = jnp.exp(m_old - m_new)
            p = jnp.exp(s - m_new)
            m_scr[...] = m_new
            l_scr[...] = alpha * l_scr[...] + jnp.sum(p, axis=0, keepdims=True)
            acc_scr[...] = alpha * acc_scr[...] + _dot(vt_scr[ki], p.astype(BF16))

        @pl.when(qi >= 1)
        def _():
            past_tile(qi - 1, b1_ref[hh])

        def far_step(ki, carry):
            past_tile(ki, far_bias)
            return carry

        lax.fori_loop(0, jnp.maximum(qi - 1, 0), far_step, 0)

        rows = slice(hh * MOBA_DH, (hh + 1) * MOBA_DH)
        ot_scr[rows, :] = (acc_scr[...] / l_scr[...])[rows, :]

    o_ref[...] = ot_scr[...].T


def _moba_prompt(rel_bias, q, k, v, bias_d, bias_1, n_seq, seq_len):
    nblk = seq_len // MOBA_BLOCK
    pair = 2 * MOBA_DH
    q_spec = pl.BlockSpec((MOBA_BLOCK, pair), lambda b, hp, qi: (b * nblk + qi, hp))
    kv_spec = pl.BlockSpec((seq_len, pair), lambda b, hp, qi: (b, hp))
    bias_spec = pl.BlockSpec((2, MOBA_BLOCK, MOBA_BLOCK), lambda b, hp, qi: (hp, 0, 0))
    return pl.pallas_call(
        _moba_prompt_body,
        grid=(n_seq, MOBA_HEADS // 2, nblk),
        in_specs=[pl.BlockSpec(memory_space=pltpu.SMEM), q_spec, kv_spec, kv_spec, bias_spec, bias_spec],
        out_specs=q_spec,
        out_shape=jax.ShapeDtypeStruct(q.shape, F32),
        scratch_shapes=[
            pltpu.VMEM((nblk, MOBA_BLOCK, pair), BF16),
            pltpu.VMEM((nblk, pair, MOBA_BLOCK), BF16),
            pltpu.VMEM((nblk, pair), F32),
            pltpu.VMEM((nblk, MOBA_BLOCK), F32),
            pltpu.VMEM((1, MOBA_BLOCK), F32),
            pltpu.VMEM((1, MOBA_BLOCK), F32),
            pltpu.VMEM((pair, MOBA_BLOCK), F32),
            pltpu.VMEM((pair, MOBA_BLOCK), F32),
        ],
        compiler_params=_cparams(("parallel", "parallel", "arbitrary")),
        name="moba_prompt",
    )(rel_bias, q, k, v, bias_d, bias_1)


def _moba_decode_body(pt_ref, q_ref, kn_ref, vn_ref, rb0_ref, ck0_ref, ck1_ref, cv0_ref, cv1_ref,
                      bdec_ref, o_ref, km_scr, m_scr, l_scr, o_scr):
    del pt_ref
    n = pl.program_id(1)
    nblk = pl.num_programs(1)
    width = MOBA_HEADS * MOBA_DH
    hidx = lax.broadcasted_iota(jnp.int32, (MOBA_HEADS, width), 0)
    jidx = lax.broadcasted_iota(jnp.int32, (MOBA_HEADS, width), 1)
    own = (jidx >> int(math.log2(MOBA_DH))) == hidx
    qbd = jnp.where(own, jnp.broadcast_to(q_ref[0], (MOBA_HEADS, width)), 0.0)

    k2 = jnp.concatenate([ck0_ref[0], ck1_ref[0]], axis=0)
    v2 = jnp.concatenate([cv0_ref[0], cv1_ref[0]], axis=0)
    km_scr[pl.ds(n, 1), :] = jnp.mean(k2, axis=0, keepdims=True)
    s = lax.dot_general(qbd.astype(BF16), k2.astype(BF16), NT_DIMS, preferred_element_type=F32)
    s = s + bdec_ref[0]
    m = jnp.max(s, axis=-1, keepdims=True)
    p = jnp.exp(s - m)
    m_scr[n] = m
    l_scr[n] = jnp.sum(p, axis=-1, keepdims=True)
    o_scr[n] = _dot(p.astype(BF16), v2.astype(BF16))

    @pl.when(n == nblk - 1)
    def _():
        nb = km_scr.shape[0]
        gate = lax.dot_general(qbd, km_scr[...], NT_DIMS, precision=lax.Precision.HIGHEST,
                               preferred_element_type=F32)
        keep = _topk_select(gate, lambda m_: 1, 1)
        l_new = jnp.sum(qbd * kn_ref[0], axis=-1, keepdims=True) + rb0_ref[...]
        m_all = l_new
        for b in range(nb):
            m_all = jnp.maximum(m_all, jnp.where(keep[:, b:b + 1] > 0.0, m_scr[b], NEG))
        w_new = jnp.exp(l_new - m_all)
        den = w_new
        num = w_new * jnp.where(own, jnp.broadcast_to(vn_ref[0], (MOBA_HEADS, width)), 0.0)
        for b in range(nb):
            w = jnp.where(keep[:, b:b + 1] > 0.0, jnp.exp(m_scr[b] - m_all), 0.0)
            den = den + w * l_scr[b]
            num = num + w * o_scr[b]
        out = jnp.where(own, num / den, 0.0)
        o_ref[0] = jnp.sum(out, axis=0, keepdims=True)


def _moba_decode(page_table, q, k_new, v_new, rb0, cache_k, cache_v, bias_dec):
    n_dec, n_pages = page_table.shape
    width = MOBA_HEADS * MOBA_DH
    ppb = MOBA_BLOCK // PAGE_SIZE
    nblk = n_pages // ppb
    row3 = lambda t: t.reshape(n_dec, 1, width)
    row_spec = pl.BlockSpec((1, 1, width), lambda b, n, pt: (b, 0, 0))
    page_spec = lambda j: pl.BlockSpec((1, PAGE_SIZE, width), lambda b, n, pt: (pt[b, ppb * n + j], 0, 0))
    grid_spec = pltpu.PrefetchScalarGridSpec(
        num_scalar_prefetch=1,
        grid=(n_dec, nblk),
        in_specs=[row_spec, row_spec, row_spec,
                  pl.BlockSpec((MOBA_HEADS, 1), lambda b, n, pt: (0, 0)),
                  page_spec(0), page_spec(1), page_spec(0), page_spec(1),
                  pl.BlockSpec((1, MOBA_HEADS, MOBA_BLOCK), lambda b, n, pt: (n, 0, 0))],
        out_specs=row_spec,
        scratch_shapes=[pltpu.VMEM((nblk, width), F32),
                        pltpu.VMEM((nblk, MOBA_HEADS, 1), F32),
                        pltpu.VMEM((nblk, MOBA_HEADS, 1), F32),
                        pltpu.VMEM((nblk, MOBA_HEADS, width), F32)],
    )
    out = pl.pallas_call(
        _moba_decode_body,
        grid_spec=grid_spec,
        out_shape=jax.ShapeDtypeStruct((n_dec, 1, width), F32),
        compiler_params=_cparams(("parallel", "arbitrary")),
        name="moba_decode",
    )(page_table, row3(q), row3(k_new), row3(v_new), rb0, cache_k, cache_k, cache_v, cache_v, bias_dec)
    return out.reshape(n_dec, width)


def _prep_ffn(w_up, w_down):
    wg = w_up[:, :D_FF].reshape(D_MODEL, FF_NCHUNK, FF_CHUNK)
    wu = w_up[:, D_FF:].reshape(D_MODEL, FF_NCHUNK, FF_CHUNK)
    wup = jnp.concatenate([wg, wu], axis=-1).transpose(1, 0, 2).astype(BF16)
    wdn = w_down.reshape(FF_NCHUNK, FF_CHUNK, D_MODEL).astype(BF16)
    return wup, wdn


def _prep_sgu(w_in, ln_g, ln_b, w_s, b_s, w_out):
    panels = lambda w: w.reshape(D_MODEL, A_NPANEL, A_PANEL).transpose(1, 0, 2).astype(BF16)
    wu = panels(w_in[:, :A_HALF])
    wv = panels(w_in[:, A_HALF:])
    lng = ln_g.reshape(A_NPANEL, 1, A_PANEL)
    lnb = ln_b.reshape(A_NPANEL, 1, A_PANEL)
    wout = w_out.reshape(A_NPANEL, A_PANEL, D_MODEL).astype(BF16)
    causal = jnp.tril(jnp.ones((A_CHUNK, A_CHUNK), dtype=bool))
    wm_p = jnp.where(causal[None], w_s, 0.0).astype(BF16)
    sb_p = jnp.broadcast_to(b_s[:, :, None], (A_GROUPS, A_CHUNK, A_GROUP_DIM))
    eye = jnp.eye(A_CHUNK, dtype=F32)
    wm_s = (w_s[:, 0, 0][:, None, None] * eye[None]).astype(BF16)
    sb_s = jnp.broadcast_to(b_s[:, 0][:, None, None], (A_GROUPS, A_CHUNK, A_GROUP_DIM))
    return (wu, wv, lng, lnb, wout), (wm_p, sb_p), (wm_s, sb_s)


def _prep_gla(w_in, w_gate, b_gate, gn, w_out):
    dq, dv = GLA_HEADS * GLA_DK, GLA_HEADS * GLA_DV
    wq = w_in[:, :dq].astype(BF16)
    wk = w_in[:, dq:2 * dq].astype(BF16)
    wv = w_in[:, 2 * dq:2 * dq + dv].astype(BF16)
    wg = w_in[:, 2 * dq + dv:2 * dq + 2 * dv].astype(BF16)
    pad = LANES - GLA_GATE_RANK
    wgl = jnp.pad(w_in[:, 2 * dq + 2 * dv:], ((0, 0), (0, pad))).astype(BF16)
    wgate = jnp.pad(w_gate, ((0, pad), (0, 0))).astype(BF16)
    return wq, wk, wv, wg, wgl, wgate, b_gate.reshape(1, dq), gn.reshape(1, GLA_DV), w_out.astype(BF16)


def kernel(x_prompt, x_sample, cache_k, cache_v, page_table, state_gla, norm_mix, norm_ffn, norm_final,
           w_in_a, ln_a_g, ln_a_b, w_s_a, b_s_a, w_out_a, w_in_b, w_gate_b, b_gate_b, gn_b, w_out_b,
           w_in_c, w_out_c, rel_bias, w_up, w_down):
    n_pr, l_pr, _ = x_prompt.shape
    n_dec, l_dec, _ = x_sample.shape
    assert l_dec == 1 and l_pr % MOBA_BLOCK == 0 and l_pr % GLA_C == 0
    depth = norm_mix.shape[0]
    past_len = page_table.shape[1] * PAGE_SIZE
    assert past_len % MOBA_BLOCK == 0
    tm_p, tm_s = 512, n_dec

    xp = x_prompt.reshape(n_pr * l_pr, D_MODEL)
    xs = x_sample.reshape(n_dec, D_MODEL)
    gfin = norm_final.reshape(1, D_MODEL)
    k_p, v_p, k_s, v_s, gla_p, gla_s, sgu_s = [], [], [], [], [], [], []

    for i in range(depth):
        kind, j = i % 3, i // 3
        gm = norm_mix[i].reshape(1, D_MODEL)
        gf = norm_ffn[i].reshape(1, D_MODEL)
        wup, wdn = _prep_ffn(w_up[i], w_down[i])
        last = gfin if i == depth - 1 else None
        proj_p = proj_s = None
        if kind == 0:
            shared, mode_p, mode_s = _prep_sgu(w_in_a[j], ln_a_g[j], ln_a_b[j], w_s_a[j], b_s_a[j], w_out_a[j])
            wu, wv, lng, lnb, wout = shared
            (xp,) = _sgu(xp, gm, wu, wv, lng, lnb, *mode_p, wout, tm_p, False)
            xs, v_rows = _sgu(xs, gm, wu, wv, lng, lnb, *mode_s, wout, tm_s, True)
            sgu_s.append(v_rows.reshape(n_dec, l_dec, A_HALF))
        elif kind == 1:
            wts = _prep_gla(w_in_b[j], w_gate_b[j], b_gate_b[j], gn_b[j], w_out_b[j])
            xp, sp = _gla_prompt(xp, gm, *wts, n_pr, l_pr)
            xs, ss = _gla_sample(xs, state_gla[j], gm, *wts)
            gla_p.append(sp)
            gla_s.append(ss)
        else:
            w_in = w_in_c[j].astype(BF16)
            bias_d, bias_1, bias_dec = _bias_tables(rel_bias, past_len)
            qp, kp, vp = _moba_qkv(xp, gm, w_in, tm_p)
            op = _moba_prompt(rel_bias, qp, kp, vp, bias_d, bias_1, n_pr, l_pr)
            qs, kn, vn = _moba_qkv(xs, gm, w_in, tm_s)
            width = MOBA_HEADS * MOBA_DH
            os_ = _moba_decode(page_table, qs, kn, vn, rel_bias[0].reshape(MOBA_HEADS, 1),
                               cache_k[j].reshape(-1, PAGE_SIZE, width),
                               cache_v[j].reshape(-1, PAGE_SIZE, width),
                               bias_dec.transpose(1, 0, 2))
            wo_c = w_out_c[j].astype(BF16)
            proj_p, proj_s = (op, wo_c), (os_, wo_c)
            k_p.append(kp.reshape(n_pr, l_pr, MOBA_HEADS, MOBA_DH))
            v_p.append(vp.reshape(n_pr, l_pr, MOBA_HEADS, MOBA_DH))
            k_s.append(kn.reshape(n_dec, l_dec, MOBA_HEADS, MOBA_DH))
            v_s.append(vn.reshape(n_dec, l_dec, MOBA_HEADS, MOBA_DH))
        if proj_p is None:
            xp = _ffn(xp, gf, wup, wdn, tm_p, gfinal=last)
            xs = _ffn(xs, gf, wup, wdn, tm_s, gfinal=last)
        else:
            xp = _ffn(xp, gf, wup, wdn, tm_p, a=proj_p[0], wo=proj_p[1], gfinal=last)
            xs = _ffn(xs, gf, wup, wdn, tm_s, a=proj_s[0], wo=proj_s[1], gfinal=last)

    return (xp.reshape(n_pr, l_pr, D_MODEL), xs.reshape(n_dec, l_dec, D_MODEL),
            jnp.stack(k_p), jnp.stack(v_p), jnp.stack(k_s), jnp.stack(v_s),
            jnp.stack(gla_p), jnp.stack(gla_s), jnp.stack(sgu_s))
```

```python
import functools
import math

import numpy as np
import jax
import jax.numpy as jnp
from jax import lax
from jax.experimental import pallas as pl
from jax.experimental.pallas import tpu as pltpu

F32 = jnp.float32
BF16 = jnp.bfloat16

D_MODEL = 1024
EPS = 1e-6
NEG = -1e30

LANES = 128
SUBLANES = 8
MXU_DIM = 256
VMEM_LIMIT = 56 * 1024 * 1024

A_CHUNK = 128
A_HALF = 3 * D_MODEL
A_GROUPS = 8
A_GROUP_DIM = A_HALF // A_GROUPS
A_PANEL = 2 * A_GROUP_DIM
A_NPANEL = A_HALF // A_PANEL

GLA_HEADS = 4
GLA_DK = 128
GLA_DV = 256
GLA_GATE_RANK = 16
GLA_GATE_NORM = 16.0
GLA_C = 256

MOBA_HEADS = 16
MOBA_DH = 64
MOBA_BLOCK = 256
MOBA_TOPK = 3
MOBA_HG = MXU_DIM // MOBA_DH
PAGE_SIZE = 128
REL_BUCKETS = 32
REL_MAX_DIST = 128

D_FF = 2816
FF_CHUNK = MXU_DIM
FF_NCHUNK = D_FF // FF_CHUNK

NT_DIMS = (((1,), (1,)), ((), ()))
TN_DIMS = (((0,), (0,)), ((), ()))


def _cparams(sem):
    return pltpu.CompilerParams(dimension_semantics=sem, vmem_limit_bytes=VMEM_LIMIT)


def _resident(shape):
    nd = len(shape)
    return pl.BlockSpec(shape, lambda *_: (0,) * nd, pipeline_mode=pl.Buffered(1))


def _rms(x, g):
    return x * lax.rsqrt(jnp.mean(x * x, axis=-1, keepdims=True) + EPS) * g


def _gelu(x):
    return 0.5 * x * (1.0 + lax.erf(x * (1.0 / math.sqrt(2.0))))


def _silu(x):
    return x * jax.nn.sigmoid(x)


def _dot(a, b):
    return jnp.dot(a, b, preferred_element_type=F32)


def _ffn_body(*refs, has_proj, final):
    refs = list(refs)
    x_ref = refs.pop(0)
    a_ref = wo_ref = gf_ref = None
    if has_proj:
        a_ref = refs.pop(0)
        wo_ref = refs.pop(0)
    g_ref = refs.pop(0)
    wup_ref = refs.pop(0)
    wdn_ref = refs.pop(0)
    if final:
        gf_ref = refs.pop(0)
    o_ref = refs.pop(0)

    x = x_ref[...]
    if has_proj:
        x = x + _dot(a_ref[...].astype(BF16), wo_ref[...])
    h = _rms(x, g_ref[...]).astype(BF16)
    o_ref[...] = x

    def step(c, carry):
        gu = _dot(h, wup_ref[c])
        act = (_silu(gu[:, :FF_CHUNK]) * gu[:, FF_CHUNK:]).astype(BF16)
        o_ref[...] += _dot(act, wdn_ref[c])
        return carry

    lax.fori_loop(0, FF_NCHUNK, step, 0)
    if final:
        o_ref[...] = _rms(o_ref[...], gf_ref[...])


def _ffn(x, g, wup, wdn, tm, a=None, wo=None, gfinal=None):
    rows = x.shape[0]
    row_spec = pl.BlockSpec((tm, D_MODEL), lambda i: (i, 0))
    args = [x]
    specs = [row_spec]
    if a is not None:
        args += [a, wo]
        specs += [row_spec, _resident(wo.shape)]
    args += [g, wup, wdn]
    specs += [_resident(g.shape), _resident(wup.shape), _resident(wdn.shape)]
    if gfinal is not None:
        args.append(gfinal)
        specs.append(_resident(gfinal.shape))
    return pl.pallas_call(
        functools.partial(_ffn_body, has_proj=a is not None, final=gfinal is not None),
        grid=(rows // tm,),
        in_specs=specs,
        out_specs=row_spec,
        out_shape=jax.ShapeDtypeStruct((rows, D_MODEL), F32),
        compiler_params=_cparams(("parallel",)),
        name="ffn",
    )(*args)


def _sgu_body(*refs, tm, emit_v):
    (x_ref, g_ref, wu_ref, wv_ref, lng_ref, lnb_ref, wm_ref, sb_ref, wout_ref) = refs[:9]
    if emit_v:
        o_ref, vout_ref, vbuf, sbuf = refs[9:]
    else:
        o_ref, vbuf, sbuf = refs[9:]
        vout_ref = None

    x = x_ref[...]
    h = _rms(x, g_ref[...]).astype(BF16)

    rsum = jnp.zeros((tm, 1), F32)
    for p in range(A_NPANEL):
        v = _gelu(_dot(h, wv_ref[p]))
        vbuf[p] = v
        rsum = rsum + jnp.sum(v, axis=-1, keepdims=True)
    mean = rsum * (1.0 / A_HALF)
    ssq = jnp.zeros((tm, 1), F32)
    for p in range(A_NPANEL):
        d = vbuf[p] - mean
        ssq = ssq + jnp.sum(d * d, axis=-1, keepdims=True)
    rstd = lax.rsqrt(ssq * (1.0 / A_HALF) + EPS)

    acc = x
    for p in range(A_NPANEL):
        vn = (vbuf[p] - mean) * rstd * lng_ref[p] + lnb_ref[p]
        if emit_v:
            vout_ref[:, p * A_PANEL:(p + 1) * A_PANEL] = vn
        vnb = vn.astype(BF16)
        for c in range(tm // A_CHUNK):
            for gg in range(2):
                grp = 2 * p + gg
                rs = slice(c * A_CHUNK, (c + 1) * A_CHUNK)
                cs = slice(gg * A_GROUP_DIM, (gg + 1) * A_GROUP_DIM)
                sbuf[rs, cs] = _dot(wm_ref[grp], vnb[rs, cs]) + sb_ref[grp]
        u = _gelu(_dot(h, wu_ref[p]))
        acc = acc + _dot((u * sbuf[...]).astype(BF16), wout_ref[p])
    o_ref[...] = acc


def _sgu(x, g, wu, wv, lng, lnb, wm, sb, wout, tm, emit_v):
    rows = x.shape[0]
    row_spec = pl.BlockSpec((tm, D_MODEL), lambda i: (i, 0))
    weights = [g, wu, wv, lng, lnb, wm, sb, wout]
    out_shape = [jax.ShapeDtypeStruct((rows, D_MODEL), F32)]
    out_specs = [row_spec]
    if emit_v:
        out_shape.append(jax.ShapeDtypeStruct((rows, A_HALF), F32))
        out_specs.append(pl.BlockSpec((tm, A_HALF), lambda i: (i, 0)))
    return pl.pallas_call(
        functools.partial(_sgu_body, tm=tm, emit_v=emit_v),
        grid=(rows // tm,),
        in_specs=[row_spec] + [_resident(w.shape) for w in weights],
        out_specs=out_specs,
        out_shape=out_shape,
        scratch_shapes=[pltpu.VMEM((A_NPANEL, tm, A_PANEL), F32), pltpu.VMEM((tm, A_PANEL), F32)],
        compiler_params=_cparams(("parallel",)),
        name="sgu",
    )(x, *weights)


def _log_sigmoid(z):
    return jnp.minimum(z, 0.0) - jnp.log1p(jnp.exp(-jnp.abs(z)))


def _gla_intra(qh, kh, cum, same_block):
    c = qh.shape[0]
    r = lax.broadcasted_iota(jnp.int32, (c, 1), 0)
    a = jnp.zeros((c, c), F32)
    m = c
    while m >= 2:
        half = m // 2
        pos = r & (m - 1)
        upper = pos >= half
        if m >= 2 * SUBLANES:
            pieces = []
            for b in range(c // m):
                row = b * m + half - 1
                pieces.append(jnp.broadcast_to(cum[row:row + 1, :], (m, cum.shape[1])))
            ref_q = pieces[0] if len(pieces) == 1 else jnp.concatenate(pieces, axis=0)
            ref_k = ref_q
        else:
            ref_q = cum
            for d in range(1, half + 1):
                ref_q = jnp.where(pos - (half - 1) == d, pltpu.roll(cum, d, 0), ref_q)
            ref_k = cum
            for o in range(1, half):
                ref_k = jnp.where((half - 1) - pos == o, pltpu.roll(cum, c - o, 0), ref_k)
        e_q = jnp.where(upper, cum - ref_q, 0.0)
        f_k = jnp.where(upper, 0.0, ref_k - cum)
        qm = jnp.where(upper, qh * jnp.exp(e_q), 0.0).astype(BF16)
        km = jnp.where(upper, 0.0, kh * jnp.exp(f_k)).astype(BF16)
        am = lax.dot_general(qm, km, NT_DIMS, preferred_element_type=F32)
        if m < c:
            am = jnp.where(same_block[m], am, 0.0)
        a = a + am
        m = half
    return a


def _gla_prompt_body(x_ref, gm_ref, wq_ref, wk_ref, wv_ref, wg_ref, wgl_ref, wgate_ref, bgate_ref,
                     gn_ref, wo_ref, o_ref, st_ref, s_scr):
    ci = pl.program_id(1)
    c = GLA_C

    @pl.when(ci == 0)
    def _():
        s_scr[...] = jnp.zeros_like(s_scr)

    x = x_ref[...]
    h = _rms(x, gm_ref[...]).astype(BF16)
    q = _dot(h, wq_ref[...]) * (GLA_DK ** -0.5)
    k = _dot(h, wk_ref[...])
    v = _dot(h, wv_ref[...])
    g = _dot(h, wg_ref[...])
    gl = _dot(h, wgl_ref[...])
    z = _dot(gl.astype(BF16), wgate_ref[...]) + bgate_ref[...]
    la = _log_sigmoid(z) * (1.0 / GLA_GATE_NORM)

    rowi = lax.broadcasted_iota(jnp.int32, (c, c), 0)
    coli = lax.broadcasted_iota(jnp.int32, (c, c), 1)
    tril = (coli <= rowi).astype(F32)
    cum = jnp.dot(tril, la, precision=lax.Precision.HIGHEST, preferred_element_type=F32)

    same_block = {}
    m = c // 2
    while m >= 2:
        sh = int(math.log2(m))
        same_block[m] = (rowi >> sh) == (coli >> sh)
        m //= 2

    outs = []
    for hd in range(GLA_HEADS):
        ks = slice(hd * GLA_DK, (hd + 1) * GLA_DK)
        vs = slice(hd * GLA_DV, (hd + 1) * GLA_DV)
        qh, kh, cumh, vh = q[:, ks], k[:, ks], cum[:, ks], v[:, vs]
        vhb = vh.astype(BF16)
        total = cumh[c - 1:c, :]
        st = s_scr[hd]
        o = lax.dot_general((qh * jnp.exp(cumh)).astype(BF16), st.astype(BF16), NT_DIMS,
                            preferred_element_type=F32)
        a = _gla_intra(qh, kh, cumh, same_block)
        o = o + _dot(a.astype(BF16), vhb)
        o = o + jnp.sum(qh * kh, axis=-1, keepdims=True) * vh
        kdec = (kh * jnp.exp(total - cumh)).astype(BF16)
        st_new = st * jnp.exp(total) + lax.dot_general(vhb, kdec, TN_DIMS,
                                                       preferred_element_type=F32)
        s_scr[hd] = st_new
        o = o * lax.rsqrt(jnp.mean(o * o, axis=-1, keepdims=True) + EPS) * gn_ref[...]
        outs.append(o)
    y = jnp.concatenate(outs, axis=1) * _silu(g)
    o_ref[...] = x + _dot(y.astype(BF16), wo_ref[...])

    @pl.when(ci == pl.num_programs(1) - 1)
    def _():
        for hd in range(GLA_HEADS):
            st_ref[0, hd] = s_scr[hd].T


def _gla_prompt(x, gm, wq, wk, wv, wg, wgl, wgate, bgate, gn, wo, n_seq, seq_len):
    n_chunk = seq_len // GLA_C
    weights = [gm, wq, wk, wv, wg, wgl, wgate, bgate, gn, wo]
    row_spec = pl.BlockSpec((GLA_C, D_MODEL), lambda b, c: (b * n_chunk + c, 0))
    return pl.pallas_call(
        _gla_prompt_body,
        grid=(n_seq, n_chunk),
        in_specs=[row_spec] + [_resident(w.shape) for w in weights],
        out_specs=[row_spec,
                   pl.BlockSpec((1, GLA_HEADS, GLA_DK, GLA_DV), lambda b, c: (b, 0, 0, 0))],
        out_shape=[jax.ShapeDtypeStruct(x.shape, F32),
                   jax.ShapeDtypeStruct((n_seq, GLA_HEADS, GLA_DK, GLA_DV), F32)],
        scratch_shapes=[pltpu.VMEM((GLA_HEADS, GLA_DV, GLA_DK), F32)],
        compiler_params=_cparams(("parallel", "arbitrary")),
        name="gla_prompt",
    )(x, *weights)


def _gla_sample_proj_body(x_ref, gm_ref, wq_ref, wk_ref, wv_ref, wg_ref, wgl_ref, wgate_ref,
                          bgate_ref, q_ref, k_ref, a_ref, v_ref, g_ref):
    h = _rms(x_ref[...], gm_ref[...]).astype(BF16)
    q_ref[...] = _dot(h, wq_ref[...]) * (GLA_DK ** -0.5)
    k_ref[...] = _dot(h, wk_ref[...])
    v_ref[...] = _dot(h, wv_ref[...])
    g_ref[...] = _dot(h, wg_ref[...])
    gl = _dot(h, wgl_ref[...])
    z = _dot(gl.astype(BF16), wgate_ref[...]) + bgate_ref[...]
    a_ref[...] = jnp.exp(_log_sigmoid(z) * (1.0 / GLA_GATE_NORM))


GLA_SEQ_PER_STEP = SUBLANES


def _gla_sample_rec_body(s_ref, qc_ref, kc_ref, ac_ref, v_ref, so_ref, o_ref):
    for j in range(GLA_SEQ_PER_STEP):
        for hd in range(GLA_HEADS):
            ks = slice(hd * GLA_DK, (hd + 1) * GLA_DK)
            vs = slice(hd * GLA_DV, (hd + 1) * GLA_DV)
            acol = ac_ref[0, ks, j:j + 1]
            kcol = kc_ref[0, ks, j:j + 1]
            qcol = qc_ref[0, ks, j:j + 1]
            vrow = v_ref[j:j + 1, vs]
            s_new = s_ref[j, hd] * acol + kcol * vrow
            so_ref[j, hd] = s_new
            o_ref[j:j + 1, vs] = jnp.sum(s_new * qcol, axis=0, keepdims=True)


def _gla_sample_out_body(x_ref, o_ref, g_ref, gn_ref, wo_ref, y_ref):
    outs = []
    for hd in range(GLA_HEADS):
        o = o_ref[:, hd * GLA_DV:(hd + 1) * GLA_DV]
        outs.append(o * lax.rsqrt(jnp.mean(o * o, axis=-1, keepdims=True) + EPS) * gn_ref[...])
    y = jnp.concatenate(outs, axis=1) * _silu(g_ref[...])
    y_ref[...] = x_ref[...] + _dot(y.astype(BF16), wo_ref[...])


def _gla_sample(x, state, gm, wq, wk, wv, wg, wgl, wgate, bgate, gn, wo):
    n = x.shape[0]
    dq = GLA_HEADS * GLA_DK
    weights = [gm, wq, wk, wv, wg, wgl, wgate, bgate]
    q, k, a, v, g = pl.pallas_call(
        _gla_sample_proj_body,
        out_shape=[jax.ShapeDtypeStruct((n, dq), F32)] * 3
        + [jax.ShapeDtypeStruct((n, D_MODEL), F32)] * 2,
        compiler_params=pltpu.CompilerParams(vmem_limit_bytes=VMEM_LIMIT),
        name="gla_sample_proj",
    )(x, *weights)

    nstep = n // GLA_SEQ_PER_STEP

    def cols(t):
        return t.reshape(nstep, GLA_SEQ_PER_STEP, dq).transpose(0, 2, 1)

    col_spec = pl.BlockSpec((1, dq, GLA_SEQ_PER_STEP), lambda i: (i, 0, 0))
    st_spec = pl.BlockSpec((GLA_SEQ_PER_STEP, GLA_HEADS, GLA_DK, GLA_DV), lambda i: (i, 0, 0, 0))
    row_spec = pl.BlockSpec((GLA_SEQ_PER_STEP, D_MODEL), lambda i: (i, 0))
    s_new, o = pl.pallas_call(
        _gla_sample_rec_body,
        grid=(nstep,),
        in_specs=[st_spec, col_spec, col_spec, col_spec, row_spec],
        out_specs=[st_spec, row_spec],
        out_shape=[jax.ShapeDtypeStruct(state.shape, F32), jax.ShapeDtypeStruct((n, D_MODEL), F32)],
        compiler_params=_cparams(("parallel",)),
        name="gla_sample_rec",
    )(state, cols(q), cols(k), cols(a), v)

    y = pl.pallas_call(
        _gla_sample_out_body,
        out_shape=jax.ShapeDtypeStruct((n, D_MODEL), F32),
        compiler_params=pltpu.CompilerParams(vmem_limit_bytes=VMEM_LIMIT),
        name="gla_sample_out",
    )(x, o, g, gn, wo)
    return y, s_new


def _t5_bucket_np(dist):
    max_exact = REL_BUCKETS // 2
    n = np.maximum(dist, 0)
    nf = np.maximum(n, max_exact).astype(np.float32)
    large = max_exact + (np.log(nf / np.float32(max_exact)) / np.float32(math.log(REL_MAX_DIST / max_exact))
                         * np.float32(REL_BUCKETS - max_exact)).astype(np.int32)
    return np.where(n < max_exact, n, np.minimum(large, REL_BUCKETS - 1)).astype(np.int32)


def _bias_tables_body(rb_ref, bd_ref, b1_ref, bdec_ref, od_ref, o1_ref, odec_ref):
    h = pl.program_id(0)
    td, t1, tdec = bd_ref[...], b1_ref[...], bdec_ref[...]
    ad = jnp.zeros(td.shape, F32)
    a1 = jnp.zeros(t1.shape, F32)
    adec = jnp.zeros(tdec.shape, F32)
    for b in range(REL_BUCKETS):
        val = rb_ref[b, h]
        ad = jnp.where(td == b, val, ad)
        a1 = jnp.where(t1 == b, val, a1)
        adec = jnp.where(tdec == b, val, adec)
    od_ref[0] = ad
    o1_ref[0] = a1
    odec_ref[0] = adec


def _bias_tables(rel_bias, past_len):
    kc = np.arange(MOBA_BLOCK)[:, None]
    qr = np.arange(MOBA_BLOCK)[None, :]
    bd = _t5_bucket_np(qr - kc)
    b1 = _t5_bucket_np(MOBA_BLOCK + qr - kc)
    nblk = past_len // MOBA_BLOCK
    kpos = np.arange(nblk)[:, None] * MOBA_BLOCK + np.arange(MOBA_BLOCK)[None, :]
    bdec = _t5_bucket_np(past_len - kpos)
    full = lambda a: pl.BlockSpec(a.shape, lambda h: (0,) * a.ndim)
    return pl.pallas_call(
        _bias_tables_body,
        grid=(MOBA_HEADS,),
        in_specs=[pl.BlockSpec(memory_space=pltpu.SMEM), full(bd), full(b1), full(bdec)],
        out_specs=[pl.BlockSpec((1,) + bd.shape, lambda h: (h, 0, 0)),
                   pl.BlockSpec((1,) + b1.shape, lambda h: (h, 0, 0)),
                   pl.BlockSpec((1,) + bdec.shape, lambda h: (h, 0, 0))],
        out_shape=[jax.ShapeDtypeStruct((MOBA_HEADS,) + bd.shape, F32),
                   jax.ShapeDtypeStruct((MOBA_HEADS,) + b1.shape, F32),
                   jax.ShapeDtypeStruct((MOBA_HEADS,) + bdec.shape, F32)],
        compiler_params=_cparams(("parallel",)),
        name="moba_bias_tables",
    )(rel_bias, jnp.asarray(bd), jnp.asarray(b1), jnp.asarray(bdec))


def _moba_qkv_body(x_ref, g_ref, w_ref, q_ref, k_ref, v_ref, *t_refs):
    h = _rms(x_ref[...], g_ref[...]).astype(BF16)
    qkv = _dot(h, w_ref[...])
    q_ref[...] = qkv[:, :D_MODEL] * (MOBA_DH ** -0.5)
    k = qkv[:, D_MODEL:2 * D_MODEL]
    v = qkv[:, 2 * D_MODEL:]
    k_ref[...] = k
    v_ref[...] = v
    if t_refs:
        t_refs[0][0] = k.T
        t_refs[1][0] = v.T


def _moba_qkv(x, g, w, tm, seq_len=None):
    rows = x.shape[0]
    row_spec = pl.BlockSpec((tm, D_MODEL), lambda i: (i, 0))
    out_specs = [row_spec] * 3
    out_shape = [jax.ShapeDtypeStruct((rows, D_MODEL), F32)] * 3
    if seq_len is not None:
        per_seq = seq_len // tm
        t_spec = pl.BlockSpec((1, D_MODEL, tm), lambda i: (i // per_seq, 0, i % per_seq))
        out_specs += [t_spec] * 2
        out_shape += [jax.ShapeDtypeStruct((rows // seq_len, D_MODEL, seq_len), F32)] * 2
    return pl.pallas_call(
        _moba_qkv_body,
        grid=(rows // tm,),
        in_specs=[row_spec, _resident(g.shape), _resident(w.shape)],
        out_specs=out_specs,
        out_shape=out_shape,
        compiler_params=_cparams(("parallel",)),
        name="moba_qkv",
    )(x, g, w)


def _topk_select(gate, valid, axis):
    n = gate.shape[axis]
    idx = lax.broadcasted_iota(jnp.int32, gate.shape, axis)
    cnt = jnp.zeros(gate.shape, jnp.int32)
    for m in range(n):
        gm = lax.slice_in_dim(gate, m, m + 1, axis=axis)
        beats = (gm > gate) | ((gm == gate) & (m < idx))
        cnt = cnt + jnp.where(beats, valid(m), 0)
    return jnp.where(cnt < MOBA_TOPK, 1.0, 0.0)


def _moba_prompt_body(rb_ref, q_ref, k_ref, v_ref, bd_ref, b1_ref, o_ref,
                      kb_scr, vt_scr, km_scr, neg_scr, m_scr, l_scr, acc_scr):
    hg = pl.program_id(1)
    qi = pl.program_id(2)
    nblk = kb_scr.shape[0]
    blk = MOBA_BLOCK
    dh = MOBA_DH

    @pl.when(qi == 0)
    def _():
        for n in range(nblk):
            kt = k_ref[n * blk:(n + 1) * blk, :]
            kb_scr[n] = kt.astype(BF16)
            km_scr[n:n + 1, :] = jnp.mean(kt, axis=0, keepdims=True)
            vt_scr[n] = v_ref[n * blk:(n + 1) * blk, :].T.astype(BF16)

    q2 = q_ref[...]
    lane = lax.broadcasted_iota(jnp.int32, (1, MOBA_HG * dh), 1)
    ki_idx = lax.broadcasted_iota(jnp.int32, (blk, blk), 0)
    qi_idx = lax.broadcasted_iota(jnp.int32, (blk, blk), 1)
    blk_idx = lax.broadcasted_iota(jnp.int32, (nblk, blk), 0)

    hs = range(MOBA_HG)
    qhb, far_bias = [], []
    for hh in hs:
        in_head = (lane >= hh * dh) & (lane < (hh + 1) * dh)
        qh = jnp.where(in_head, q2, 0.0)
        qhb.append(qh.astype(BF16))
        far_bias.append(rb_ref[REL_BUCKETS - 1, MOBA_HG * hg + hh])
        kmh = jnp.where(in_head, km_scr[...], 0.0)
        gate = lax.dot_general(kmh, qh, NT_DIMS, precision=lax.Precision.HIGHEST,
                               preferred_element_type=F32)
        keep = _topk_select(gate, lambda m: (m < qi).astype(jnp.int32), 0)
        neg_scr[hh] = jnp.where((keep > 0.0) & (blk_idx < qi), 0.0, NEG)

    def scores(ki):
        kt = kb_scr[ki]
        return [lax.dot_general(kt, qhb[hh], NT_DIMS, preferred_element_type=F32) for hh in hs]

    def values(ki, p):
        return [_dot(vt_scr[ki, hh * dh:(hh + 1) * dh, :], p[hh]) for hh in hs]

    s = scores(qi)
    p = []
    for hh in hs:
        sh = jnp.where(ki_idx <= qi_idx, s[hh] + bd_ref[hh], NEG)
        m0 = jnp.max(sh, axis=0, keepdims=True)
        ph = jnp.exp(sh - m0)
        m_scr[hh] = m0
        l_scr[hh] = jnp.sum(ph, axis=0, keepdims=True)
        p.append(ph.astype(BF16))
    pv = values(qi, p)
    for hh in hs:
        acc_scr[hh] = pv[hh]

    def past_tile(ki, bias_of):
        s = scores(ki)
        p, alpha = [], []
        for hh in hs:
            sh = s[hh] + bias_of(hh, neg_scr[hh, pl.ds(ki, 1), :])
            m_old = m_scr[hh]
            m_new = jnp.maximum(m_old, jnp.max(sh, axis=0, keepdims=True))
            a = jnp.exp(m_old - m_new)
            ph = jnp.exp(sh - m_new)
            m_scr[hh] = m_new
            l_scr[hh] = a * l_scr[hh] + jnp.sum(ph, axis=0, keepdims=True)
            p.append(ph.astype(BF16))
            alpha.append(a)
        pv = values(ki, p)
        for hh in hs:
            acc_scr[hh] = alpha[hh] * acc_scr[hh] + pv[hh]

    @pl.when(qi >= 1)
    def _():
        past_tile(qi - 1, lambda hh, neg_row: b1_ref[hh] + neg_row)

    def far_step(ki, carry):
        past_tile(ki, lambda hh, neg_row: neg_row + far_bias[hh])
        return carry

    lax.fori_loop(0, jnp.maximum(qi - 1, 0), far_step, 0)

    ot = jnp.concatenate([acc_scr[hh] / l_scr[hh] for hh in hs], axis=0)
    o_ref[...] = ot.T


def _moba_prompt(rel_bias, q, k, v, bias_d, bias_1, n_seq, seq_len):
    nblk = seq_len // MOBA_BLOCK
    hgrp = MOBA_HG
    width = hgrp * MOBA_DH
    q_spec = pl.BlockSpec((MOBA_BLOCK, width), lambda b, hg, qi: (b * nblk + qi, hg))
    kv_spec = pl.BlockSpec((seq_len, width), lambda b, hg, qi: (b, hg))
    bias_spec = pl.BlockSpec((hgrp, MOBA_BLOCK, MOBA_BLOCK), lambda b, hg, qi: (hg, 0, 0))
    return pl.pallas_call(
        _moba_prompt_body,
        grid=(n_seq, MOBA_HEADS // hgrp, nblk),
        in_specs=[pl.BlockSpec(memory_space=pltpu.SMEM), q_spec, kv_spec, kv_spec, bias_spec, bias_spec],
        out_specs=q_spec,
        out_shape=jax.ShapeDtypeStruct(q.shape, F32),
        scratch_shapes=[
            pltpu.VMEM((nblk, MOBA_BLOCK, width), BF16),
            pltpu.VMEM((nblk, width, MOBA_BLOCK), BF16),
            pltpu.VMEM((nblk, width), F32),
            pltpu.VMEM((hgrp, nblk, MOBA_BLOCK), F32),
            pltpu.VMEM((hgrp, 1, MOBA_BLOCK), F32),
            pltpu.VMEM((hgrp, 1, MOBA_BLOCK), F32),
            pltpu.VMEM((hgrp, MOBA_DH, MOBA_BLOCK), F32),
        ],
        compiler_params=_cparams(("parallel", "parallel", "arbitrary")),
        name="moba_prompt",
    )(rel_bias, q, k, v, bias_d, bias_1)


def _moba_decode_body(pt_ref, q_ref, qt_ref, kn_ref, vnt_ref, rb0_ref, ck0_ref, ck1_ref, cv0_ref,
                      cv1_ref, bdec_ref, o_ref, s0_scr, s1_scr, gate_scr, m_scr, l_scr, part_scr):
    del pt_ref
    n = pl.program_id(1)
    nb = m_scr.shape[0]
    heads, dh = MOBA_HEADS, MOBA_DH

    qt = qt_ref[0]
    for h in range(heads):
        qc = qt[:, h:h + 1]
        s0_scr[h:h + 1, :] = jnp.sum(ck0_ref[0, h] * qc, axis=0, keepdims=True)
        s1_scr[h:h + 1, :] = jnp.sum(ck1_ref[0, h] * qc, axis=0, keepdims=True)
    s0, s1 = s0_scr[...], s1_scr[...]
    gate_scr[n] = jnp.sum(s0, axis=-1, keepdims=True) + jnp.sum(s1, axis=-1, keepdims=True)
    z0 = s0 + bdec_ref[0, 0]
    z1 = s1 + bdec_ref[0, 1]
    m = jnp.maximum(jnp.max(z0, axis=-1, keepdims=True), jnp.max(z1, axis=-1, keepdims=True))
    p0 = jnp.exp(z0 - m)
    p1 = jnp.exp(z1 - m)
    m_scr[n] = m
    l_scr[n] = jnp.sum(p0, axis=-1, keepdims=True) + jnp.sum(p1, axis=-1, keepdims=True)
    for h in range(heads):
        part_scr[n, h] = cv0_ref[0, h] * p0[h:h + 1, :] + cv1_ref[0, h] * p1[h:h + 1, :]

    @pl.when(n == nb - 1)
    def _():
        blk_lane = lax.broadcasted_iota(jnp.int32, (heads, nb), 1)
        gate = jnp.zeros((heads, nb), F32)
        for b in range(nb):
            gate = jnp.where(blk_lane == b, gate_scr[b], gate)
        keep = _topk_select(gate, lambda m_: 1, 1)
        l_new = jnp.sum(q_ref[0] * kn_ref[0], axis=-1, keepdims=True) + rb0_ref[...]
        m_all = l_new
        for b in range(nb):
            m_all = jnp.maximum(m_all, jnp.where(keep[:, b:b + 1] > 0.0, m_scr[b], NEG))
        w_new = jnp.exp(l_new - m_all)
        den = w_new
        w_blk = []
        for b in range(nb):
            w = jnp.where(keep[:, b:b + 1] > 0.0, jnp.exp(m_scr[b] - m_all), 0.0)
            den = den + w * l_scr[b]
            w_blk.append(jnp.broadcast_to(w, (heads, PAGE_SIZE)))
        w_new = jnp.broadcast_to(w_new, (heads, PAGE_SIZE))
        den = jnp.broadcast_to(den, (heads, PAGE_SIZE))
        vnt = vnt_ref[0]
        head_lane = lax.broadcasted_iota(jnp.int32, (dh, heads), 1)
        out = jnp.zeros((dh, heads), F32)
        for h in range(heads):
            acc = part_scr[0, h] * w_blk[0][h:h + 1, :]
            for b in range(1, nb):
                acc = acc + part_scr[b, h] * w_blk[b][h:h + 1, :]
            tot = jnp.sum(acc, axis=-1, keepdims=True) + vnt[:, h:h + 1] * w_new[h:h + 1, :]
            res = tot / den[h:h + 1, :]
            out = jnp.where(head_lane == h, res[:, :heads], out)
        o_ref[0] = out


def _moba_decode(page_table, q, k_new, v_new, rb0, cache_k, cache_v, bias_dec):
    n_dec, n_pages = page_table.shape
    heads, dh = MOBA_HEADS, MOBA_DH
    ppb = MOBA_BLOCK // PAGE_SIZE
    assert ppb == 2
    nblk = n_pages // ppb
    rows = lambda t: t.reshape(n_dec, heads, dh)
    cols = lambda t: t.reshape(n_dec, heads, dh).transpose(0, 2, 1)
    row_spec = pl.BlockSpec((1, heads, dh), lambda b, n, pt: (b, 0, 0))
    col_spec = pl.BlockSpec((1, dh, heads), lambda b, n, pt: (b, 0, 0))
    page_spec = lambda j: pl.BlockSpec((1, heads, dh, PAGE_SIZE),
                                       lambda b, n, pt: (pt[b, ppb * n + j], 0, 0, 0))
    grid_spec = pltpu.PrefetchScalarGridSpec(
        num_scalar_prefetch=1,
        grid=(n_dec, nblk),
        in_specs=[row_spec, col_spec, row_spec, col_spec,
                  pl.BlockSpec((heads, 1), lambda b, n, pt: (0, 0)),
                  page_spec(0), page_spec(1), page_spec(0), page_spec(1),
                  pl.BlockSpec((1, ppb, heads, PAGE_SIZE), lambda b, n, pt: (n, 0, 0, 0))],
        out_specs=col_spec,
        scratch_shapes=[pltpu.VMEM((heads, PAGE_SIZE), F32),
                        pltpu.VMEM((heads, PAGE_SIZE), F32),
                        pltpu.VMEM((nblk, heads, 1), F32),
                        pltpu.VMEM((nblk, heads, 1), F32),
                        pltpu.VMEM((nblk, heads, 1), F32),
                        pltpu.VMEM((nblk, heads, dh, PAGE_SIZE), F32)],
    )
    out = pl.pallas_call(
        _moba_decode_body,
        grid_spec=grid_spec,
        out_shape=jax.ShapeDtypeStruct((n_dec, dh, heads), F32),
        compiler_params=_cparams(("parallel", "arbitrary")),
        name="moba_decode",
    )(page_table, rows(q), cols(q), rows(k_new), cols(v_new), rb0,
      cache_k, cache_k, cache_v, cache_v, bias_dec)
    return out.transpose(0, 2, 1).reshape(n_dec, heads * dh)


def _prep_ffn(w_up, w_down):
    wg = w_up[:, :D_FF].reshape(D_MODEL, FF_NCHUNK, FF_CHUNK)
    wu = w_up[:, D_FF:].reshape(D_MODEL, FF_NCHUNK, FF_CHUNK)
    wup = jnp.concatenate([wg, wu], axis=-1).transpose(1, 0, 2).astype(BF16)
    wdn = w_down.reshape(FF_NCHUNK, FF_CHUNK, D_MODEL).astype(BF16)
    return wup, wdn


def _prep_sgu(w_in, ln_g, ln_b, w_s, b_s, w_out):
    panels = lambda w: w.reshape(D_MODEL, A_NPANEL, A_PANEL).transpose(1, 0, 2).astype(BF16)
    wu = panels(w_in[:, :A_HALF])
    wv = panels(w_in[:, A_HALF:])
    lng = ln_g.reshape(A_NPANEL, 1, A_PANEL)
    lnb = ln_b.reshape(A_NPANEL, 1, A_PANEL)
    wout = w_out.reshape(A_NPANEL, A_PANEL, D_MODEL).astype(BF16)
    causal = jnp.tril(jnp.ones((A_CHUNK, A_CHUNK), dtype=bool))
    wm_p = jnp.where(causal[None], w_s, 0.0).astype(BF16)
    sb_p = jnp.broadcast_to(b_s[:, :, None], (A_GROUPS, A_CHUNK, A_GROUP_DIM))
    eye = jnp.eye(A_CHUNK, dtype=F32)
    wm_s = (w_s[:, 0, 0][:, None, None] * eye[None]).astype(BF16)
    sb_s = jnp.broadcast_to(b_s[:, 0][:, None, None], (A_GROUPS, A_CHUNK, A_GROUP_DIM))
    return (wu, wv, lng, lnb, wout), (wm_p, sb_p), (wm_s, sb_s)


def _prep_gla(w_in, w_gate, b_gate, gn, w_out):
    dq, dv = GLA_HEADS * GLA_DK, GLA_HEADS * GLA_DV
    wq = w_in[:, :dq].astype(BF16)
    wk = w_in[:, dq:2 * dq].astype(BF16)
    wv = w_in[:, 2 * dq:2 * dq + dv].astype(BF16)
    wg = w_in[:, 2 * dq + dv:2 * dq + 2 * dv].astype(BF16)
    pad = LANES - GLA_GATE_RANK
    wgl = jnp.pad(w_in[:, 2 * dq + 2 * dv:], ((0, 0), (0, pad))).astype(BF16)
    wgate = jnp.pad(w_gate, ((0, pad), (0, 0))).astype(BF16)
    return wq, wk, wv, wg, wgl, wgate, b_gate.reshape(1, dq), gn.reshape(1, GLA_DV), w_out.astype(BF16)


def kernel(x_prompt, x_sample, cache_k, cache_v, page_table, state_gla, norm_mix, norm_ffn, norm_final,
           w_in_a, ln_a_g, ln_a_b, w_s_a, b_s_a, w_out_a, w_in_b, w_gate_b, b_gate_b, gn_b, w_out_b,
           w_in_c, w_out_c, rel_bias, w_up, w_down):
    n_pr, l_pr, _ = x_prompt.shape
    n_dec, l_dec, _ = x_sample.shape
    assert l_dec == 1 and l_pr % MOBA_BLOCK == 0 and l_pr % GLA_C == 0
    depth = norm_mix.shape[0]
    past_len = page_table.shape[1] * PAGE_SIZE
    assert past_len % MOBA_BLOCK == 0
    tm_p, tm_s = 512, n_dec

    xp = x_prompt.reshape(n_pr * l_pr, D_MODEL)
    xs = x_sample.reshape(n_dec, D_MODEL)
    gfin = norm_final.reshape(1, D_MODEL)
    k_p, v_p, k_s, v_s, gla_p, gla_s, sgu_s = [], [], [], [], [], [], []

    for i in range(depth):
        kind, j = i % 3, i // 3
        gm = norm_mix[i].reshape(1, D_MODEL)
        gf = norm_ffn[i].reshape(1, D_MODEL)
        wup, wdn = _prep_ffn(w_up[i], w_down[i])
        last = gfin if i == depth - 1 else None
        proj_p = proj_s = None
        if kind == 0:
            shared, mode_p, mode_s = _prep_sgu(w_in_a[j], ln_a_g[j], ln_a_b[j], w_s_a[j], b_s_a[j], w_out_a[j])
            wu, wv, lng, lnb, wout = shared
            (xp,) = _sgu(xp, gm, wu, wv, lng, lnb, *mode_p, wout, tm_p, False)
            xs, v_rows = _sgu(xs, gm, wu, wv, lng, lnb, *mode_s, wout, tm_s, True)
            sgu_s.append(v_rows.reshape(n_dec, l_dec, A_HALF))
        elif kind == 1:
            wts = _prep_gla(w_in_b[j], w_gate_b[j], b_gate_b[j], gn_b[j], w_out_b[j])
            xp, sp = _gla_prompt(xp, gm, *wts, n_pr, l_pr)
            xs, ss = _gla_sample(xs, state_gla[j], gm, *wts)
            gla_p.append(sp)
            gla_s.append(ss)
        else:
            w_in = w_in_c[j].astype(BF16)
            bias_d, bias_1, bias_dec = _bias_tables(rel_bias, past_len)
            qp, kp, vp, kpt, vpt = _moba_qkv(xp, gm, w_in, tm_p, seq_len=l_pr)
            op = _moba_prompt(rel_bias, qp, kp, vp, bias_d, bias_1, n_pr, l_pr)
            qs, kn, vn = _moba_qkv(xs, gm, w_in, tm_s)
            os_ = _moba_decode(page_table, qs, kn, vn, rel_bias[0].reshape(MOBA_HEADS, 1),
                               cache_k[j].transpose(0, 2, 3, 1), cache_v[j].transpose(0, 2, 3, 1),
                               bias_dec.reshape(MOBA_HEADS, -1, 2, PAGE_SIZE).transpose(1, 2, 0, 3))
            wo_c = w_out_c[j].astype(BF16)
            proj_p, proj_s = (op, wo_c), (os_, wo_c)
            heads_last = lambda t: t.reshape(n_pr, MOBA_HEADS, MOBA_DH, l_pr).transpose(0, 3, 1, 2)
            k_p.append(heads_last(kpt))
            v_p.append(heads_last(vpt))
            k_s.append(kn.reshape(n_dec, l_dec, MOBA_HEADS, MOBA_DH))
            v_s.append(vn.reshape(n_dec, l_dec, MOBA_HEADS, MOBA_DH))
        if proj_p is None:
            xp = _ffn(xp, gf, wup, wdn, tm_p, gfinal=last)
            xs = _ffn(xs, gf, wup, wdn, tm_s, gfinal=last)
        else:
            xp = _ffn(xp, gf, wup, wdn, tm_p, a=proj_p[0], wo=proj_p[1], gfinal=last)
            xs = _ffn(xs, gf, wup, wdn, tm_s, a=proj_s[0], wo=proj_s[1], gfinal=last)

    return (xp.reshape(n_pr, l_pr, D_MODEL), xs.reshape(n_dec, l_dec, D_MODEL),
            jnp.stack(k_p), jnp.stack(v_p), jnp.stack(k_s), jnp.stack(v_s),
            jnp.stack(gla_p), jnp.stack(gla_s), jnp.stack(sgu_s))
```

```python
import functools
import math

import numpy as np
import jax
import jax.numpy as jnp
from jax import lax
from jax.experimental import pallas as pl
from jax.experimental.pallas import tpu as pltpu

F32 = jnp.float32
BF16 = jnp.bfloat16

D_MODEL = 1024
EPS = 1e-6
NEG = -1e30

LANES = 128
SUBLANES = 8
MXU_DIM = 256
VMEM_LIMIT = 56 * 1024 * 1024

A_CHUNK = 128
A_HALF = 3 * D_MODEL
A_GROUPS = 8
A_GROUP_DIM = A_HALF // A_GROUPS
A_PANEL = 2 * A_GROUP_DIM
A_NPANEL = A_HALF // A_PANEL

GLA_HEADS = 4
GLA_DK = 128
GLA_DV = 256
GLA_GATE_RANK = 16
GLA_GATE_NORM = 16.0
GLA_C = 256

MOBA_HEADS = 16
MOBA_DH = 64
MOBA_BLOCK = 256
MOBA_TOPK = 3
MOBA_HG = MXU_DIM // MOBA_DH
PAGE_SIZE = 128
REL_BUCKETS = 32
REL_MAX_DIST = 128

D_FF = 2816
FF_CHUNK = MXU_DIM
FF_NCHUNK = D_FF // FF_CHUNK

NT_DIMS = (((1,), (1,)), ((), ()))
TN_DIMS = (((0,), (0,)), ((), ()))


def _cparams(sem):
    return pltpu.CompilerParams(dimension_semantics=sem, vmem_limit_bytes=VMEM_LIMIT)


def _resident(shape):
    nd = len(shape)
    return pl.BlockSpec(shape, lambda *_: (0,) * nd, pipeline_mode=pl.Buffered(1))


def _rms(x, g):
    return x * lax.rsqrt(jnp.mean(x * x, axis=-1, keepdims=True) + EPS) * g


def _gelu(x):
    return 0.5 * x * (1.0 + lax.erf(x * (1.0 / math.sqrt(2.0))))


def _silu(x):
    return x * jax.nn.sigmoid(x)


def _dot(a, b):
    return jnp.dot(a, b, preferred_element_type=F32)


def _ffn_body(*refs, has_proj, final):
    refs = list(refs)
    x_ref = refs.pop(0)
    a_ref = wo_ref = gf_ref = None
    if has_proj:
        a_ref = refs.pop(0)
        wo_ref = refs.pop(0)
    g_ref = refs.pop(0)
    wup_ref = refs.pop(0)
    wdn_ref = refs.pop(0)
    if final:
        gf_ref = refs.pop(0)
    o_ref = refs.pop(0)

    x = x_ref[...]
    if has_proj:
        x = x + _dot(a_ref[...].astype(BF16), wo_ref[...])
    h = _rms(x, g_ref[...]).astype(BF16)
    o_ref[...] = x

    for c in range(FF_NCHUNK):
        cols = slice(c * FF_CHUNK, (c + 1) * FF_CHUNK)
        gate = _dot(h, wup_ref[:, cols])
        up = _dot(h, wup_ref[:, D_FF + c * FF_CHUNK:D_FF + (c + 1) * FF_CHUNK])
        act = (_silu(gate) * up).astype(BF16)
        o_ref[...] += _dot(act, wdn_ref[cols, :])
    if final:
        o_ref[...] = _rms(o_ref[...], gf_ref[...])


def _ffn(x, g, wup, wdn, tm, a=None, wo=None, gfinal=None):
    rows = x.shape[0]
    row_spec = pl.BlockSpec((tm, D_MODEL), lambda i: (i, 0))
    args = [x]
    specs = [row_spec]
    if a is not None:
        args += [a, wo]
        specs += [row_spec, _resident(wo.shape)]
    args += [g, wup, wdn]
    specs += [_resident(g.shape), _resident(wup.shape), _resident(wdn.shape)]
    if gfinal is not None:
        args.append(gfinal)
        specs.append(_resident(gfinal.shape))
    return pl.pallas_call(
        functools.partial(_ffn_body, has_proj=a is not None, final=gfinal is not None),
        grid=(rows // tm,),
        in_specs=specs,
        out_specs=row_spec,
        out_shape=jax.ShapeDtypeStruct((rows, D_MODEL), F32),
        compiler_params=_cparams(("parallel",)),
        name="ffn",
    )(*args)


def _sgu_body(*refs, tm, emit_v):
    (x_ref, g_ref, win_ref, lng_ref, lnb_ref, wm_ref, sb_ref, wout_ref) = refs[:8]
    if emit_v:
        o_ref, vout_ref, vbuf, sbuf = refs[8:]
    else:
        o_ref, vbuf, sbuf = refs[8:]
        vout_ref = None
    panel = lambda p, base=0: slice(base + p * A_PANEL, base + (p + 1) * A_PANEL)

    x = x_ref[...]
    h = _rms(x, g_ref[...]).astype(BF16)

    rsum = jnp.zeros((tm, 1), F32)
    for p in range(A_NPANEL):
        v = _gelu(_dot(h, win_ref[:, panel(p, A_HALF)]))
        vbuf[p] = v
        rsum = rsum + jnp.sum(v, axis=-1, keepdims=True)
    mean = rsum * (1.0 / A_HALF)
    ssq = jnp.zeros((tm, 1), F32)
    for p in range(A_NPANEL):
        d = vbuf[p] - mean
        ssq = ssq + jnp.sum(d * d, axis=-1, keepdims=True)
    rstd = lax.rsqrt(ssq * (1.0 / A_HALF) + EPS)

    acc = x
    for p in range(A_NPANEL):
        vn = (vbuf[p] - mean) * rstd * lng_ref[:, panel(p)] + lnb_ref[:, panel(p)]
        if emit_v:
            vout_ref[:, p * A_PANEL:(p + 1) * A_PANEL] = vn
        vnb = vn.astype(BF16)
        for c in range(tm // A_CHUNK):
            for gg in range(2):
                grp = 2 * p + gg
                rs = slice(c * A_CHUNK, (c + 1) * A_CHUNK)
                cs = slice(gg * A_GROUP_DIM, (gg + 1) * A_GROUP_DIM)
                sbuf[rs, cs] = _dot(wm_ref[grp], vnb[rs, cs]) + sb_ref[grp]
        u = _gelu(_dot(h, win_ref[:, panel(p)]))
        acc = acc + _dot((u * sbuf[...]).astype(BF16), wout_ref[panel(p), :])
    o_ref[...] = acc


def _sgu(x, g, win, lng, lnb, wm, sb, wout, tm, emit_v):
    rows = x.shape[0]
    row_spec = pl.BlockSpec((tm, D_MODEL), lambda i: (i, 0))
    weights = [g, win, lng, lnb, wm, sb, wout]
    out_shape = [jax.ShapeDtypeStruct((rows, D_MODEL), F32)]
    out_specs = [row_spec]
    if emit_v:
        out_shape.append(jax.ShapeDtypeStruct((rows, A_HALF), F32))
        out_specs.append(pl.BlockSpec((tm, A_HALF), lambda i: (i, 0)))
    return pl.pallas_call(
        functools.partial(_sgu_body, tm=tm, emit_v=emit_v),
        grid=(rows // tm,),
        in_specs=[row_spec] + [_resident(w.shape) for w in weights],
        out_specs=out_specs,
        out_shape=out_shape,
        scratch_shapes=[pltpu.VMEM((A_NPANEL, tm, A_PANEL), F32), pltpu.VMEM((tm, A_PANEL), F32)],
        compiler_params=_cparams(("parallel",)),
        name="sgu",
    )(x, *weights)


def _log_sigmoid(z):
    return jnp.minimum(z, 0.0) - jnp.log1p(jnp.exp(-jnp.abs(z)))


def _gla_intra(qh, kh, cum, same_block):
    c = qh.shape[0]
    r = lax.broadcasted_iota(jnp.int32, (c, 1), 0)
    a = jnp.zeros((c, c), F32)
    m = c
    while m >= 2:
        half = m // 2
        pos = r & (m - 1)
        upper = pos >= half
        if m >= 2 * SUBLANES:
            pieces = []
            for b in range(c // m):
                row = b * m + half - 1
                pieces.append(jnp.broadcast_to(cum[row:row + 1, :], (m, cum.shape[1])))
            ref_q = pieces[0] if len(pieces) == 1 else jnp.concatenate(pieces, axis=0)
            ref_k = ref_q
        else:
            ref_q = cum
            for d in range(1, half + 1):
                ref_q = jnp.where(pos - (half - 1) == d, pltpu.roll(cum, d, 0), ref_q)
            ref_k = cum
            for o in range(1, half):
                ref_k = jnp.where((half - 1) - pos == o, pltpu.roll(cum, c - o, 0), ref_k)
        decay = jnp.exp(jnp.where(upper, cum - ref_q, ref_k - cum))
        x = jnp.where(upper, qh, kh) * decay
        qm = jnp.where(upper, x, 0.0).astype(BF16)
        km = jnp.where(upper, 0.0, x).astype(BF16)
        am = lax.dot_general(qm, km, NT_DIMS, preferred_element_type=F32)
        if m < c:
            am = jnp.where(same_block[m], am, 0.0)
        a = a + am
        m = half
    return a


def _gla_project(h, win_ref, wgl_ref, wgate_ref, bgate_ref):
    dq, dv = GLA_HEADS * GLA_DK, GLA_HEADS * GLA_DV
    q = _dot(h, win_ref[:, :dq]) * (GLA_DK ** -0.5)
    k = _dot(h, win_ref[:, dq:2 * dq])
    v = _dot(h, win_ref[:, 2 * dq:2 * dq + dv])
    g = _dot(h, win_ref[:, 2 * dq + dv:2 * dq + 2 * dv])
    gl = _dot(h, wgl_ref[...])
    z = _dot(gl.astype(BF16), wgate_ref[...]) + bgate_ref[...]
    return q, k, v, g, z


def _gla_prompt_body(x_ref, gm_ref, win_ref, wgl_ref, wgate_ref, bgate_ref,
                     gn_ref, wo_ref, o_ref, st_ref, s_scr):
    ci = pl.program_id(1)
    c = GLA_C

    @pl.when(ci == 0)
    def _():
        s_scr[...] = jnp.zeros_like(s_scr)

    x = x_ref[...]
    h = _rms(x, gm_ref[...]).astype(BF16)
    q, k, v, g, z = _gla_project(h, win_ref, wgl_ref, wgate_ref, bgate_ref)
    la = _log_sigmoid(z) * (1.0 / GLA_GATE_NORM)

    rowi = lax.broadcasted_iota(jnp.int32, (c, c), 0)
    coli = lax.broadcasted_iota(jnp.int32, (c, c), 1)
    tril = (coli <= rowi).astype(F32)
    cum = jnp.dot(tril, la, precision=lax.Precision.HIGHEST, preferred_element_type=F32)

    same_block = {}
    m = c // 2
    while m >= 2:
        sh = int(math.log2(m))
        same_block[m] = (rowi >> sh) == (coli >> sh)
        m //= 2

    outs = []
    for hd in range(GLA_HEADS):
        ks = slice(hd * GLA_DK, (hd + 1) * GLA_DK)
        vs = slice(hd * GLA_DV, (hd + 1) * GLA_DV)
        qh, kh, cumh, vh = q[:, ks], k[:, ks], cum[:, ks], v[:, vs]
        vhb = vh.astype(BF16)
        total = cumh[c - 1:c, :]
        st = s_scr[hd]
        o = lax.dot_general((qh * jnp.exp(cumh)).astype(BF16), st.astype(BF16), NT_DIMS,
                            preferred_element_type=F32)
        a = _gla_intra(qh, kh, cumh, same_block)
        o = o + _dot(a.astype(BF16), vhb)
        o = o + jnp.sum(qh * kh, axis=-1, keepdims=True) * vh
        kdec = (kh * jnp.exp(total - cumh)).astype(BF16)
        st_new = st * jnp.exp(total) + lax.dot_general(vhb, kdec, TN_DIMS,
                                                       preferred_element_type=F32)
        s_scr[hd] = st_new
        o = o * lax.rsqrt(jnp.mean(o * o, axis=-1, keepdims=True) + EPS) * gn_ref[...]
        outs.append(o)
    y = jnp.concatenate(outs, axis=1) * _silu(g)
    o_ref[...] = x + _dot(y.astype(BF16), wo_ref[...])

    @pl.when(ci == pl.num_programs(1) - 1)
    def _():
        for hd in range(GLA_HEADS):
            st_ref[0, hd] = s_scr[hd].T


def _gla_prompt(x, gm, win, wgl, wgate, bgate, gn, wo, n_seq, seq_len):
    n_chunk = seq_len // GLA_C
    weights = [gm, win, wgl, wgate, bgate, gn, wo]
    row_spec = pl.BlockSpec((GLA_C, D_MODEL), lambda b, c: (b * n_chunk + c, 0))
    return pl.pallas_call(
        _gla_prompt_body,
        grid=(n_seq, n_chunk),
        in_specs=[row_spec] + [_resident(w.shape) for w in weights],
        out_specs=[row_spec,
                   pl.BlockSpec((1, GLA_HEADS, GLA_DK, GLA_DV), lambda b, c: (b, 0, 0, 0))],
        out_shape=[jax.ShapeDtypeStruct(x.shape, F32),
                   jax.ShapeDtypeStruct((n_seq, GLA_HEADS, GLA_DK, GLA_DV), F32)],
        scratch_shapes=[pltpu.VMEM((GLA_HEADS, GLA_DV, GLA_DK), F32)],
        compiler_params=_cparams(("parallel", "arbitrary")),
        name="gla_prompt",
    )(x, *weights)


def _gla_sample_proj_body(x_ref, gm_ref, win_ref, wgl_ref, wgate_ref, bgate_ref,
                          q_ref, k_ref, a_ref, v_ref, g_ref):
    h = _rms(x_ref[...], gm_ref[...]).astype(BF16)
    q_ref[...], k_ref[...], v_ref[...], g_ref[...], z = _gla_project(
        h, win_ref, wgl_ref, wgate_ref, bgate_ref)
    a_ref[...] = jnp.exp(_log_sigmoid(z) * (1.0 / GLA_GATE_NORM))


GLA_SEQ_PER_STEP = SUBLANES


def _gla_sample_rec_body(s_ref, qc_ref, kc_ref, ac_ref, v_ref, so_ref, o_ref):
    for j in range(GLA_SEQ_PER_STEP):
        for hd in range(GLA_HEADS):
            ks = slice(hd * GLA_DK, (hd + 1) * GLA_DK)
            vs = slice(hd * GLA_DV, (hd + 1) * GLA_DV)
            acol = ac_ref[0, ks, j:j + 1]
            kcol = kc_ref[0, ks, j:j + 1]
            qcol = qc_ref[0, ks, j:j + 1]
            vrow = v_ref[j:j + 1, vs]
            s_new = s_ref[j, hd] * acol + kcol * vrow
            so_ref[j, hd] = s_new
            o_ref[j:j + 1, vs] = jnp.sum(s_new * qcol, axis=0, keepdims=True)


def _gla_sample_out_body(x_ref, o_ref, g_ref, gn_ref, wo_ref, y_ref):
    outs = []
    for hd in range(GLA_HEADS):
        o = o_ref[:, hd * GLA_DV:(hd + 1) * GLA_DV]
        outs.append(o * lax.rsqrt(jnp.mean(o * o, axis=-1, keepdims=True) + EPS) * gn_ref[...])
    y = jnp.concatenate(outs, axis=1) * _silu(g_ref[...])
    y_ref[...] = x_ref[...] + _dot(y.astype(BF16), wo_ref[...])


def _gla_sample(x, state, gm, win, wgl, wgate, bgate, gn, wo):
    n = x.shape[0]
    dq = GLA_HEADS * GLA_DK
    weights = [gm, win, wgl, wgate, bgate]
    q, k, a, v, g = pl.pallas_call(
        _gla_sample_proj_body,
        out_shape=[jax.ShapeDtypeStruct((n, dq), F32)] * 3
        + [jax.ShapeDtypeStruct((n, D_MODEL), F32)] * 2,
        compiler_params=pltpu.CompilerParams(vmem_limit_bytes=VMEM_LIMIT),
        name="gla_sample_proj",
    )(x, *weights)

    nstep = n // GLA_SEQ_PER_STEP

    def cols(t):
        return t.reshape(nstep, GLA_SEQ_PER_STEP, dq).transpose(0, 2, 1)

    col_spec = pl.BlockSpec((1, dq, GLA_SEQ_PER_STEP), lambda i: (i, 0, 0))
    st_spec = pl.BlockSpec((GLA_SEQ_PER_STEP, GLA_HEADS, GLA_DK, GLA_DV), lambda i: (i, 0, 0, 0))
    row_spec = pl.BlockSpec((GLA_SEQ_PER_STEP, D_MODEL), lambda i: (i, 0))
    s_new, o = pl.pallas_call(
        _gla_sample_rec_body,
        grid=(nstep,),
        in_specs=[st_spec, col_spec, col_spec, col_spec, row_spec],
        out_specs=[st_spec, row_spec],
        out_shape=[jax.ShapeDtypeStruct(state.shape, F32), jax.ShapeDtypeStruct((n, D_MODEL), F32)],
        compiler_params=_cparams(("parallel",)),
        name="gla_sample_rec",
    )(state, cols(q), cols(k), cols(a), v)

    y = pl.pallas_call(
        _gla_sample_out_body,
        out_shape=jax.ShapeDtypeStruct((n, D_MODEL), F32),
        compiler_params=pltpu.CompilerParams(vmem_limit_bytes=VMEM_LIMIT),
        name="gla_sample_out",
    )(x, o, g, gn, wo)
    return y, s_new


def _t5_bucket_np(dist):
    max_exact = REL_BUCKETS // 2
    n = np.maximum(dist, 0)
    nf = np.maximum(n, max_exact).astype(np.float32)
    large = max_exact + (np.log(nf / np.float32(max_exact)) / np.float32(math.log(REL_MAX_DIST / max_exact))
                         * np.float32(REL_BUCKETS - max_exact)).astype(np.int32)
    return np.where(n < max_exact, n, np.minimum(large, REL_BUCKETS - 1)).astype(np.int32)


def _bias_tables_body(rb_ref, bd_ref, b1_ref, bdec_ref, od_ref, o1_ref, odec_ref):
    h = pl.program_id(0)
    td, t1, tdec = bd_ref[...], b1_ref[...], bdec_ref[...]
    ad = jnp.zeros(td.shape, F32)
    a1 = jnp.zeros(t1.shape, F32)
    adec = jnp.zeros(tdec.shape, F32)
    for b in range(REL_BUCKETS):
        val = rb_ref[b, h]
        ad = jnp.where(td == b, val, ad)
        a1 = jnp.where(t1 == b, val, a1)
        adec = jnp.where(tdec == b, val, adec)
    od_ref[0] = ad
    o1_ref[0] = a1
    odec_ref[0] = adec


def _bias_tables(rel_bias, past_len):
    kc = np.arange(MOBA_BLOCK)[:, None]
    qr = np.arange(MOBA_BLOCK)[None, :]
    bd = _t5_bucket_np(qr - kc)
    b1 = _t5_bucket_np(MOBA_BLOCK + qr - kc)
    nblk = past_len // MOBA_BLOCK
    kpos = np.arange(nblk)[:, None] * MOBA_BLOCK + np.arange(MOBA_BLOCK)[None, :]
    bdec = _t5_bucket_np(past_len - kpos)
    full = lambda a: pl.BlockSpec(a.shape, lambda h: (0,) * a.ndim)
    return pl.pallas_call(
        _bias_tables_body,
        grid=(MOBA_HEADS,),
        in_specs=[pl.BlockSpec(memory_space=pltpu.SMEM), full(bd), full(b1), full(bdec)],
        out_specs=[pl.BlockSpec((1,) + bd.shape, lambda h: (h, 0, 0)),
                   pl.BlockSpec((1,) + b1.shape, lambda h: (h, 0, 0)),
                   pl.BlockSpec((1,) + bdec.shape, lambda h: (h, 0, 0))],
        out_shape=[jax.ShapeDtypeStruct((MOBA_HEADS,) + bd.shape, F32),
                   jax.ShapeDtypeStruct((MOBA_HEADS,) + b1.shape, F32),
                   jax.ShapeDtypeStruct((MOBA_HEADS,) + bdec.shape, F32)],
        compiler_params=_cparams(("parallel",)),
        name="moba_bias_tables",
    )(rel_bias, jnp.asarray(bd), jnp.asarray(b1), jnp.asarray(bdec))


def _moba_qkv_body(x_ref, g_ref, w_ref, q_ref, k_ref, v_ref, *t_refs):
    h = _rms(x_ref[...], g_ref[...]).astype(BF16)
    qkv = _dot(h, w_ref[...])
    q_ref[...] = qkv[:, :D_MODEL] * (MOBA_DH ** -0.5)
    k = qkv[:, D_MODEL:2 * D_MODEL]
    v = qkv[:, 2 * D_MODEL:]
    k_ref[...] = k
    v_ref[...] = v
    if t_refs:
        t_refs[0][0] = k.T
        t_refs[1][0] = v.T


def _moba_qkv(x, g, w, tm, seq_len=None):
    rows = x.shape[0]
    row_spec = pl.BlockSpec((tm, D_MODEL), lambda i: (i, 0))
    out_specs = [row_spec] * 3
    out_shape = [jax.ShapeDtypeStruct((rows, D_MODEL), F32)] * 3
    if seq_len is not None:
        per_seq = seq_len // tm
        t_spec = pl.BlockSpec((1, D_MODEL, tm), lambda i: (i // per_seq, 0, i % per_seq))
        out_specs += [t_spec] * 2
        out_shape += [jax.ShapeDtypeStruct((rows // seq_len, D_MODEL, seq_len), F32)] * 2
    return pl.pallas_call(
        _moba_qkv_body,
        grid=(rows // tm,),
        in_specs=[row_spec, _resident(g.shape), _resident(w.shape)],
        out_specs=out_specs,
        out_shape=out_shape,
        compiler_params=_cparams(("parallel",)),
        name="moba_qkv",
    )(x, g, w)


def _topk_select(gate, valid, axis):
    n = gate.shape[axis]
    idx = lax.broadcasted_iota(jnp.int32, gate.shape, axis)
    cnt = jnp.zeros(gate.shape, jnp.int32)
    for m in range(n):
        gm = lax.slice_in_dim(gate, m, m + 1, axis=axis)
        beats = (gm > gate) | ((gm == gate) & (m < idx))
        cnt = cnt + jnp.where(beats, valid(m), 0)
    return jnp.where(cnt < MOBA_TOPK, 1.0, 0.0)


def _moba_prompt_body(rb_ref, q_ref, k_ref, v_ref, bd_ref, b1_ref, o_ref,
                      kb_scr, vt_scr, km_scr, neg_scr, m_scr, l_scr, acc_scr):
    hg = pl.program_id(1)
    qi = pl.program_id(2)
    nblk = kb_scr.shape[0]
    blk = MOBA_BLOCK
    dh = MOBA_DH

    @pl.when(qi == 0)
    def _():
        for n in range(nblk):
            kt = k_ref[n * blk:(n + 1) * blk, :]
            kb_scr[n] = kt.astype(BF16)
            km_scr[n:n + 1, :] = jnp.mean(kt, axis=0, keepdims=True)
            vt_scr[n] = v_ref[n * blk:(n + 1) * blk, :].T.astype(BF16)

    q2 = q_ref[...]
    lane = lax.broadcasted_iota(jnp.int32, (1, MOBA_HG * dh), 1)
    ki_idx = lax.broadcasted_iota(jnp.int32, (blk, blk), 0)
    qi_idx = lax.broadcasted_iota(jnp.int32, (blk, blk), 1)
    blk_idx = lax.broadcasted_iota(jnp.int32, (nblk, blk), 0)

    hs = range(MOBA_HG)
    qhb, far_bias = [], []
    for hh in hs:
        in_head = (lane >= hh * dh) & (lane < (hh + 1) * dh)
        qh = jnp.where(in_head, q2, 0.0)
        qhb.append(qh.astype(BF16))
        far_bias.append(rb_ref[REL_BUCKETS - 1, MOBA_HG * hg + hh])
        kmh = jnp.where(in_head, km_scr[...], 0.0)
        gate = lax.dot_general(kmh, qh, NT_DIMS, precision=lax.Precision.HIGHEST,
                               preferred_element_type=F32)
        keep = _topk_select(gate, lambda m: (m < qi).astype(jnp.int32), 0)
        neg_scr[hh] = jnp.where((keep > 0.0) & (blk_idx < qi), 0.0, NEG)

    def scores(ki):
        kt = kb_scr[ki]
        return [lax.dot_general(kt, qhb[hh], NT_DIMS, preferred_element_type=F32) for hh in hs]

    def values(ki, p):
        return [_dot(vt_scr[ki, hh * dh:(hh + 1) * dh, :], p[hh]) for hh in hs]

    s = scores(qi)
    p = []
    for hh in hs:
        sh = jnp.where(ki_idx <= qi_idx, s[hh] + bd_ref[hh], NEG)
        m0 = jnp.max(sh, axis=0, keepdims=True)
        ph = jnp.exp(sh - m0)
        m_scr[hh] = m0
        l_scr[hh] = jnp.sum(ph, axis=0, keepdims=True)
        p.append(ph.astype(BF16))
    pv = values(qi, p)
    for hh in hs:
        acc_scr[hh] = pv[hh]

    def past_tiles(kis, tile_bias, row_bias):
        s = [scores(ki) for ki in kis]
        p, alpha = [[] for _ in kis], []
        for hh in hs:
            rows = [neg_scr[hh, pl.ds(ki, 1), :] + row_bias[hh] for ki in kis]
            sh = [s[t][hh] if tile_bias is None else s[t][hh] + tile_bias[hh] for t in range(len(kis))]
            m_old = m_scr[hh]
            m_new = m_old
            for t in range(len(kis)):
                m_new = jnp.maximum(m_new, jnp.max(sh[t], axis=0, keepdims=True) + rows[t])
            a = jnp.exp(m_old - m_new)
            l_new = a * l_scr[hh]
            for t in range(len(kis)):
                ph = jnp.exp(sh[t] - (m_new - rows[t]))
                l_new = l_new + jnp.sum(ph, axis=0, keepdims=True)
                p[t].append(ph.astype(BF16))
            m_scr[hh] = m_new
            l_scr[hh] = l_new
            alpha.append(a)
        pv = [values(kis[t], p[t]) for t in range(len(kis))]
        for hh in hs:
            acc_scr[hh] = alpha[hh] * acc_scr[hh] + functools.reduce(
                lambda x, y: x + y, [pv[t][hh] for t in range(len(kis))])

    zero_bias = [0.0] * MOBA_HG

    @pl.when(qi >= 1)
    def _():
        past_tiles([qi - 1], [b1_ref[hh] for hh in hs], zero_bias)

    n_far = jnp.maximum(qi - 1, 0)

    def far_pair(i, carry):
        past_tiles([2 * i, 2 * i + 1], None, far_bias)
        return carry

    lax.fori_loop(0, n_far >> 1, far_pair, 0)

    @pl.when((n_far & 1) == 1)
    def _():
        past_tiles([n_far - 1], None, far_bias)

    ot = jnp.concatenate([acc_scr[hh] / l_scr[hh] for hh in hs], axis=0)
    o_ref[...] = ot.T


def _moba_prompt(rel_bias, q, k, v, bias_d, bias_1, n_seq, seq_len):
    nblk = seq_len // MOBA_BLOCK
    hgrp = MOBA_HG
    width = hgrp * MOBA_DH
    q_spec = pl.BlockSpec((MOBA_BLOCK, width), lambda b, hg, qi: (b * nblk + qi, hg))
    kv_spec = pl.BlockSpec((seq_len, width), lambda b, hg, qi: (b, hg))
    bias_spec = pl.BlockSpec((hgrp, MOBA_BLOCK, MOBA_BLOCK), lambda b, hg, qi: (hg, 0, 0))
    return pl.pallas_call(
        _moba_prompt_body,
        grid=(n_seq, MOBA_HEADS // hgrp, nblk),
        in_specs=[pl.BlockSpec(memory_space=pltpu.SMEM), q_spec, kv_spec, kv_spec, bias_spec, bias_spec],
        out_specs=q_spec,
        out_shape=jax.ShapeDtypeStruct(q.shape, F32),
        scratch_shapes=[
            pltpu.VMEM((nblk, MOBA_BLOCK, width), BF16),
            pltpu.VMEM((nblk, width, MOBA_BLOCK), BF16),
            pltpu.VMEM((nblk, width), F32),
            pltpu.VMEM((hgrp, nblk, MOBA_BLOCK), F32),
            pltpu.VMEM((hgrp, 1, MOBA_BLOCK), F32),
            pltpu.VMEM((hgrp, 1, MOBA_BLOCK), F32),
            pltpu.VMEM((hgrp, MOBA_DH, MOBA_BLOCK), F32),
        ],
        compiler_params=_cparams(("parallel", "parallel", "arbitrary")),
        name="moba_prompt",
    )(rel_bias, q, k, v, bias_d, bias_1)


def _col_to_row(col):
    n = col.shape[0]
    r = lax.broadcasted_iota(jnp.int32, (n, n), 0)
    c = lax.broadcasted_iota(jnp.int32, (n, n), 1)
    return jnp.sum(jnp.where(r == c, jnp.broadcast_to(col, (n, n)), 0.0), axis=0, keepdims=True)


def _moba_decode_body(pt_ref, q_ref, qt_ref, kn_ref, vnt_ref, rb0_ref, bdec_ref, *refs, npg):
    del pt_ref
    ck, cv = refs[:npg], refs[npg:2 * npg]
    o_ref, qb_scr, s_scr, p_scr, t_scr = refs[2 * npg:]
    heads, dh = MOBA_HEADS, MOBA_DH
    ppb = MOBA_BLOCK // PAGE_SIZE
    nb = npg // ppb

    qt = qt_ref[0]
    for h in range(heads):
        qb_scr[h] = jnp.broadcast_to(qt[:, h:h + 1], (dh, PAGE_SIZE))

    def k_step(h, carry):
        qb = qb_scr[h]
        for pg in range(npg):
            s_scr[pg, pl.ds(h, 1), :] = jnp.sum(ck[pg][0, h] * qb, axis=0, keepdims=True)
        return carry

    lax.fori_loop(0, heads, k_step, 0)

    z, m_blk = [], []
    blk_lane = lax.broadcasted_iota(jnp.int32, (heads, nb), 1)
    gate = jnp.zeros((heads, nb), F32)
    for b in range(nb):
        zb = []
        g = jnp.zeros((heads, 1), F32)
        for j in range(ppb):
            s = s_scr[ppb * b + j]
            g = g + jnp.sum(s, axis=-1, keepdims=True)
            zb.append(s + bdec_ref[ppb * b + j])
        gate = jnp.where(blk_lane == b, g, gate)
        m_blk.append(functools.reduce(jnp.maximum, [jnp.max(t, axis=-1, keepdims=True) for t in zb]))
        z += zb
    keep = _topk_select(gate, lambda m_: 1, 1)

    l_new = jnp.sum(q_ref[0] * kn_ref[0], axis=-1, keepdims=True) + rb0_ref[...]
    m_all = l_new
    for b in range(nb):
        m_all = jnp.maximum(m_all, jnp.where(keep[:, b:b + 1] > 0.0, m_blk[b], NEG))
    w_new = jnp.exp(l_new - m_all)
    den = w_new
    for b in range(nb):
        kept = keep[:, b:b + 1] > 0.0
        for j in range(ppb):
            p = jnp.exp(jnp.where(kept, z[ppb * b + j] - m_all, NEG))
            den = den + jnp.sum(p, axis=-1, keepdims=True)
            p_scr[ppb * b + j] = p

    def v_step(h, carry):
        acc = cv[0][0, h] * p_scr[0, pl.ds(h, 1), :]
        for pg in range(1, npg):
            acc = acc + cv[pg][0, h] * p_scr[pg, pl.ds(h, 1), :]
        t_scr[h] = jnp.broadcast_to(jnp.sum(acc, axis=-1, keepdims=True), (dh, PAGE_SIZE))
        return carry

    lax.fori_loop(0, heads, v_step, 0)

    head_lane = lax.broadcasted_iota(jnp.int32, (dh, heads), 1)
    tot = jnp.zeros((dh, heads), F32)
    for h in range(heads):
        tot = jnp.where(head_lane == h, t_scr[h][:, :heads], tot)
    o_ref[0] = (tot + vnt_ref[0] * _col_to_row(w_new)) / _col_to_row(den)


def _moba_decode(page_table, q, k_new, v_new, rb0, cache_k, cache_v, bias_dec):
    n_dec, n_pages = page_table.shape
    heads, dh = MOBA_HEADS, MOBA_DH
    rows = lambda t: t.reshape(n_dec, heads, dh)
    cols = lambda t: t.reshape(n_dec, heads, dh).transpose(0, 2, 1)
    row_spec = pl.BlockSpec((1, heads, dh), lambda b, pt: (b, 0, 0))
    col_spec = pl.BlockSpec((1, dh, heads), lambda b, pt: (b, 0, 0))
    page_specs = [pl.BlockSpec((1, heads, dh, PAGE_SIZE), lambda b, pt, j=j: (pt[b, j], 0, 0, 0))
                  for j in range(n_pages)]
    grid_spec = pltpu.PrefetchScalarGridSpec(
        num_scalar_prefetch=1,
        grid=(n_dec,),
        in_specs=[row_spec, col_spec, row_spec, col_spec,
                  pl.BlockSpec((heads, 1), lambda b, pt: (0, 0)),
                  pl.BlockSpec(bias_dec.shape, lambda b, pt: (0, 0, 0))] + page_specs + page_specs,
        out_specs=col_spec,
        scratch_shapes=[pltpu.VMEM((heads, dh, PAGE_SIZE), F32),
                        pltpu.VMEM((n_pages, heads, PAGE_SIZE), F32),
                        pltpu.VMEM((n_pages, heads, PAGE_SIZE), F32),
                        pltpu.VMEM((heads, dh, PAGE_SIZE), F32)],
    )
    out = pl.pallas_call(
        functools.partial(_moba_decode_body, npg=n_pages),
        grid_spec=grid_spec,
        out_shape=jax.ShapeDtypeStruct((n_dec, dh, heads), F32),
        compiler_params=_cparams(("parallel",)),
        name="moba_decode",
    )(page_table, rows(q), cols(q), rows(k_new), cols(v_new), rb0, bias_dec,
      *([cache_k] * n_pages), *([cache_v] * n_pages))
    return out.transpose(0, 2, 1).reshape(n_dec, heads * dh)


def _prep_ffn(w_up, w_down):
    return w_up.astype(BF16), w_down.astype(BF16)


def _prep_sgu(w_in, ln_g, ln_b, w_s, b_s, w_out):
    win = w_in.astype(BF16)
    lng = ln_g.reshape(1, A_HALF)
    lnb = ln_b.reshape(1, A_HALF)
    wout = w_out.astype(BF16)
    causal = jnp.tril(jnp.ones((A_CHUNK, A_CHUNK), dtype=bool))
    wm_p = jnp.where(causal[None], w_s, 0.0).astype(BF16)
    sb_p = jnp.broadcast_to(b_s[:, :, None], (A_GROUPS, A_CHUNK, A_GROUP_DIM))
    eye = jnp.eye(A_CHUNK, dtype=F32)
    wm_s = (w_s[:, 0, 0][:, None, None] * eye[None]).astype(BF16)
    sb_s = jnp.broadcast_to(b_s[:, 0][:, None, None], (A_GROUPS, A_CHUNK, A_GROUP_DIM))
    return (win, lng, lnb, wout), (wm_p, sb_p), (wm_s, sb_s)


def _prep_gla(w_in, w_gate, b_gate, gn, w_out):
    dq, dv = GLA_HEADS * GLA_DK, GLA_HEADS * GLA_DV
    pad = LANES - GLA_GATE_RANK
    wgl = jnp.pad(w_in[:, 2 * dq + 2 * dv:], ((0, 0), (0, pad))).astype(BF16)
    wgate = jnp.pad(w_gate, ((0, pad), (0, 0))).astype(BF16)
    return (w_in.astype(BF16), wgl, wgate, b_gate.reshape(1, dq), gn.reshape(1, GLA_DV),
            w_out.astype(BF16))


def kernel(x_prompt, x_sample, cache_k, cache_v, page_table, state_gla, norm_mix, norm_ffn, norm_final,
           w_in_a, ln_a_g, ln_a_b, w_s_a, b_s_a, w_out_a, w_in_b, w_gate_b, b_gate_b, gn_b, w_out_b,
           w_in_c, w_out_c, rel_bias, w_up, w_down):
    n_pr, l_pr, _ = x_prompt.shape
    n_dec, l_dec, _ = x_sample.shape
    assert l_dec == 1 and l_pr % MOBA_BLOCK == 0 and l_pr % GLA_C == 0
    depth = norm_mix.shape[0]
    past_len = page_table.shape[1] * PAGE_SIZE
    assert past_len % MOBA_BLOCK == 0
    tm_p, tm_s = 512, n_dec

    xp = x_prompt.reshape(n_pr * l_pr, D_MODEL)
    xs = x_sample.reshape(n_dec, D_MODEL)
    gfin = norm_final.reshape(1, D_MODEL)
    k_p, v_p, k_s, v_s, gla_p, gla_s, sgu_s = [], [], [], [], [], [], []

    for i in range(depth):
        kind, j = i % 3, i // 3
        gm = norm_mix[i].reshape(1, D_MODEL)
        gf = norm_ffn[i].reshape(1, D_MODEL)
        wup, wdn = _prep_ffn(w_up[i], w_down[i])
        last = gfin if i == depth - 1 else None
        proj_p = proj_s = None
        if kind == 0:
            shared, mode_p, mode_s = _prep_sgu(w_in_a[j], ln_a_g[j], ln_a_b[j], w_s_a[j], b_s_a[j], w_out_a[j])
            win, lng, lnb, wout = shared
            (xp,) = _sgu(xp, gm, win, lng, lnb, *mode_p, wout, tm_p, False)
            xs, v_rows = _sgu(xs, gm, win, lng, lnb, *mode_s, wout, tm_s, True)
            sgu_s.append(v_rows.reshape(n_dec, l_dec, A_HALF))
        elif kind == 1:
            wts = _prep_gla(w_in_b[j], w_gate_b[j], b_gate_b[j], gn_b[j], w_out_b[j])
            xp, sp = _gla_prompt(xp, gm, *wts, n_pr, l_pr)
            xs, ss = _gla_sample(xs, state_gla[j], gm, *wts)
            gla_p.append(sp)
            gla_s.append(ss)
        else:
            w_in = w_in_c[j].astype(BF16)
            bias_d, bias_1, bias_dec = _bias_tables(rel_bias, past_len)
            qp, kp, vp, kpt, vpt = _moba_qkv(xp, gm, w_in, tm_p, seq_len=l_pr)
            op = _moba_prompt(rel_bias, qp, kp, vp, bias_d, bias_1, n_pr, l_pr)
            qs, kn, vn = _moba_qkv(xs, gm, w_in, tm_s)
            os_ = _moba_decode(page_table, qs, kn, vn, rel_bias[0].reshape(MOBA_HEADS, 1),
                               cache_k[j].transpose(0, 2, 3, 1), cache_v[j].transpose(0, 2, 3, 1),
                               bias_dec.reshape(MOBA_HEADS, -1, PAGE_SIZE).transpose(1, 0, 2))
            wo_c = w_out_c[j].astype(BF16)
            proj_p, proj_s = (op, wo_c), (os_, wo_c)
            heads_last = lambda t: t.reshape(n_pr, MOBA_HEADS, MOBA_DH, l_pr).transpose(0, 3, 1, 2)
            k_p.append(heads_last(kpt))
            v_p.append(heads_last(vpt))
            k_s.append(kn.reshape(n_dec, l_dec, MOBA_HEADS, MOBA_DH))
            v_s.append(vn.reshape(n_dec, l_dec, MOBA_HEADS, MOBA_DH))
        if proj_p is None:
            xp = _ffn(xp, gf, wup, wdn, tm_p, gfinal=last)
            xs = _ffn(xs, gf, wup, wdn, tm_s, gfinal=last)
        else:
            xp = _ffn(xp, gf, wup, wdn, tm_p, a=proj_p[0], wo=proj_p[1], gfinal=last)
            xs = _ffn(xs, gf, wup, wdn, tm_s, a=proj_s[0], wo=proj_s[1], gfinal=last)

    return (xp.reshape(n_pr, l_pr, D_MODEL), xs.reshape(n_dec, l_dec, D_MODEL),
            jnp.stack(k_p), jnp.stack(v_p), jnp.stack(k_s), jnp.stack(v_s),
            jnp.stack(gla_p), jnp.stack(gla_s), jnp.stack(sgu_s))
```

```python
import functools
import math

import numpy as np
import jax
import jax.numpy as jnp
from jax import lax
from jax.experimental import pallas as pl
from jax.experimental.pallas import tpu as pltpu

F32 = jnp.float32
BF16 = jnp.bfloat16

D_MODEL = 1024
EPS = 1e-6
NEG = -1e30

LANES = 128
SUBLANES = 8
MXU_DIM = 256
VMEM_LIMIT = 56 * 1024 * 1024

A_CHUNK = 128
A_HALF = 3 * D_MODEL
A_GROUPS = 8
A_GROUP_DIM = A_HALF // A_GROUPS
A_PANEL = 2 * A_GROUP_DIM
A_NPANEL = A_HALF // A_PANEL

GLA_HEADS = 4
GLA_DK = 128
GLA_DV = 256
GLA_GATE_RANK = 16
GLA_GATE_NORM = 16.0
GLA_C = 256

MOBA_HEADS = 16
MOBA_DH = 64
MOBA_BLOCK = 256
MOBA_TOPK = 3
MOBA_HG = MXU_DIM // MOBA_DH
MOBA_ONES_ROWS = 2 * SUBLANES
LOG2E = math.log2(math.e)
PAGE_SIZE = 128
REL_BUCKETS = 32
REL_MAX_DIST = 128

D_FF = 2816
FF_CHUNK = MXU_DIM
FF_NCHUNK = D_FF // FF_CHUNK

NT_DIMS = (((1,), (1,)), ((), ()))
TN_DIMS = (((0,), (0,)), ((), ()))


def _cparams(sem):
    return pltpu.CompilerParams(dimension_semantics=sem, vmem_limit_bytes=VMEM_LIMIT)


def _resident(w):
    if isinstance(w, tuple):
        arr, layer = w
        nd = arr.ndim - 1
        return pl.BlockSpec((None,) + arr.shape[1:], lambda *_: (layer,) + (0,) * nd,
                            pipeline_mode=pl.Buffered(1))
    nd = w.ndim
    return pl.BlockSpec(w.shape, lambda *_: (0,) * nd, pipeline_mode=pl.Buffered(1))


def _operand(w):
    return w[0] if isinstance(w, tuple) else w


def _rms(x, g):
    return x * lax.rsqrt(jnp.mean(x * x, axis=-1, keepdims=True) + EPS) * g


def _gelu(x):
    return 0.5 * x * (1.0 + lax.erf(x * (1.0 / math.sqrt(2.0))))


def _silu(x):
    return x * jax.nn.sigmoid(x)


def _dot(a, b):
    return jnp.dot(a, b, preferred_element_type=F32)


def _ffn_body(*refs, has_proj, final):
    refs = list(refs)
    x_ref = refs.pop(0)
    a_ref = wo_ref = gf_ref = None
    if has_proj:
        a_ref = refs.pop(0)
        wo_ref = refs.pop(0)
    g_ref = refs.pop(0)
    wup_ref = refs.pop(0)
    wdn_ref = refs.pop(0)
    if final:
        gf_ref = refs.pop(0)
    o_ref = refs.pop(0)

    x = x_ref[...]
    if has_proj:
        x = x + _dot(a_ref[...].astype(BF16), wo_ref[...])
    h = _rms(x, g_ref[...]).astype(BF16)
    o_ref[...] = x

    for c in range(FF_NCHUNK):
        cols = slice(c * FF_CHUNK, (c + 1) * FF_CHUNK)
        gate = _dot(h, wup_ref[:, cols])
        up = _dot(h, wup_ref[:, D_FF + c * FF_CHUNK:D_FF + (c + 1) * FF_CHUNK])
        act = (_silu(gate) * up).astype(BF16)
        o_ref[...] += _dot(act, wdn_ref[cols, :])
    if final:
        o_ref[...] = _rms(o_ref[...], gf_ref[...])


def _ffn(x, g, wup, wdn, tm, a=None, wo=None, gfinal=None):
    rows = x.shape[0]
    row_spec = pl.BlockSpec((tm, D_MODEL), lambda i: (i, 0))
    args = [x]
    specs = [row_spec]
    if a is not None:
        args += [a, wo]
        specs += [row_spec, _resident(wo)]
    args += [g, _operand(wup), _operand(wdn)]
    specs += [_resident(g), _resident(wup), _resident(wdn)]
    if gfinal is not None:
        args.append(gfinal)
        specs.append(_resident(gfinal))
    return pl.pallas_call(
        functools.partial(_ffn_body, has_proj=a is not None, final=gfinal is not None),
        grid=(rows // tm,),
        in_specs=specs,
        out_specs=row_spec,
        out_shape=jax.ShapeDtypeStruct((rows, D_MODEL), F32),
        compiler_params=_cparams(("parallel",)),
        name="ffn",
    )(*args)


def _sgu_body(*refs, tm, emit_v):
    (x_ref, g_ref, win_ref, lng_ref, lnb_ref, wm_ref, sb_ref, wout_ref) = refs[:8]
    if emit_v:
        o_ref, vout_ref, vbuf, ubuf, sbuf = refs[8:]
    else:
        o_ref, vbuf, ubuf, sbuf = refs[8:]
        vout_ref = None
    panel = lambda p, base=0: slice(base + p * A_PANEL, base + (p + 1) * A_PANEL)

    x = x_ref[...]
    h = _rms(x, g_ref[...]).astype(BF16)

    rsum = jnp.zeros((tm, 1), F32)
    for p in range(A_NPANEL):
        v = _gelu(_dot(h, win_ref[:, panel(p, A_HALF)]))
        vbuf[p] = v
        rsum = rsum + jnp.sum(v, axis=-1, keepdims=True)
    for p in range(A_NPANEL):
        ubuf[p] = _gelu(_dot(h, win_ref[:, panel(p)]))
    mean = rsum * (1.0 / A_HALF)
    ssq = jnp.zeros((tm, 1), F32)
    for p in range(A_NPANEL):
        d = vbuf[p] - mean
        ssq = ssq + jnp.sum(d * d, axis=-1, keepdims=True)
    rstd = lax.rsqrt(ssq * (1.0 / A_HALF) + EPS)

    acc = x
    for p in range(A_NPANEL):
        vn = (vbuf[p] - mean) * rstd * lng_ref[:, panel(p)] + lnb_ref[:, panel(p)]
        if emit_v:
            vout_ref[:, p * A_PANEL:(p + 1) * A_PANEL] = vn
        vnb = vn.astype(BF16)
        for c in range(tm // A_CHUNK):
            for gg in range(2):
                grp = 2 * p + gg
                rs = slice(c * A_CHUNK, (c + 1) * A_CHUNK)
                cs = slice(gg * A_GROUP_DIM, (gg + 1) * A_GROUP_DIM)
                sbuf[rs, cs] = _dot(wm_ref[grp], vnb[rs, cs]) + sb_ref[grp]
        acc = acc + _dot((ubuf[p] * sbuf[...]).astype(BF16), wout_ref[panel(p), :])
    o_ref[...] = acc


def _sgu(x, g, win, lng, lnb, wm, sb, wout, tm, emit_v):
    rows = x.shape[0]
    row_spec = pl.BlockSpec((tm, D_MODEL), lambda i: (i, 0))
    weights = [g, win, lng, lnb, wm, sb, wout]
    out_shape = [jax.ShapeDtypeStruct((rows, D_MODEL), F32)]
    out_specs = [row_spec]
    if emit_v:
        out_shape.append(jax.ShapeDtypeStruct((rows, A_HALF), F32))
        out_specs.append(pl.BlockSpec((tm, A_HALF), lambda i: (i, 0)))
    return pl.pallas_call(
        functools.partial(_sgu_body, tm=tm, emit_v=emit_v),
        grid=(rows // tm,),
        in_specs=[row_spec] + [_resident(w) for w in weights],
        out_specs=out_specs,
        out_shape=out_shape,
        scratch_shapes=[pltpu.VMEM((A_NPANEL, tm, A_PANEL), F32),
                        pltpu.VMEM((A_NPANEL, tm, A_PANEL), F32),
                        pltpu.VMEM((tm, A_PANEL), F32)],
        compiler_params=_cparams(("parallel",)),
        name="sgu",
    )(x, *[_operand(w) for w in weights])


def _log_sigmoid(z):
    return jnp.minimum(z, 0.0) - jnp.log1p(jnp.exp(-jnp.abs(z)))


def _gla_intra(qh, kh, cum, same_block):
    c = qh.shape[0]
    r = lax.broadcasted_iota(jnp.int32, (c, 1), 0)
    a = jnp.zeros((c, c), F32)
    m = c
    while m >= 2:
        half = m // 2
        pos = r & (m - 1)
        upper = pos >= half
        if m >= 2 * SUBLANES:
            pieces = []
            for b in range(c // m):
                row = b * m + half - 1
                pieces.append(jnp.broadcast_to(cum[row:row + 1, :], (m, cum.shape[1])))
            ref_q = pieces[0] if len(pieces) == 1 else jnp.concatenate(pieces, axis=0)
            ref_k = ref_q
        else:
            ref_q = cum
            for d in range(1, half + 1):
                ref_q = jnp.where(pos - (half - 1) == d, pltpu.roll(cum, d, 0), ref_q)
            ref_k = cum
            for o in range(1, half):
                ref_k = jnp.where((half - 1) - pos == o, pltpu.roll(cum, c - o, 0), ref_k)
        decay = jnp.exp(jnp.where(upper, cum - ref_q, ref_k - cum))
        x = jnp.where(upper, qh, kh) * decay
        qm = jnp.where(upper, x, 0.0).astype(BF16)
        km = jnp.where(upper, 0.0, x).astype(BF16)
        am = lax.dot_general(qm, km, NT_DIMS, preferred_element_type=F32)
        if m < c:
            am = jnp.where(same_block[m], am, 0.0)
        a = a + am
        m = half
    return a


def _gla_project(h, win_ref, wgl_ref, wgate_ref, bgate_ref):
    dq, dv = GLA_HEADS * GLA_DK, GLA_HEADS * GLA_DV
    q = _dot(h, win_ref[:, :dq]) * (GLA_DK ** -0.5)
    k = _dot(h, win_ref[:, dq:2 * dq])
    v = _dot(h, win_ref[:, 2 * dq:2 * dq + dv])
    g = _dot(h, win_ref[:, 2 * dq + dv:2 * dq + 2 * dv])
    gl = _dot(h, wgl_ref[...])
    z = _dot(gl.astype(BF16), wgate_ref[...]) + bgate_ref[...]
    return q, k, v, g, z


def _gla_prompt_body(x_ref, gm_ref, win_ref, wgl_ref, wgate_ref, bgate_ref,
                     gn_ref, wo_ref, o_ref, st_ref, s_scr):
    ci = pl.program_id(1)
    c = GLA_C

    @pl.when(ci == 0)
    def _():
        s_scr[...] = jnp.zeros_like(s_scr)

    x = x_ref[...]
    h = _rms(x, gm_ref[...]).astype(BF16)
    q, k, v, g, z = _gla_project(h, win_ref, wgl_ref, wgate_ref, bgate_ref)
    la = _log_sigmoid(z) * (1.0 / GLA_GATE_NORM)

    rowi = lax.broadcasted_iota(jnp.int32, (c, c), 0)
    coli = lax.broadcasted_iota(jnp.int32, (c, c), 1)
    tril = (coli <= rowi).astype(F32)
    cum = jnp.dot(tril, la, precision=lax.Precision.HIGHEST, preferred_element_type=F32)

    same_block = {}
    m = c // 2
    while m >= 2:
        sh = int(math.log2(m))
        same_block[m] = (rowi >> sh) == (coli >> sh)
        m //= 2

    outs = []
    for hd in range(GLA_HEADS):
        ks = slice(hd * GLA_DK, (hd + 1) * GLA_DK)
        vs = slice(hd * GLA_DV, (hd + 1) * GLA_DV)
        qh, kh, cumh, vh = q[:, ks], k[:, ks], cum[:, ks], v[:, vs]
        vhb = vh.astype(BF16)
        total = cumh[c - 1:c, :]
        st = s_scr[hd]
        o = lax.dot_general((qh * jnp.exp(cumh)).astype(BF16), st.astype(BF16), NT_DIMS,
                            preferred_element_type=F32)
        a = _gla_intra(qh, kh, cumh, same_block)
        o = o + _dot(a.astype(BF16), vhb)
        o = o + jnp.sum(qh * kh, axis=-1, keepdims=True) * vh
        kdec = (kh * jnp.exp(total - cumh)).astype(BF16)
        st_new = st * jnp.exp(total) + lax.dot_general(vhb, kdec, TN_DIMS,
                                                       preferred_element_type=F32)
        s_scr[hd] = st_new
        o = o * lax.rsqrt(jnp.mean(o * o, axis=-1, keepdims=True) + EPS) * gn_ref[...]
        outs.append(o)
    y = jnp.concatenate(outs, axis=1) * _silu(g)
    o_ref[...] = x + _dot(y.astype(BF16), wo_ref[...])

    @pl.when(ci == pl.num_programs(1) - 1)
    def _():
        for hd in range(GLA_HEADS):
            st_ref[0, hd] = s_scr[hd].T


def _gla_prompt(x, gm, win, wgl, wgate, bgate, gn, wo, n_seq, seq_len):
    n_chunk = seq_len // GLA_C
    weights = [gm, win, wgl, wgate, bgate, gn, wo]
    row_spec = pl.BlockSpec((GLA_C, D_MODEL), lambda b, c: (b * n_chunk + c, 0))
    return pl.pallas_call(
        _gla_prompt_body,
        grid=(n_seq, n_chunk),
        in_specs=[row_spec] + [_resident(w) for w in weights],
        out_specs=[row_spec,
                   pl.BlockSpec((1, GLA_HEADS, GLA_DK, GLA_DV), lambda b, c: (b, 0, 0, 0))],
        out_shape=[jax.ShapeDtypeStruct(x.shape, F32),
                   jax.ShapeDtypeStruct((n_seq, GLA_HEADS, GLA_DK, GLA_DV), F32)],
        scratch_shapes=[pltpu.VMEM((GLA_HEADS, GLA_DV, GLA_DK), F32)],
        compiler_params=_cparams(("parallel", "arbitrary")),
        name="gla_prompt",
    )(x, *weights)


def _gla_sample_proj_body(x_ref, gm_ref, win_ref, wgl_ref, wgate_ref, bgate_ref,
                          q_ref, k_ref, a_ref, v_ref, g_ref):
    h = _rms(x_ref[...], gm_ref[...]).astype(BF16)
    q_ref[...], k_ref[...], v_ref[...], g_ref[...], z = _gla_project(
        h, win_ref, wgl_ref, wgate_ref, bgate_ref)
    a_ref[...] = jnp.exp(_log_sigmoid(z) * (1.0 / GLA_GATE_NORM))


GLA_SEQ_PER_STEP = SUBLANES


def _gla_sample_rec_body(s_ref, qc_ref, kc_ref, ac_ref, v_ref, so_ref, o_ref):
    for j in range(GLA_SEQ_PER_STEP):
        for hd in range(GLA_HEADS):
            ks = slice(hd * GLA_DK, (hd + 1) * GLA_DK)
            vs = slice(hd * GLA_DV, (hd + 1) * GLA_DV)
            acol = ac_ref[0, ks, j:j + 1]
            kcol = kc_ref[0, ks, j:j + 1]
            qcol = qc_ref[0, ks, j:j + 1]
            vrow = v_ref[j:j + 1, vs]
            s_new = s_ref[j, hd] * acol + kcol * vrow
            so_ref[j, hd] = s_new
            o_ref[j:j + 1, vs] = jnp.sum(s_new * qcol, axis=0, keepdims=True)


def _gla_sample_out_body(x_ref, o_ref, g_ref, gn_ref, wo_ref, y_ref):
    outs = []
    for hd in range(GLA_HEADS):
        o = o_ref[:, hd * GLA_DV:(hd + 1) * GLA_DV]
        outs.append(o * lax.rsqrt(jnp.mean(o * o, axis=-1, keepdims=True) + EPS) * gn_ref[...])
    y = jnp.concatenate(outs, axis=1) * _silu(g_ref[...])
    y_ref[...] = x_ref[...] + _dot(y.astype(BF16), wo_ref[...])


def _gla_sample(x, state, gm, win, wgl, wgate, bgate, gn, wo):
    n = x.shape[0]
    dq = GLA_HEADS * GLA_DK
    weights = [gm, win, wgl, wgate, bgate]
    q, k, a, v, g = pl.pallas_call(
        _gla_sample_proj_body,
        out_shape=[jax.ShapeDtypeStruct((n, dq), F32)] * 3
        + [jax.ShapeDtypeStruct((n, D_MODEL), F32)] * 2,
        compiler_params=pltpu.CompilerParams(vmem_limit_bytes=VMEM_LIMIT),
        name="gla_sample_proj",
    )(x, *weights)

    nstep = n // GLA_SEQ_PER_STEP

    def cols(t):
        return t.reshape(nstep, GLA_SEQ_PER_STEP, dq).transpose(0, 2, 1)

    col_spec = pl.BlockSpec((1, dq, GLA_SEQ_PER_STEP), lambda i: (i, 0, 0))
    st_spec = pl.BlockSpec((GLA_SEQ_PER_STEP, GLA_HEADS, GLA_DK, GLA_DV), lambda i: (i, 0, 0, 0))
    row_spec = pl.BlockSpec((GLA_SEQ_PER_STEP, D_MODEL), lambda i: (i, 0))
    s_new, o = pl.pallas_call(
        _gla_sample_rec_body,
        grid=(nstep,),
        in_specs=[st_spec, col_spec, col_spec, col_spec, row_spec],
        out_specs=[st_spec, row_spec],
        out_shape=[jax.ShapeDtypeStruct(state.shape, F32), jax.ShapeDtypeStruct((n, D_MODEL), F32)],
        compiler_params=_cparams(("parallel",)),
        name="gla_sample_rec",
    )(state, cols(q), cols(k), cols(a), v)

    y = pl.pallas_call(
        _gla_sample_out_body,
        out_shape=jax.ShapeDtypeStruct((n, D_MODEL), F32),
        compiler_params=pltpu.CompilerParams(vmem_limit_bytes=VMEM_LIMIT),
        name="gla_sample_out",
    )(x, o, g, gn, wo)
    return y, s_new


def _t5_bucket_np(dist):
    max_exact = REL_BUCKETS // 2
    n = np.maximum(dist, 0)
    nf = np.maximum(n, max_exact).astype(np.float32)
    large = max_exact + (np.log(nf / np.float32(max_exact)) / np.float32(math.log(REL_MAX_DIST / max_exact))
                         * np.float32(REL_BUCKETS - max_exact)).astype(np.int32)
    return np.where(n < max_exact, n, np.minimum(large, REL_BUCKETS - 1)).astype(np.int32)


def _bias_tables_body(rb_ref, bd_ref, b1_ref, bdec_ref, od_ref, o1_ref, odec_ref):
    h = pl.program_id(0)
    td, t1, tdec = bd_ref[...], b1_ref[...], bdec_ref[...]
    ad = jnp.zeros(td.shape, F32)
    a1 = jnp.zeros(t1.shape, F32)
    adec = jnp.zeros(tdec.shape, F32)
    for b in range(REL_BUCKETS):
        val = rb_ref[b, h]
        ad = jnp.where(td == b, val, ad)
        a1 = jnp.where(t1 == b, val, a1)
        adec = jnp.where(tdec == b, val, adec)
    od_ref[0] = ad * LOG2E
    o1_ref[0] = a1 * LOG2E
    odec_ref[0] = adec


def _bias_tables(rel_bias, past_len):
    kc = np.arange(MOBA_BLOCK)[:, None]
    qr = np.arange(MOBA_BLOCK)[None, :]
    bd = _t5_bucket_np(qr - kc)
    b1 = _t5_bucket_np(MOBA_BLOCK + qr - kc)
    nblk = past_len // MOBA_BLOCK
    kpos = np.arange(nblk)[:, None] * MOBA_BLOCK + np.arange(MOBA_BLOCK)[None, :]
    bdec = _t5_bucket_np(past_len - kpos)
    full = lambda a: pl.BlockSpec(a.shape, lambda h: (0,) * a.ndim)
    return pl.pallas_call(
        _bias_tables_body,
        grid=(MOBA_HEADS,),
        in_specs=[pl.BlockSpec(memory_space=pltpu.SMEM), full(bd), full(b1), full(bdec)],
        out_specs=[pl.BlockSpec((1,) + bd.shape, lambda h: (h, 0, 0)),
                   pl.BlockSpec((1,) + b1.shape, lambda h: (h, 0, 0)),
                   pl.BlockSpec((1,) + bdec.shape, lambda h: (h, 0, 0))],
        out_shape=[jax.ShapeDtypeStruct((MOBA_HEADS,) + bd.shape, F32),
                   jax.ShapeDtypeStruct((MOBA_HEADS,) + b1.shape, F32),
                   jax.ShapeDtypeStruct((MOBA_HEADS,) + bdec.shape, F32)],
        compiler_params=_cparams(("parallel",)),
        name="moba_bias_tables",
    )(rel_bias, jnp.asarray(bd), jnp.asarray(b1), jnp.asarray(bdec))


def _moba_qkv_body(x_ref, g_ref, w_ref, q_ref, k_ref, v_ref, *t_refs):
    h = _rms(x_ref[...], g_ref[...]).astype(BF16)
    qkv = _dot(h, w_ref[...])
    q_ref[...] = qkv[:, :D_MODEL] * (MOBA_DH ** -0.5)
    k = qkv[:, D_MODEL:2 * D_MODEL]
    v = qkv[:, 2 * D_MODEL:]
    k_ref[...] = k
    v_ref[...] = v
    if t_refs:
        t_refs[0][0] = k.T
        t_refs[1][0] = v.T


def _moba_qkv(x, g, w, tm, seq_len=None):
    rows = x.shape[0]
    row_spec = pl.BlockSpec((tm, D_MODEL), lambda i: (i, 0))
    out_specs = [row_spec] * 3
    out_shape = [jax.ShapeDtypeStruct((rows, D_MODEL), F32)] * 3
    if seq_len is not None:
        per_seq = seq_len // tm
        t_spec = pl.BlockSpec((1, D_MODEL, tm), lambda i: (i // per_seq, 0, i % per_seq))
        out_specs += [t_spec] * 2
        out_shape += [jax.ShapeDtypeStruct((rows // seq_len, D_MODEL, seq_len), F32)] * 2
    return pl.pallas_call(
        _moba_qkv_body,
        grid=(rows // tm,),
        in_specs=[row_spec, _resident(g), _resident(w)],
        out_specs=out_specs,
        out_shape=out_shape,
        compiler_params=_cparams(("parallel",)),
        name="moba_qkv",
    )(x, g, w)


def _topk_select(gate, valid, axis):
    n = gate.shape[axis]
    idx = lax.broadcasted_iota(jnp.int32, gate.shape, axis)
    cnt = jnp.zeros(gate.shape, jnp.int32)
    for m in range(n):
        gm = lax.slice_in_dim(gate, m, m + 1, axis=axis)
        beats = (gm > gate) | ((gm == gate) & (m < idx))
        cnt = cnt + jnp.where(beats, valid(m), 0)
    return jnp.where(cnt < MOBA_TOPK, 1.0, 0.0)


def _moba_prompt_body(rb_ref, q_ref, k_ref, v_ref, bd_ref, b1_ref, o_ref,
                      kb_scr, vt_scr, km_scr, neg_scr, m_scr, acc_scr):
    hg = pl.program_id(1)
    qi = pl.program_id(2)
    nblk = kb_scr.shape[0]
    blk = MOBA_BLOCK
    dh = MOBA_DH
    width = MOBA_HG * dh

    @pl.when(qi == 0)
    def _():
        ones = jnp.ones((MOBA_ONES_ROWS, blk), BF16)
        for n in range(nblk):
            kt = k_ref[n * blk:(n + 1) * blk, :]
            kb_scr[n] = kt.astype(BF16)
            km_scr[n:n + 1, :] = jnp.mean(kt, axis=0, keepdims=True)
            vt = v_ref[n * blk:(n + 1) * blk, :].T.astype(BF16)
            for hh in range(MOBA_HG):
                vt_scr[n, hh, :dh, :] = vt[hh * dh:(hh + 1) * dh, :]
                vt_scr[n, hh, dh:, :] = ones

    q2 = q_ref[...]
    lane = lax.broadcasted_iota(jnp.int32, (1, MOBA_HG * dh), 1)
    ki_idx = lax.broadcasted_iota(jnp.int32, (blk, blk), 0)
    qi_idx = lax.broadcasted_iota(jnp.int32, (blk, blk), 1)
    blk_idx = lax.broadcasted_iota(jnp.int32, (nblk, blk), 0)

    hs = range(MOBA_HG)
    qhb, far_bias = [], []
    for hh in hs:
        in_head = (lane >= hh * dh) & (lane < (hh + 1) * dh)
        qh = jnp.where(in_head, q2, 0.0)
        qhb.append((qh * LOG2E).astype(BF16))
        far_bias.append(rb_ref[REL_BUCKETS - 1, MOBA_HG * hg + hh] * LOG2E)
        kmh = jnp.where(in_head, km_scr[...], 0.0)
        gate = lax.dot_general(kmh, qh, NT_DIMS, precision=lax.Precision.HIGHEST,
                               preferred_element_type=F32)
        keep = _topk_select(gate, lambda m: (m < qi).astype(jnp.int32), 0)
        neg_scr[hh] = jnp.where((keep > 0.0) & (blk_idx < qi), 0.0, NEG)

    def scores(ki, n):
        kt = kb_scr[pl.ds(ki, n)].reshape(n * blk, width)
        return [lax.dot_general(kt, qhb[hh], NT_DIMS, preferred_element_type=F32) for hh in hs]

    def values(ki, p):
        return [_dot(vt_scr[ki, hh], p[hh]) for hh in hs]

    def weights(x):
        return jnp.exp2(x.astype(BF16))

    s = scores(qi, 1)
    p = []
    for hh in hs:
        sh = jnp.where(ki_idx <= qi_idx, s[hh] + bd_ref[hh], NEG)
        m0 = jnp.max(sh, axis=0, keepdims=True)
        m_scr[hh] = m0
        p.append(weights(sh - m0))
    pv = values(qi, p)
    for hh in hs:
        acc_scr[hh] = pv[hh]

    def past_tiles(ki0, n, tile_bias, row_bias):
        s = scores(ki0, n)
        p, alpha = [[] for _ in range(n)], []
        for hh in hs:
            rows = [neg_scr[hh, pl.ds(ki0 + t, 1), :] + row_bias[hh] for t in range(n)]
            sh = [s[hh][t * blk:(t + 1) * blk, :] for t in range(n)]
            if tile_bias is not None:
                sh = [x + tile_bias[hh] for x in sh]
            m_old = m_scr[hh]
            m_new = m_old
            for t in range(n):
                m_new = jnp.maximum(m_new, jnp.max(sh[t], axis=0, keepdims=True) + rows[t])
            for t in range(n):
                p[t].append(weights(sh[t] - (m_new - rows[t])))
            m_scr[hh] = m_new
            alpha.append(jnp.exp2(m_old - m_new))
        pv = [values(ki0 + t, p[t]) for t in range(n)]
        for hh in hs:
            acc_scr[hh] = alpha[hh] * acc_scr[hh] + functools.reduce(
                lambda x, y: x + y, [pv[t][hh] for t in range(n)])

    zero_bias = [0.0] * MOBA_HG

    @pl.when(qi >= 1)
    def _():
        past_tiles(qi - 1, 1, [b1_ref[hh] for hh in hs], zero_bias)

    n_far = jnp.maximum(qi - 1, 0)

    def far_pair(i, carry):
        past_tiles(2 * i, 2, None, far_bias)
        return carry

    lax.fori_loop(0, n_far >> 1, far_pair, 0)

    @pl.when((n_far & 1) == 1)
    def _():
        past_tiles(n_far - 1, 1, None, far_bias)

    ot = jnp.concatenate([acc_scr[hh][:dh, :] / acc_scr[hh][dh:dh + 1, :] for hh in hs], axis=0)
    o_ref[...] = ot.T


def _moba_prompt(rel_bias, q, k, v, bias_d, bias_1, n_seq, seq_len):
    nblk = seq_len // MOBA_BLOCK
    hgrp = MOBA_HG
    width = hgrp * MOBA_DH
    q_spec = pl.BlockSpec((MOBA_BLOCK, width), lambda b, hg, qi: (b * nblk + qi, hg))
    kv_spec = pl.BlockSpec((seq_len, width), lambda b, hg, qi: (b, hg))
    bias_spec = pl.BlockSpec((hgrp, MOBA_BLOCK, MOBA_BLOCK), lambda b, hg, qi: (hg, 0, 0))
    return pl.pallas_call(
        _moba_prompt_body,
        grid=(n_seq, MOBA_HEADS // hgrp, nblk),
        in_specs=[pl.BlockSpec(memory_space=pltpu.SMEM), q_spec, kv_spec, kv_spec, bias_spec, bias_spec],
        out_specs=q_spec,
        out_shape=jax.ShapeDtypeStruct(q.shape, F32),
        scratch_shapes=[
            pltpu.VMEM((nblk, MOBA_BLOCK, width), BF16),
            pltpu.VMEM((nblk, hgrp, MOBA_DH + MOBA_ONES_ROWS, MOBA_BLOCK), BF16),
            pltpu.VMEM((nblk, width), F32),
            pltpu.VMEM((hgrp, nblk, MOBA_BLOCK), F32),
            pltpu.VMEM((hgrp, 1, MOBA_BLOCK), F32),
            pltpu.VMEM((hgrp, MOBA_DH + MOBA_ONES_ROWS, MOBA_BLOCK), F32),
        ],
        compiler_params=_cparams(("parallel", "parallel", "arbitrary")),
        name="moba_prompt",
    )(rel_bias, q, k, v, bias_d, bias_1)


def _col_to_row(col):
    n = col.shape[0]
    r = lax.broadcasted_iota(jnp.int32, (n, n), 0)
    c = lax.broadcasted_iota(jnp.int32, (n, n), 1)
    return jnp.sum(jnp.where(r == c, jnp.broadcast_to(col, (n, n)), 0.0), axis=0, keepdims=True)


def _moba_decode_body(pt_ref, q_ref, qt_ref, kn_ref, vnt_ref, rb0_ref, bdec_ref, *refs, npg):
    del pt_ref
    ck, cv = refs[:npg], refs[npg:2 * npg]
    o_ref, qb_scr, s_scr, p_scr, t_scr = refs[2 * npg:]
    heads, dh = MOBA_HEADS, MOBA_DH
    ppb = MOBA_BLOCK // PAGE_SIZE
    nb = npg // ppb

    qt = qt_ref[0]
    for h in range(heads):
        qb_scr[h] = jnp.broadcast_to(qt[:, h:h + 1], (dh, PAGE_SIZE))

    def k_step(h, carry):
        qb = qb_scr[h]
        for pg in range(npg):
            s_scr[pg, pl.ds(h, 1), :] = jnp.sum(ck[pg][0, h] * qb, axis=0, keepdims=True)
        return carry

    lax.fori_loop(0, heads, k_step, 0)

    z, m_blk = [], []
    blk_lane = lax.broadcasted_iota(jnp.int32, (heads, nb), 1)
    gate = jnp.zeros((heads, nb), F32)
    for b in range(nb):
        zb = []
        g = jnp.zeros((heads, 1), F32)
        for j in range(ppb):
            s = s_scr[ppb * b + j]
            g = g + jnp.sum(s, axis=-1, keepdims=True)
            zb.append(s + bdec_ref[ppb * b + j])
        gate = jnp.where(blk_lane == b, g, gate)
        m_blk.append(functools.reduce(jnp.maximum, [jnp.max(t, axis=-1, keepdims=True) for t in zb]))
        z += zb
    keep = _topk_select(gate, lambda m_: 1, 1)

    l_new = jnp.sum(q_ref[0] * kn_ref[0], axis=-1, keepdims=True) + rb0_ref[...]
    m_all = l_new
    for b in range(nb):
        m_all = jnp.maximum(m_all, jnp.where(keep[:, b:b + 1] > 0.0, m_blk[b], NEG))
    w_new = jnp.exp(l_new - m_all)
    den = w_new
    for b in range(nb):
        kept = keep[:, b:b + 1] > 0.0
        for j in range(ppb):
            p = jnp.exp(jnp.where(kept, z[ppb * b + j] - m_all, NEG))
            den = den + jnp.sum(p, axis=-1, keepdims=True)
            p_scr[ppb * b + j] = p

    def v_step(h, carry):
        acc = cv[0][0, h] * p_scr[0, pl.ds(h, 1), :]
        for pg in range(1, npg):
            acc = acc + cv[pg][0, h] * p_scr[pg, pl.ds(h, 1), :]
        t_scr[h] = jnp.broadcast_to(jnp.sum(acc, axis=-1, keepdims=True), (dh, PAGE_SIZE))
        return carry

    lax.fori_loop(0, heads, v_step, 0)

    head_lane = lax.broadcasted_iota(jnp.int32, (dh, heads), 1)
    tot = jnp.zeros((dh, heads), F32)
    for h in range(heads):
        tot = jnp.where(head_lane == h, t_scr[h][:, :heads], tot)
    o_ref[0] = (tot + vnt_ref[0] * _col_to_row(w_new)) / _col_to_row(den)


def _moba_decode(page_table, q, k_new, v_new, rb0, cache_k, cache_v, bias_dec):
    n_dec, n_pages = page_table.shape
    heads, dh = MOBA_HEADS, MOBA_DH
    rows = lambda t: t.reshape(n_dec, heads, dh)
    cols = lambda t: t.reshape(n_dec, heads, dh).transpose(0, 2, 1)
    row_spec = pl.BlockSpec((1, heads, dh), lambda b, pt: (b, 0, 0))
    col_spec = pl.BlockSpec((1, dh, heads), lambda b, pt: (b, 0, 0))
    page_specs = [pl.BlockSpec((1, heads, dh, PAGE_SIZE), lambda b, pt, j=j: (pt[b, j], 0, 0, 0))
                  for j in range(n_pages)]
    grid_spec = pltpu.PrefetchScalarGridSpec(
        num_scalar_prefetch=1,
        grid=(n_dec,),
        in_specs=[row_spec, col_spec, row_spec, col_spec,
                  pl.BlockSpec((heads, 1), lambda b, pt: (0, 0)),
                  pl.BlockSpec(bias_dec.shape, lambda b, pt: (0, 0, 0))] + page_specs + page_specs,
        out_specs=col_spec,
        scratch_shapes=[pltpu.VMEM((heads, dh, PAGE_SIZE), F32),
                        pltpu.VMEM((n_pages, heads, PAGE_SIZE), F32),
                        pltpu.VMEM((n_pages, heads, PAGE_SIZE), F32),
                        pltpu.VMEM((heads, dh, PAGE_SIZE), F32)],
    )
    out = pl.pallas_call(
        functools.partial(_moba_decode_body, npg=n_pages),
        grid_spec=grid_spec,
        out_shape=jax.ShapeDtypeStruct((n_dec, dh, heads), F32),
        compiler_params=_cparams(("parallel",)),
        name="moba_decode",
    )(page_table, rows(q), cols(q), rows(k_new), cols(v_new), rb0, bias_dec,
      *([cache_k] * n_pages), *([cache_v] * n_pages))
    return out.transpose(0, 2, 1).reshape(n_dec, heads * dh)


def _prep_sgu(ln_g, ln_b, w_s, b_s):
    lng = ln_g.reshape(1, A_HALF)
    lnb = ln_b.reshape(1, A_HALF)
    causal = jnp.tril(jnp.ones((A_CHUNK, A_CHUNK), dtype=bool))
    wm_p = jnp.where(causal[None], w_s, 0.0).astype(BF16)
    sb_p = jnp.broadcast_to(b_s[:, :, None], (A_GROUPS, A_CHUNK, A_GROUP_DIM))
    eye = jnp.eye(A_CHUNK, dtype=F32)
    wm_s = (w_s[:, 0, 0][:, None, None] * eye[None]).astype(BF16)
    sb_s = jnp.broadcast_to(b_s[:, 0][:, None, None], (A_GROUPS, A_CHUNK, A_GROUP_DIM))
    return (lng, lnb), (wm_p, sb_p), (wm_s, sb_s)


def _prep_gla(w_in, w_gate, b_gate, gn, w_out):
    dq, dv = GLA_HEADS * GLA_DK, GLA_HEADS * GLA_DV
    pad = LANES - GLA_GATE_RANK
    wgl = jnp.pad(w_in[:, 2 * dq + 2 * dv:], ((0, 0), (0, pad))).astype(BF16)
    wgate = jnp.pad(w_gate, ((0, pad), (0, 0))).astype(BF16)
    return (w_in.astype(BF16), wgl, wgate, b_gate.reshape(1, dq), gn.reshape(1, GLA_DV),
            w_out.astype(BF16))


def kernel(x_prompt, x_sample, cache_k, cache_v, page_table, state_gla, norm_mix, norm_ffn, norm_final,
           w_in_a, ln_a_g, ln_a_b, w_s_a, b_s_a, w_out_a, w_in_b, w_gate_b, b_gate_b, gn_b, w_out_b,
           w_in_c, w_out_c, rel_bias, w_up, w_down):
    n_pr, l_pr, _ = x_prompt.shape
    n_dec, l_dec, _ = x_sample.shape
    assert l_dec == 1 and l_pr % MOBA_BLOCK == 0 and l_pr % GLA_C == 0
    depth = norm_mix.shape[0]
    past_len = page_table.shape[1] * PAGE_SIZE
    assert past_len % MOBA_BLOCK == 0
    tm_p, tm_s = 512, n_dec

    xp = x_prompt.reshape(n_pr * l_pr, D_MODEL)
    xs = x_sample.reshape(n_dec, D_MODEL)
    gfin = norm_final.reshape(1, D_MODEL)
    w_up_b, w_down_b = w_up.astype(BF16), w_down.astype(BF16)
    w_in_a_b, w_out_a_b = w_in_a.astype(BF16), w_out_a.astype(BF16)
    k_p, v_p, k_s, v_s, gla_p, gla_s, sgu_s = [], [], [], [], [], [], []

    for i in range(depth):
        kind, j = i % 3, i // 3
        gm = norm_mix[i].reshape(1, D_MODEL)
        gf = norm_ffn[i].reshape(1, D_MODEL)
        wup, wdn = (w_up_b, i), (w_down_b, i)
        last = gfin if i == depth - 1 else None
        proj_p = proj_s = None
        if kind == 0:
            (lng, lnb), mode_p, mode_s = _prep_sgu(ln_a_g[j], ln_a_b[j], w_s_a[j], b_s_a[j])
            win, wout = (w_in_a_b, j), (w_out_a_b, j)
            (xp,) = _sgu(xp, gm, win, lng, lnb, *mode_p, wout, tm_p, False)
            xs, v_rows = _sgu(xs, gm, win, lng, lnb, *mode_s, wout, tm_s, True)
            sgu_s.append(v_rows.reshape(n_dec, l_dec, A_HALF))
        elif kind == 1:
            wts = _prep_gla(w_in_b[j], w_gate_b[j], b_gate_b[j], gn_b[j], w_out_b[j])
            xp, sp = _gla_prompt(xp, gm, *wts, n_pr, l_pr)
            xs, ss = _gla_sample(xs, state_gla[j], gm, *wts)
            gla_p.append(sp)
            gla_s.append(ss)
        else:
            w_in = w_in_c[j].astype(BF16)
            bias_d, bias_1, bias_dec = _bias_tables(rel_bias, past_len)
            qp, kp, vp, kpt, vpt = _moba_qkv(xp, gm, w_in, tm_p, seq_len=l_pr)
            op = _moba_prompt(rel_bias, qp, kp, vp, bias_d, bias_1, n_pr, l_pr)
            qs, kn, vn = _moba_qkv(xs, gm, w_in, tm_s)
            os_ = _moba_decode(page_table, qs, kn, vn, rel_bias[0].reshape(MOBA_HEADS, 1),
                               cache_k[j].transpose(0, 2, 3, 1), cache_v[j].transpose(0, 2, 3, 1),
                               bias_dec.reshape(MOBA_HEADS, -1, PAGE_SIZE).transpose(1, 0, 2))
            wo_c = w_out_c[j].astype(BF16)
            proj_p, proj_s = (op, wo_c), (os_, wo_c)
            heads_last = lambda t: t.reshape(n_pr, MOBA_HEADS, MOBA_DH, l_pr).transpose(0, 3, 1, 2)
            k_p.append(heads_last(kpt))
            v_p.append(heads_last(vpt))
            k_s.append(kn.reshape(n_dec, l_dec, MOBA_HEADS, MOBA_DH))
            v_s.append(vn.reshape(n_dec, l_dec, MOBA_HEADS, MOBA_DH))
        if proj_p is None:
            xp = _ffn(xp, gf, wup, wdn, tm_p, gfinal=last)
            xs = _ffn(xs, gf, wup, wdn, tm_s, gfinal=last)
        else:
            xp = _ffn(xp, gf, wup, wdn, tm_p, a=proj_p[0], wo=proj_p[1], gfinal=last)
            xs = _ffn(xs, gf, wup, wdn, tm_s, a=proj_s[0], wo=proj_s[1], gfinal=last)

    return (xp.reshape(n_pr, l_pr, D_MODEL), xs.reshape(n_dec, l_dec, D_MODEL),
            jnp.stack(k_p), jnp.stack(v_p), jnp.stack(k_s), jnp.stack(v_s),
            jnp.stack(gla_p), jnp.stack(gla_s), jnp.stack(sgu_s))
```

```python
import functools
import math

import numpy as np
import jax
import jax.numpy as jnp
from jax import lax
from jax.experimental import pallas as pl
from jax.experimental.pallas import tpu as pltpu

F32 = jnp.float32
BF16 = jnp.bfloat16

D_MODEL = 1024
EPS = 1e-6
NEG = -1e30

LANES = 128
SUBLANES = 8
MXU_DIM = 256
VMEM_LIMIT = 56 * 1024 * 1024

A_CHUNK = 128
A_HALF = 3 * D_MODEL
A_GROUPS = 8
A_GROUP_DIM = A_HALF // A_GROUPS
A_PANEL = 2 * A_GROUP_DIM
A_NPANEL = A_HALF // A_PANEL

GLA_HEADS = 4
GLA_DK = 128
GLA_DV = 256
GLA_GATE_RANK = 16
GLA_GATE_NORM = 16.0
GLA_C = 256

MOBA_HEADS = 16
MOBA_DH = 64
MOBA_BLOCK = 256
MOBA_TOPK = 3
MOBA_HG = MXU_DIM // MOBA_DH
MOBA_ONES_ROWS = 2 * SUBLANES
LOG2E = math.log2(math.e)
PAGE_SIZE = 128
REL_BUCKETS = 32
REL_MAX_DIST = 128

D_FF = 2816
FF_CHUNK = MXU_DIM
FF_NCHUNK = D_FF // FF_CHUNK

NT_DIMS = (((1,), (1,)), ((), ()))
TN_DIMS = (((0,), (0,)), ((), ()))


def _cparams(sem):
    return pltpu.CompilerParams(dimension_semantics=sem, vmem_limit_bytes=VMEM_LIMIT)


def _resident(w):
    if isinstance(w, tuple):
        arr, layer = w
        nd = arr.ndim - 1
        return pl.BlockSpec((None,) + arr.shape[1:], lambda *_: (layer,) + (0,) * nd,
                            pipeline_mode=pl.Buffered(1))
    nd = w.ndim
    return pl.BlockSpec(w.shape, lambda *_: (0,) * nd, pipeline_mode=pl.Buffered(1))


def _operand(w):
    return w[0] if isinstance(w, tuple) else w


def _rms(x, g):
    return x * lax.rsqrt(jnp.mean(x * x, axis=-1, keepdims=True) + EPS) * g


def _gelu(x):
    return 0.5 * x * (1.0 + lax.erf(x * (1.0 / math.sqrt(2.0))))


def _silu(x):
    return x * jax.nn.sigmoid(x)


def _dot(a, b):
    return jnp.dot(a, b, preferred_element_type=F32)


def _ffn_body(*refs, has_proj, final):
    refs = list(refs)
    x_ref = refs.pop(0)
    a_ref = wo_ref = gf_ref = None
    if has_proj:
        a_ref = refs.pop(0)
        wo_ref = refs.pop(0)
    g_ref = refs.pop(0)
    wup_ref = refs.pop(0)
    wdn_ref = refs.pop(0)
    if final:
        gf_ref = refs.pop(0)
    o_ref = refs.pop(0)

    x = x_ref[...]
    if has_proj:
        x = x + _dot(a_ref[...].astype(BF16), wo_ref[...])
    h = _rms(x, g_ref[...]).astype(BF16)
    o_ref[...] = x

    for c in range(FF_NCHUNK):
        cols = slice(c * FF_CHUNK, (c + 1) * FF_CHUNK)
        gate = _dot(h, wup_ref[:, cols])
        up = _dot(h, wup_ref[:, D_FF + c * FF_CHUNK:D_FF + (c + 1) * FF_CHUNK])
        act = (_silu(gate) * up).astype(BF16)
        o_ref[...] += _dot(act, wdn_ref[cols, :])
    if final:
        o_ref[...] = _rms(o_ref[...], gf_ref[...])


def _ffn(x, g, wup, wdn, tm, a=None, wo=None, gfinal=None):
    rows = x.shape[0]
    row_spec = pl.BlockSpec((tm, D_MODEL), lambda i: (i, 0))
    args = [x]
    specs = [row_spec]
    if a is not None:
        args += [a, wo]
        specs += [row_spec, _resident(wo)]
    args += [g, _operand(wup), _operand(wdn)]
    specs += [_resident(g), _resident(wup), _resident(wdn)]
    if gfinal is not None:
        args.append(gfinal)
        specs.append(_resident(gfinal))
    return pl.pallas_call(
        functools.partial(_ffn_body, has_proj=a is not None, final=gfinal is not None),
        grid=(rows // tm,),
        in_specs=specs,
        out_specs=row_spec,
        out_shape=jax.ShapeDtypeStruct((rows, D_MODEL), F32),
        compiler_params=_cparams(("parallel",)),
        name="ffn",
    )(*args)


def _sgu_body(*refs, tm, emit_v):
    (x_ref, g_ref, win_ref, lng_ref, lnb_ref, wm_ref, sb_ref, wout_ref) = refs[:8]
    if emit_v:
        o_ref, vout_ref, vbuf, ubuf, sbuf = refs[8:]
    else:
        o_ref, vbuf, ubuf, sbuf = refs[8:]
        vout_ref = None
    panel = lambda p, base=0: slice(base + p * A_PANEL, base + (p + 1) * A_PANEL)

    x = x_ref[...]
    h = _rms(x, g_ref[...]).astype(BF16)

    rsum = jnp.zeros((tm, 1), F32)
    for p in range(A_NPANEL):
        v = _gelu(_dot(h, win_ref[:, panel(p, A_HALF)]))
        vbuf[p] = v
        rsum = rsum + jnp.sum(v, axis=-1, keepdims=True)
    for p in range(A_NPANEL):
        ubuf[p] = _gelu(_dot(h, win_ref[:, panel(p)]))
    mean = rsum * (1.0 / A_HALF)
    ssq = jnp.zeros((tm, 1), F32)
    for p in range(A_NPANEL):
        d = vbuf[p] - mean
        ssq = ssq + jnp.sum(d * d, axis=-1, keepdims=True)
    rstd = lax.rsqrt(ssq * (1.0 / A_HALF) + EPS)

    acc = x
    for p in range(A_NPANEL):
        vn = (vbuf[p] - mean) * rstd * lng_ref[:, panel(p)] + lnb_ref[:, panel(p)]
        if emit_v:
            vout_ref[:, p * A_PANEL:(p + 1) * A_PANEL] = vn
        vnb = vn.astype(BF16)
        for c in range(tm // A_CHUNK):
            for gg in range(2):
                grp = 2 * p + gg
                rs = slice(c * A_CHUNK, (c + 1) * A_CHUNK)
                cs = slice(gg * A_GROUP_DIM, (gg + 1) * A_GROUP_DIM)
                sbuf[rs, cs] = _dot(wm_ref[grp], vnb[rs, cs]) + sb_ref[grp]
        acc = acc + _dot((ubuf[p] * sbuf[...]).astype(BF16), wout_ref[panel(p), :])
    o_ref[...] = acc


def _sgu(x, g, win, lng, lnb, wm, sb, wout, tm, emit_v):
    rows = x.shape[0]
    row_spec = pl.BlockSpec((tm, D_MODEL), lambda i: (i, 0))
    weights = [g, win, lng, lnb, wm, sb, wout]
    out_shape = [jax.ShapeDtypeStruct((rows, D_MODEL), F32)]
    out_specs = [row_spec]
    if emit_v:
        out_shape.append(jax.ShapeDtypeStruct((rows, A_HALF), F32))
        out_specs.append(pl.BlockSpec((tm, A_HALF), lambda i: (i, 0)))
    return pl.pallas_call(
        functools.partial(_sgu_body, tm=tm, emit_v=emit_v),
        grid=(rows // tm,),
        in_specs=[row_spec] + [_resident(w) for w in weights],
        out_specs=out_specs,
        out_shape=out_shape,
        scratch_shapes=[pltpu.VMEM((A_NPANEL, tm, A_PANEL), F32),
                        pltpu.VMEM((A_NPANEL, tm, A_PANEL), F32),
                        pltpu.VMEM((tm, A_PANEL), F32)],
        compiler_params=_cparams(("parallel",)),
        name="sgu",
    )(x, *[_operand(w) for w in weights])


def _log_sigmoid(z):
    return jnp.minimum(z, 0.0) - jnp.log1p(jnp.exp(-jnp.abs(z)))


def _gla_intra(qh, kh, cum, same_block):
    c = qh.shape[0]
    r = lax.broadcasted_iota(jnp.int32, (c, 1), 0)
    a = jnp.zeros((c, c), F32)
    m = c
    while m >= 2:
        half = m // 2
        pos = r & (m - 1)
        upper = pos >= half
        if m >= 2 * SUBLANES:
            pieces = []
            for b in range(c // m):
                row = b * m + half - 1
                pieces.append(jnp.broadcast_to(cum[row:row + 1, :], (m, cum.shape[1])))
            ref_q = pieces[0] if len(pieces) == 1 else jnp.concatenate(pieces, axis=0)
            ref_k = ref_q
        else:
            ref_q = cum
            for d in range(1, half + 1):
                ref_q = jnp.where(pos - (half - 1) == d, pltpu.roll(cum, d, 0), ref_q)
            ref_k = cum
            for o in range(1, half):
                ref_k = jnp.where((half - 1) - pos == o, pltpu.roll(cum, c - o, 0), ref_k)
        decay = jnp.exp(jnp.where(upper, cum - ref_q, ref_k - cum))
        x = jnp.where(upper, qh, kh) * decay
        qm = jnp.where(upper, x, 0.0).astype(BF16)
        km = jnp.where(upper, 0.0, x).astype(BF16)
        am = lax.dot_general(qm, km, NT_DIMS, preferred_element_type=F32)
        if m < c:
            am = jnp.where(same_block[m], am, 0.0)
        a = a + am
        m = half
    return a


def _gla_project(h, win_ref, wgl_ref, wgate_ref, bgate_ref):
    dq, dv = GLA_HEADS * GLA_DK, GLA_HEADS * GLA_DV
    q = _dot(h, win_ref[:, :dq]) * (GLA_DK ** -0.5)
    k = _dot(h, win_ref[:, dq:2 * dq])
    v = _dot(h, win_ref[:, 2 * dq:2 * dq + dv])
    g = _dot(h, win_ref[:, 2 * dq + dv:2 * dq + 2 * dv])
    gl = _dot(h, wgl_ref[...])
    z = _dot(gl.astype(BF16), wgate_ref[...]) + bgate_ref[...]
    return q, k, v, g, z


def _gla_prompt_body(x_ref, gm_ref, win_ref, wgl_ref, wgate_ref, bgate_ref,
                     gn_ref, wo_ref, o_ref, st_ref, s_scr):
    ci = pl.program_id(1)
    c = GLA_C

    @pl.when(ci == 0)
    def _():
        s_scr[...] = jnp.zeros_like(s_scr)

    x = x_ref[...]
    h = _rms(x, gm_ref[...]).astype(BF16)
    q, k, v, g, z = _gla_project(h, win_ref, wgl_ref, wgate_ref, bgate_ref)
    la = _log_sigmoid(z) * (1.0 / GLA_GATE_NORM)

    rowi = lax.broadcasted_iota(jnp.int32, (c, c), 0)
    coli = lax.broadcasted_iota(jnp.int32, (c, c), 1)
    tril = (coli <= rowi).astype(F32)
    cum = jnp.dot(tril, la, precision=lax.Precision.HIGHEST, preferred_element_type=F32)

    same_block = {}
    m = c // 2
    while m >= 2:
        sh = int(math.log2(m))
        same_block[m] = (rowi >> sh) == (coli >> sh)
        m //= 2

    outs = []
    for hd in range(GLA_HEADS):
        ks = slice(hd * GLA_DK, (hd + 1) * GLA_DK)
        vs = slice(hd * GLA_DV, (hd + 1) * GLA_DV)
        qh, kh, cumh, vh = q[:, ks], k[:, ks], cum[:, ks], v[:, vs]
        vhb = vh.astype(BF16)
        total = cumh[c - 1:c, :]
        st = s_scr[hd]
        o = lax.dot_general((qh * jnp.exp(cumh)).astype(BF16), st.astype(BF16), NT_DIMS,
                            preferred_element_type=F32)
        a = _gla_intra(qh, kh, cumh, same_block)
        o = o + _dot(a.astype(BF16), vhb)
        o = o + jnp.sum(qh * kh, axis=-1, keepdims=True) * vh
        kdec = (kh * jnp.exp(total - cumh)).astype(BF16)
        st_new = st * jnp.exp(total) + lax.dot_general(vhb, kdec, TN_DIMS,
                                                       preferred_element_type=F32)
        s_scr[hd] = st_new
        o = o * lax.rsqrt(jnp.mean(o * o, axis=-1, keepdims=True) + EPS) * gn_ref[...]
        outs.append(o)
    y = jnp.concatenate(outs, axis=1) * _silu(g)
    o_ref[...] = x + _dot(y.astype(BF16), wo_ref[...])

    @pl.when(ci == pl.num_programs(1) - 1)
    def _():
        for hd in range(GLA_HEADS):
            st_ref[0, hd] = s_scr[hd].T


def _gla_prompt(x, gm, win, wgl, wgate, bgate, gn, wo, n_seq, seq_len):
    n_chunk = seq_len // GLA_C
    weights = [gm, win, wgl, wgate, bgate, gn, wo]
    row_spec = pl.BlockSpec((GLA_C, D_MODEL), lambda b, c: (b * n_chunk + c, 0))
    return pl.pallas_call(
        _gla_prompt_body,
        grid=(n_seq, n_chunk),
        in_specs=[row_spec] + [_resident(w) for w in weights],
        out_specs=[row_spec,
                   pl.BlockSpec((1, GLA_HEADS, GLA_DK, GLA_DV), lambda b, c: (b, 0, 0, 0))],
        out_shape=[jax.ShapeDtypeStruct(x.shape, F32),
                   jax.ShapeDtypeStruct((n_seq, GLA_HEADS, GLA_DK, GLA_DV), F32)],
        scratch_shapes=[pltpu.VMEM((GLA_HEADS, GLA_DV, GLA_DK), F32)],
        compiler_params=_cparams(("parallel", "arbitrary")),
        name="gla_prompt",
    )(x, *weights)


def _gla_sample_proj_body(x_ref, gm_ref, win_ref, wgl_ref, wgate_ref, bgate_ref,
                          q_ref, k_ref, a_ref, v_ref, g_ref):
    h = _rms(x_ref[...], gm_ref[...]).astype(BF16)
    q_ref[...], k_ref[...], v_ref[...], g_ref[...], z = _gla_project(
        h, win_ref, wgl_ref, wgate_ref, bgate_ref)
    a_ref[...] = jnp.exp(_log_sigmoid(z) * (1.0 / GLA_GATE_NORM))


GLA_SEQ_PER_STEP = SUBLANES


def _gla_sample_rec_body(s_ref, qc_ref, kc_ref, ac_ref, v_ref, so_ref, o_ref):
    for j in range(GLA_SEQ_PER_STEP):
        for hd in range(GLA_HEADS):
            ks = slice(hd * GLA_DK, (hd + 1) * GLA_DK)
            vs = slice(hd * GLA_DV, (hd + 1) * GLA_DV)
            acol = ac_ref[0, ks, j:j + 1]
            kcol = kc_ref[0, ks, j:j + 1]
            qcol = qc_ref[0, ks, j:j + 1]
            vrow = v_ref[j:j + 1, vs]
            s_new = s_ref[j, hd] * acol + kcol * vrow
            so_ref[j, hd] = s_new
            o_ref[j:j + 1, vs] = jnp.sum(s_new * qcol, axis=0, keepdims=True)


def _gla_sample_out_body(x_ref, o_ref, g_ref, gn_ref, wo_ref, y_ref):
    outs = []
    for hd in range(GLA_HEADS):
        o = o_ref[:, hd * GLA_DV:(hd + 1) * GLA_DV]
        outs.append(o * lax.rsqrt(jnp.mean(o * o, axis=-1, keepdims=True) + EPS) * gn_ref[...])
    y = jnp.concatenate(outs, axis=1) * _silu(g_ref[...])
    y_ref[...] = x_ref[...] + _dot(y.astype(BF16), wo_ref[...])


def _gla_sample(x, state, gm, win, wgl, wgate, bgate, gn, wo):
    n = x.shape[0]
    dq = GLA_HEADS * GLA_DK
    weights = [gm, win, wgl, wgate, bgate]
    q, k, a, v, g = pl.pallas_call(
        _gla_sample_proj_body,
        out_shape=[jax.ShapeDtypeStruct((n, dq), F32)] * 3
        + [jax.ShapeDtypeStruct((n, D_MODEL), F32)] * 2,
        compiler_params=pltpu.CompilerParams(vmem_limit_bytes=VMEM_LIMIT),
        name="gla_sample_proj",
    )(x, *weights)

    nstep = n // GLA_SEQ_PER_STEP

    def cols(t):
        return t.reshape(nstep, GLA_SEQ_PER_STEP, dq).transpose(0, 2, 1)

    col_spec = pl.BlockSpec((1, dq, GLA_SEQ_PER_STEP), lambda i: (i, 0, 0))
    st_spec = pl.BlockSpec((GLA_SEQ_PER_STEP, GLA_HEADS, GLA_DK, GLA_DV), lambda i: (i, 0, 0, 0))
    row_spec = pl.BlockSpec((GLA_SEQ_PER_STEP, D_MODEL), lambda i: (i, 0))
    s_new, o = pl.pallas_call(
        _gla_sample_rec_body,
        grid=(nstep,),
        in_specs=[st_spec, col_spec, col_spec, col_spec, row_spec],
        out_specs=[st_spec, row_spec],
        out_shape=[jax.ShapeDtypeStruct(state.shape, F32), jax.ShapeDtypeStruct((n, D_MODEL), F32)],
        compiler_params=_cparams(("parallel",)),
        name="gla_sample_rec",
    )(state, cols(q), cols(k), cols(a), v)

    y = pl.pallas_call(
        _gla_sample_out_body,
        out_shape=jax.ShapeDtypeStruct((n, D_MODEL), F32),
        compiler_params=pltpu.CompilerParams(vmem_limit_bytes=VMEM_LIMIT),
        name="gla_sample_out",
    )(x, o, g, gn, wo)
    return y, s_new


def _t5_bucket_np(dist):
    max_exact = REL_BUCKETS // 2
    n = np.maximum(dist, 0)
    nf = np.maximum(n, max_exact).astype(np.float32)
    large = max_exact + (np.log(nf / np.float32(max_exact)) / np.float32(math.log(REL_MAX_DIST / max_exact))
                         * np.float32(REL_BUCKETS - max_exact)).astype(np.int32)
    return np.where(n < max_exact, n, np.minimum(large, REL_BUCKETS - 1)).astype(np.int32)


def _bias_tables_body(rb_ref, bd_ref, b1_ref, bdec_ref, od_ref, o1_ref, odec_ref):
    h = pl.program_id(0)
    td, t1, tdec = bd_ref[...], b1_ref[...], bdec_ref[...]
    ad = jnp.zeros(td.shape, F32)
    a1 = jnp.zeros(t1.shape, F32)
    adec = jnp.zeros(tdec.shape, F32)
    for b in range(REL_BUCKETS):
        val = rb_ref[b, h]
        ad = jnp.where(td == b, val, ad)
        a1 = jnp.where(t1 == b, val, a1)
        adec = jnp.where(tdec == b, val, adec)
    od_ref[0] = ad * LOG2E
    o1_ref[0] = a1 * LOG2E
    odec_ref[0] = adec


def _bias_tables(rel_bias, past_len):
    kc = np.arange(MOBA_BLOCK)[:, None]
    qr = np.arange(MOBA_BLOCK)[None, :]
    bd = _t5_bucket_np(qr - kc)
    b1 = _t5_bucket_np(MOBA_BLOCK + qr - kc)
    nblk = past_len // MOBA_BLOCK
    kpos = np.arange(nblk)[:, None] * MOBA_BLOCK + np.arange(MOBA_BLOCK)[None, :]
    bdec = _t5_bucket_np(past_len - kpos)
    full = lambda a: pl.BlockSpec(a.shape, lambda h: (0,) * a.ndim)
    return pl.pallas_call(
        _bias_tables_body,
        grid=(MOBA_HEADS,),
        in_specs=[pl.BlockSpec(memory_space=pltpu.SMEM), full(bd), full(b1), full(bdec)],
        out_specs=[pl.BlockSpec((1,) + bd.shape, lambda h: (h, 0, 0)),
                   pl.BlockSpec((1,) + b1.shape, lambda h: (h, 0, 0)),
                   pl.BlockSpec((1,) + bdec.shape, lambda h: (h, 0, 0))],
        out_shape=[jax.ShapeDtypeStruct((MOBA_HEADS,) + bd.shape, F32),
                   jax.ShapeDtypeStruct((MOBA_HEADS,) + b1.shape, F32),
                   jax.ShapeDtypeStruct((MOBA_HEADS,) + bdec.shape, F32)],
        compiler_params=_cparams(("parallel",)),
        name="moba_bias_tables",
    )(rel_bias, jnp.asarray(bd), jnp.asarray(b1), jnp.asarray(bdec))


def _moba_qkv_body(x_ref, g_ref, w_ref, q_ref, k_ref, v_ref, *t_refs):
    h = _rms(x_ref[...], g_ref[...]).astype(BF16)
    qkv = _dot(h, w_ref[...])
    q_ref[...] = qkv[:, :D_MODEL] * (MOBA_DH ** -0.5)
    k = qkv[:, D_MODEL:2 * D_MODEL]
    v = qkv[:, 2 * D_MODEL:]
    k_ref[...] = k
    v_ref[...] = v
    if t_refs:
        t_refs[0][0] = k.T
        t_refs[1][0] = v.T


def _moba_qkv(x, g, w, tm, seq_len=None):
    rows = x.shape[0]
    row_spec = pl.BlockSpec((tm, D_MODEL), lambda i: (i, 0))
    out_specs = [row_spec] * 3
    out_shape = [jax.ShapeDtypeStruct((rows, D_MODEL), F32)] * 3
    if seq_len is not None:
        per_seq = seq_len // tm
        t_spec = pl.BlockSpec((1, D_MODEL, tm), lambda i: (i // per_seq, 0, i % per_seq))
        out_specs += [t_spec] * 2
        out_shape += [jax.ShapeDtypeStruct((rows // seq_len, D_MODEL, seq_len), F32)] * 2
    return pl.pallas_call(
        _moba_qkv_body,
        grid=(rows // tm,),
        in_specs=[row_spec, _resident(g), _resident(w)],
        out_specs=out_specs,
        out_shape=out_shape,
        compiler_params=_cparams(("parallel",)),
        name="moba_qkv",
    )(x, g, w)


def _topk_select(gate, valid, axis):
    n = gate.shape[axis]
    idx = lax.broadcasted_iota(jnp.int32, gate.shape, axis)
    cnt = jnp.zeros(gate.shape, jnp.int32)
    for m in range(n):
        gm = lax.slice_in_dim(gate, m, m + 1, axis=axis)
        beats = (gm > gate) | ((gm == gate) & (m < idx))
        cnt = cnt + jnp.where(beats, valid(m), 0)
    return jnp.where(cnt < MOBA_TOPK, 1.0, 0.0)


def _moba_prompt_body(rb_ref, q_ref, k_ref, v_ref, bd_ref, b1_ref, o_ref,
                      kb_scr, vt_scr, km_scr):
    hg = pl.program_id(1)
    qi = pl.program_id(2)
    nblk = kb_scr.shape[0]
    blk = MOBA_BLOCK
    dh = MOBA_DH
    width = MOBA_HG * dh

    lane = lax.broadcasted_iota(jnp.int32, (1, width), 1)
    in_head = [(lane >= hh * dh) & (lane < (hh + 1) * dh) for hh in range(MOBA_HG)]

    @pl.when(qi == 0)
    def _():
        ones = jnp.ones((MOBA_ONES_ROWS, blk), BF16)
        means = []
        for n in range(nblk):
            kt = k_ref[n * blk:(n + 1) * blk, :]
            kb_scr[n] = kt.astype(BF16)
            means.append(jnp.mean(kt, axis=0, keepdims=True))
            vt = v_ref[n * blk:(n + 1) * blk, :].T.astype(BF16)
            for hh in range(MOBA_HG):
                vt_scr[n, hh, :dh, :] = vt[hh * dh:(hh + 1) * dh, :]
                vt_scr[n, hh, dh:, :] = ones
        km = jnp.concatenate(means, axis=0)
        per_head = [jnp.where(in_head[hh], km, 0.0) for hh in range(MOBA_HG)]
        pad = jnp.zeros((LANES - MOBA_HG * nblk, width), F32)
        km_scr[...] = jnp.concatenate(per_head + [pad], axis=0).T

    q2 = q_ref[...]
    ki_idx = lax.broadcasted_iota(jnp.int32, (blk, blk), 0)
    qi_idx = lax.broadcasted_iota(jnp.int32, (blk, blk), 1)
    blk_idx = lax.broadcasted_iota(jnp.int32, (nblk, blk), 0)

    hs = range(MOBA_HG)
    qhb, far_bias, neg = [], [], []
    gates = jnp.dot(q2, km_scr[...], precision=lax.Precision.HIGHEST, preferred_element_type=F32).T
    for hh in hs:
        qhb.append((jnp.where(in_head[hh], q2, 0.0) * LOG2E).astype(BF16))
        far_bias.append(rb_ref[REL_BUCKETS - 1, MOBA_HG * hg + hh] * LOG2E)
        gate = gates[hh * nblk:(hh + 1) * nblk, :]
        keep = _topk_select(gate, lambda m: (m < qi).astype(jnp.int32), 0)
        neg.append(jnp.where((keep > 0.0) & (blk_idx < qi), 0.0, NEG))

    def scores(k0, n):
        kt = kb_scr[k0:k0 + n].reshape(n * blk, width)
        return [lax.dot_general(kt, qhb[hh], NT_DIMS, preferred_element_type=F32) for hh in hs]

    def values(ki, p):
        return [_dot(vt_scr[ki, hh], p[hh]) for hh in hs]

    def weights(x):
        return jnp.exp2(x.astype(BF16))

    def step(c):
        units = [(c - 1, 2)] if c >= 1 else [(0, 1)]
        t = 0
        while t < c - 1:
            n = min(2, c - 1 - t)
            units.append((t, n))
            t += n
        m = [None] * MOBA_HG
        acc = [None] * MOBA_HG
        s_next = scores(*units[0])
        for ui, (k0, n) in enumerate(units):
            s = s_next
            if ui + 1 < len(units):
                s_next = scores(*units[ui + 1])
            p = [[] for _ in range(n)]
            alpha = [None] * MOBA_HG
            for hh in hs:
                sh, rows, tile_max = [], [], []
                for t in range(n):
                    kb = k0 + t
                    x = s[hh][t * blk:(t + 1) * blk, :]
                    if kb == c:
                        x, r = jnp.where(ki_idx <= qi_idx, x + bd_ref[hh], NEG), None
                    elif kb == c - 1:
                        x, r = x + b1_ref[hh], neg[hh][kb:kb + 1, :]
                    else:
                        r = neg[hh][kb:kb + 1, :] + far_bias[hh]
                    mx = jnp.max(x, axis=0, keepdims=True)
                    sh.append(x)
                    rows.append(r)
                    tile_max.append(mx if r is None else mx + r)
                m_new = functools.reduce(jnp.maximum, tile_max if m[hh] is None else [m[hh]] + tile_max)
                for t in range(n):
                    p[t].append(weights(sh[t] - (m_new if rows[t] is None else m_new - rows[t])))
                if m[hh] is not None:
                    alpha[hh] = jnp.exp2(m[hh] - m_new)
                m[hh] = m_new
            pv = [values(k0 + t, p[t]) for t in range(n)]
            for hh in hs:
                tot = functools.reduce(lambda x, y: x + y, [pv[t][hh] for t in range(n)])
                acc[hh] = tot if acc[hh] is None else alpha[hh] * acc[hh] + tot
        ot = jnp.concatenate([acc[hh][:dh, :] / acc[hh][dh:dh + 1, :] for hh in hs], axis=0)
        o_ref[...] = ot.T

    for c in range(nblk):
        pl.when(qi == c)(functools.partial(step, c))


def _moba_prompt(rel_bias, q, k, v, bias_d, bias_1, n_seq, seq_len):
    nblk = seq_len // MOBA_BLOCK
    hgrp = MOBA_HG
    width = hgrp * MOBA_DH
    q_spec = pl.BlockSpec((MOBA_BLOCK, width), lambda b, hg, qi: (b * nblk + qi, hg))
    kv_spec = pl.BlockSpec((seq_len, width), lambda b, hg, qi: (b, hg))
    bias_spec = pl.BlockSpec((hgrp, MOBA_BLOCK, MOBA_BLOCK), lambda b, hg, qi: (hg, 0, 0))
    return pl.pallas_call(
        _moba_prompt_body,
        grid=(n_seq, MOBA_HEADS // hgrp, nblk),
        in_specs=[pl.BlockSpec(memory_space=pltpu.SMEM), q_spec, kv_spec, kv_spec, bias_spec, bias_spec],
        out_specs=q_spec,
        out_shape=jax.ShapeDtypeStruct(q.shape, F32),
        scratch_shapes=[
            pltpu.VMEM((nblk, MOBA_BLOCK, width), BF16),
            pltpu.VMEM((nblk, hgrp, MOBA_DH + MOBA_ONES_ROWS, MOBA_BLOCK), BF16),
            pltpu.VMEM((width, LANES), F32),
        ],
        compiler_params=_cparams(("parallel", "parallel", "arbitrary")),
        name="moba_prompt",
    )(rel_bias, q, k, v, bias_d, bias_1)


def _col_to_row(col):
    n = col.shape[0]
    r = lax.broadcasted_iota(jnp.int32, (n, n), 0)
    c = lax.broadcasted_iota(jnp.int32, (n, n), 1)
    return jnp.sum(jnp.where(r == c, jnp.broadcast_to(col, (n, n)), 0.0), axis=0, keepdims=True)


def _moba_decode_body(pt_ref, q_ref, qt_ref, kn_ref, vnt_ref, rb0_ref, bdec_ref, *refs, npg):
    del pt_ref
    ck, cv = refs[:npg], refs[npg:2 * npg]
    o_ref, qb_scr, s_scr, p_scr, t_scr = refs[2 * npg:]
    heads, dh = MOBA_HEADS, MOBA_DH
    ppb = MOBA_BLOCK // PAGE_SIZE
    nb = npg // ppb

    qt = qt_ref[0]
    for h in range(heads):
        qb_scr[h] = jnp.broadcast_to(qt[:, h:h + 1], (dh, PAGE_SIZE))

    def k_step(h, carry):
        qb = qb_scr[h]
        for pg in range(npg):
            s_scr[pg, pl.ds(h, 1), :] = jnp.sum(ck[pg][0, h] * qb, axis=0, keepdims=True)
        return carry

    lax.fori_loop(0, heads, k_step, 0)

    z, m_blk = [], []
    blk_lane = lax.broadcasted_iota(jnp.int32, (heads, nb), 1)
    gate = jnp.zeros((heads, nb), F32)
    for b in range(nb):
        zb = []
        g = jnp.zeros((heads, 1), F32)
        for j in range(ppb):
            s = s_scr[ppb * b + j]
            g = g + jnp.sum(s, axis=-1, keepdims=True)
            zb.append(s + bdec_ref[ppb * b + j])
        gate = jnp.where(blk_lane == b, g, gate)
        m_blk.append(functools.reduce(jnp.maximum, [jnp.max(t, axis=-1, keepdims=True) for t in zb]))
        z += zb
    keep = _topk_select(gate, lambda m_: 1, 1)

    l_new = jnp.sum(q_ref[0] * kn_ref[0], axis=-1, keepdims=True) + rb0_ref[...]
    m_all = l_new
    for b in range(nb):
        m_all = jnp.maximum(m_all, jnp.where(keep[:, b:b + 1] > 0.0, m_blk[b], NEG))
    w_new = jnp.exp(l_new - m_all)
    den = w_new
    for b in range(nb):
        kept = keep[:, b:b + 1] > 0.0
        for j in range(ppb):
            p = jnp.exp(jnp.where(kept, z[ppb * b + j] - m_all, NEG))
            den = den + jnp.sum(p, axis=-1, keepdims=True)
            p_scr[ppb * b + j] = p

    def v_step(h, carry):
        acc = cv[0][0, h] * p_scr[0, pl.ds(h, 1), :]
        for pg in range(1, npg):
            acc = acc + cv[pg][0, h] * p_scr[pg, pl.ds(h, 1), :]
        t_scr[h] = jnp.broadcast_to(jnp.sum(acc, axis=-1, keepdims=True), (dh, PAGE_SIZE))
        return carry

    lax.fori_loop(0, heads, v_step, 0)

    head_lane = lax.broadcasted_iota(jnp.int32, (dh, heads), 1)
    tot = jnp.zeros((dh, heads), F32)
    for h in range(heads):
        tot = jnp.where(head_lane == h, t_scr[h][:, :heads], tot)
    o_ref[0] = (tot + vnt_ref[0] * _col_to_row(w_new)) / _col_to_row(den)


def _moba_decode(page_table, q, k_new, v_new, rb0, cache_k, cache_v, bias_dec):
    n_dec, n_pages = page_table.shape
    heads, dh = MOBA_HEADS, MOBA_DH
    rows = lambda t: t.reshape(n_dec, heads, dh)
    cols = lambda t: t.reshape(n_dec, heads, dh).transpose(0, 2, 1)
    row_spec = pl.BlockSpec((1, heads, dh), lambda b, pt: (b, 0, 0))
    col_spec = pl.BlockSpec((1, dh, heads), lambda b, pt: (b, 0, 0))
    page_specs = [pl.BlockSpec((1, heads, dh, PAGE_SIZE), lambda b, pt, j=j: (pt[b, j], 0, 0, 0))
                  for j in range(n_pages)]
    grid_spec = pltpu.PrefetchScalarGridSpec(
        num_scalar_prefetch=1,
        grid=(n_dec,),
        in_specs=[row_spec, col_spec, row_spec, col_spec,
                  pl.BlockSpec((heads, 1), lambda b, pt: (0, 0)),
                  pl.BlockSpec(bias_dec.shape, lambda b, pt: (0, 0, 0))] + page_specs + page_specs,
        out_specs=col_spec,
        scratch_shapes=[pltpu.VMEM((heads, dh, PAGE_SIZE), F32),
                        pltpu.VMEM((n_pages, heads, PAGE_SIZE), F32),
                        pltpu.VMEM((n_pages, heads, PAGE_SIZE), F32),
                        pltpu.VMEM((heads, dh, PAGE_SIZE), F32)],
    )
    out = pl.pallas_call(
        functools.partial(_moba_decode_body, npg=n_pages),
        grid_spec=grid_spec,
        out_shape=jax.ShapeDtypeStruct((n_dec, dh, heads), F32),
        compiler_params=_cparams(("parallel",)),
        name="moba_decode",
    )(page_table, rows(q), cols(q), rows(k_new), cols(v_new), rb0, bias_dec,
      *([cache_k] * n_pages), *([cache_v] * n_pages))
    return out.transpose(0, 2, 1).reshape(n_dec, heads * dh)


def _prep_sgu(ln_g, ln_b, w_s, b_s):
    lng = ln_g.reshape(1, A_HALF)
    lnb = ln_b.reshape(1, A_HALF)
    causal = jnp.tril(jnp.ones((A_CHUNK, A_CHUNK), dtype=bool))
    wm_p = jnp.where(causal[None], w_s, 0.0).astype(BF16)
    sb_p = jnp.broadcast_to(b_s[:, :, None], (A_GROUPS, A_CHUNK, A_GROUP_DIM))
    eye = jnp.eye(A_CHUNK, dtype=F32)
    wm_s = (w_s[:, 0, 0][:, None, None] * eye[None]).astype(BF16)
    sb_s = jnp.broadcast_to(b_s[:, 0][:, None, None], (A_GROUPS, A_CHUNK, A_GROUP_DIM))
    return (lng, lnb), (wm_p, sb_p), (wm_s, sb_s)


def _prep_gla(w_in, w_gate, b_gate, gn, w_out):
    dq, dv = GLA_HEADS * GLA_DK, GLA_HEADS * GLA_DV
    pad = LANES - GLA_GATE_RANK
    wgl = jnp.pad(w_in[:, 2 * dq + 2 * dv:], ((0, 0), (0, pad))).astype(BF16)
    wgate = jnp.pad(w_gate, ((0, pad), (0, 0))).astype(BF16)
    return (w_in.astype(BF16), wgl, wgate, b_gate.reshape(1, dq), gn.reshape(1, GLA_DV),
            w_out.astype(BF16))


def kernel(x_prompt, x_sample, cache_k, cache_v, page_table, state_gla, norm_mix, norm_ffn, norm_final,
           w_in_a, ln_a_g, ln_a_b, w_s_a, b_s_a, w_out_a, w_in_b, w_gate_b, b_gate_b, gn_b, w_out_b,
           w_in_c, w_out_c, rel_bias, w_up, w_down):
    n_pr, l_pr, _ = x_prompt.shape
    n_dec, l_dec, _ = x_sample.shape
    assert l_dec == 1 and l_pr % MOBA_BLOCK == 0 and l_pr % GLA_C == 0
    depth = norm_mix.shape[0]
    past_len = page_table.shape[1] * PAGE_SIZE
    assert past_len % MOBA_BLOCK == 0
    tm_p, tm_s = 512, n_dec

    xp = x_prompt.reshape(n_pr * l_pr, D_MODEL)
    xs = x_sample.reshape(n_dec, D_MODEL)
    gfin = norm_final.reshape(1, D_MODEL)
    w_up_b, w_down_b = w_up.astype(BF16), w_down.astype(BF16)
    w_in_a_b, w_out_a_b = w_in_a.astype(BF16), w_out_a.astype(BF16)
    k_p, v_p, k_s, v_s, gla_p, gla_s, sgu_s = [], [], [], [], [], [], []

    for i in range(depth):
        kind, j = i % 3, i // 3
        gm = norm_mix[i].reshape(1, D_MODEL)
        gf = norm_ffn[i].reshape(1, D_MODEL)
        wup, wdn = (w_up_b, i), (w_down_b, i)
        last = gfin if i == depth - 1 else None
        proj_p = proj_s = None
        if kind == 0:
            (lng, lnb), mode_p, mode_s = _prep_sgu(ln_a_g[j], ln_a_b[j], w_s_a[j], b_s_a[j])
            win, wout = (w_in_a_b, j), (w_out_a_b, j)
            (xp,) = _sgu(xp, gm, win, lng, lnb, *mode_p, wout, tm_p, False)
            xs, v_rows = _sgu(xs, gm, win, lng, lnb, *mode_s, wout, tm_s, True)
            sgu_s.append(v_rows.reshape(n_dec, l_dec, A_HALF))
        elif kind == 1:
            wts = _prep_gla(w_in_b[j], w_gate_b[j], b_gate_b[j], gn_b[j], w_out_b[j])
            xp, sp = _gla_prompt(xp, gm, *wts, n_pr, l_pr)
            xs, ss = _gla_sample(xs, state_gla[j], gm, *wts)
            gla_p.append(sp)
            gla_s.append(ss)
        else:
            w_in = w_in_c[j].astype(BF16)
            bias_d, bias_1, bias_dec = _bias_tables(rel_bias, past_len)
            qp, kp, vp, kpt, vpt = _moba_qkv(xp, gm, w_in, tm_p, seq_len=l_pr)
            op = _moba_prompt(rel_bias, qp, kp, vp, bias_d, bias_1, n_pr, l_pr)
            qs, kn, vn = _moba_qkv(xs, gm, w_in, tm_s)
            os_ = _moba_decode(page_table, qs, kn, vn, rel_bias[0].reshape(MOBA_HEADS, 1),
                               cache_k[j].transpose(0, 2, 3, 1), cache_v[j].transpose(0, 2, 3, 1),
                               bias_dec.reshape(MOBA_HEADS, -1, PAGE_SIZE).transpose(1, 0, 2))
            wo_c = w_out_c[j].astype(BF16)
            proj_p, proj_s = (op, wo_c), (os_, wo_c)
            heads_last = lambda t: t.reshape(n_pr, MOBA_HEADS, MOBA_DH, l_pr).transpose(0, 3, 1, 2)
            k_p.append(heads_last(kpt))
            v_p.append(heads_last(vpt))
            k_s.append(kn.reshape(n_dec, l_dec, MOBA_HEADS, MOBA_DH))
            v_s.append(vn.reshape(n_dec, l_dec, MOBA_HEADS, MOBA_DH))
        if proj_p is None:
            xp = _ffn(xp, gf, wup, wdn, tm_p, gfinal=last)
            xs = _ffn(xs, gf, wup, wdn, tm_s, gfinal=last)
        else:
            xp = _ffn(xp, gf, wup, wdn, tm_p, a=proj_p[0], wo=proj_p[1], gfinal=last)
            xs = _ffn(xs, gf, wup, wdn, tm_s, a=proj_s[0], wo=proj_s[1], gfinal=last)

    return (xp.reshape(n_pr, l_pr, D_MODEL), xs.reshape(n_dec, l_dec, D_MODEL),
            jnp.stack(k_p), jnp.stack(v_p), jnp.stack(k_s), jnp.stack(v_s),
            jnp.stack(gla_p), jnp.stack(gla_s), jnp.stack(sgu_s))
```

```python
import functools
import math

import numpy as np
import jax
import jax.numpy as jnp
from jax import lax
from jax.experimental import pallas as pl
from jax.experimental.pallas import tpu as pltpu

F32 = jnp.float32
BF16 = jnp.bfloat16

D_MODEL = 1024
EPS = 1e-6
NEG = -1e30

LANES = 128
SUBLANES = 8
MXU_DIM = 256
VMEM_LIMIT = 56 * 1024 * 1024

A_CHUNK = 128
A_HALF = 3 * D_MODEL
A_GROUPS = 8
A_GROUP_DIM = A_HALF // A_GROUPS
A_PANEL = 2 * A_GROUP_DIM
A_NPANEL = A_HALF // A_PANEL

GLA_HEADS = 4
GLA_DK = 128
GLA_DV = 256
GLA_GATE_RANK = 16
GLA_GATE_NORM = 16.0
GLA_C = 256

MOBA_HEADS = 16
MOBA_DH = 64
MOBA_BLOCK = 256
MOBA_TOPK = 3
MOBA_HG = MXU_DIM // MOBA_DH
MOBA_ONES_ROWS = 2 * SUBLANES
LOG2E = math.log2(math.e)
PAGE_SIZE = 128
REL_BUCKETS = 32
REL_MAX_DIST = 128

D_FF = 2816
FF_CHUNK = MXU_DIM
FF_NCHUNK = D_FF // FF_CHUNK

NT_DIMS = (((1,), (1,)), ((), ()))
TN_DIMS = (((0,), (0,)), ((), ()))


def _cparams(sem):
    return pltpu.CompilerParams(dimension_semantics=sem, vmem_limit_bytes=VMEM_LIMIT)


def _resident(w):
    if isinstance(w, tuple):
        arr, layer = w
        nd = arr.ndim - 1
        return pl.BlockSpec((None,) + arr.shape[1:], lambda *_: (layer,) + (0,) * nd,
                            pipeline_mode=pl.Buffered(1))
    nd = w.ndim
    return pl.BlockSpec(w.shape, lambda *_: (0,) * nd, pipeline_mode=pl.Buffered(1))


def _operand(w):
    return w[0] if isinstance(w, tuple) else w


def _rms(x, g):
    return x * lax.rsqrt(jnp.mean(x * x, axis=-1, keepdims=True) + EPS) * g


def _gelu(x):
    return 0.5 * x * (1.0 + lax.erf(x * (1.0 / math.sqrt(2.0))))


def _silu(x):
    return x * jax.nn.sigmoid(x)


def _dot(a, b):
    return jnp.dot(a, b, preferred_element_type=F32)


def _ffn_body(*refs, has_proj, final):
    refs = list(refs)
    x_ref = refs.pop(0)
    a_ref = wo_ref = gf_ref = None
    if has_proj:
        a_ref = refs.pop(0)
        wo_ref = refs.pop(0)
    g_ref = refs.pop(0)
    wup_ref = refs.pop(0)
    wdn_ref = refs.pop(0)
    if final:
        gf_ref = refs.pop(0)
    o_ref = refs.pop(0)

    x = x_ref[...]
    if has_proj:
        x = x + _dot(a_ref[...].astype(BF16), wo_ref[...])
    h = _rms(x, g_ref[...]).astype(BF16)
    o_ref[...] = x

    for c in range(FF_NCHUNK):
        cols = slice(c * FF_CHUNK, (c + 1) * FF_CHUNK)
        gate = _dot(h, wup_ref[:, cols])
        up = _dot(h, wup_ref[:, D_FF + c * FF_CHUNK:D_FF + (c + 1) * FF_CHUNK])
        act = (_silu(gate) * up).astype(BF16)
        o_ref[...] += _dot(act, wdn_ref[cols, :])
    if final:
        o_ref[...] = _rms(o_ref[...], gf_ref[...])


def _ffn(x, g, wup, wdn, tm, a=None, wo=None, gfinal=None):
    rows = x.shape[0]
    row_spec = pl.BlockSpec((tm, D_MODEL), lambda i: (i, 0))
    args = [x]
    specs = [row_spec]
    if a is not None:
        args += [a, wo]
        specs += [row_spec, _resident(wo)]
    args += [g, _operand(wup), _operand(wdn)]
    specs += [_resident(g), _resident(wup), _resident(wdn)]
    if gfinal is not None:
        args.append(gfinal)
        specs.append(_resident(gfinal))
    return pl.pallas_call(
        functools.partial(_ffn_body, has_proj=a is not None, final=gfinal is not None),
        grid=(rows // tm,),
        in_specs=specs,
        out_specs=row_spec,
        out_shape=jax.ShapeDtypeStruct((rows, D_MODEL), F32),
        compiler_params=_cparams(("parallel",)),
        name="ffn",
    )(*args)


def _sgu_body(*refs, tm, emit_v):
    (x_ref, g_ref, win_ref, lng_ref, lnb_ref, wm_ref, sb_ref, wout_ref) = refs[:8]
    if emit_v:
        o_ref, vout_ref, vbuf, ubuf, sbuf = refs[8:]
    else:
        o_ref, vbuf, ubuf, sbuf = refs[8:]
        vout_ref = None
    panel = lambda p, base=0: slice(base + p * A_PANEL, base + (p + 1) * A_PANEL)

    x = x_ref[...]
    h = _rms(x, g_ref[...]).astype(BF16)

    rsum = jnp.zeros((tm, 1), F32)
    for p in range(A_NPANEL):
        v = _gelu(_dot(h, win_ref[:, panel(p, A_HALF)]))
        vbuf[p] = v
        rsum = rsum + jnp.sum(v, axis=-1, keepdims=True)
    for p in range(A_NPANEL):
        ubuf[p] = _gelu(_dot(h, win_ref[:, panel(p)]))
    mean = rsum * (1.0 / A_HALF)
    ssq = jnp.zeros((tm, 1), F32)
    for p in range(A_NPANEL):
        d = vbuf[p] - mean
        ssq = ssq + jnp.sum(d * d, axis=-1, keepdims=True)
    rstd = lax.rsqrt(ssq * (1.0 / A_HALF) + EPS)

    acc = x
    for p in range(A_NPANEL):
        vn = (vbuf[p] - mean) * rstd * lng_ref[:, panel(p)] + lnb_ref[:, panel(p)]
        if emit_v:
            vout_ref[:, p * A_PANEL:(p + 1) * A_PANEL] = vn
        vnb = vn.astype(BF16)
        for c in range(tm // A_CHUNK):
            for gg in range(2):
                grp = 2 * p + gg
                rs = slice(c * A_CHUNK, (c + 1) * A_CHUNK)
                cs = slice(gg * A_GROUP_DIM, (gg + 1) * A_GROUP_DIM)
                sbuf[rs, cs] = _dot(wm_ref[grp], vnb[rs, cs]) + sb_ref[grp]
        acc = acc + _dot((ubuf[p] * sbuf[...]).astype(BF16), wout_ref[panel(p), :])
    o_ref[...] = acc


def _sgu(x, g, win, lng, lnb, wm, sb, wout, tm, emit_v):
    rows = x.shape[0]
    row_spec = pl.BlockSpec((tm, D_MODEL), lambda i: (i, 0))
    weights = [g, win, lng, lnb, wm, sb, wout]
    out_shape = [jax.ShapeDtypeStruct((rows, D_MODEL), F32)]
    out_specs = [row_spec]
    if emit_v:
        out_shape.append(jax.ShapeDtypeStruct((rows, A_HALF), F32))
        out_specs.append(pl.BlockSpec((tm, A_HALF), lambda i: (i, 0)))
    return pl.pallas_call(
        functools.partial(_sgu_body, tm=tm, emit_v=emit_v),
        grid=(rows // tm,),
        in_specs=[row_spec] + [_resident(w) for w in weights],
        out_specs=out_specs,
        out_shape=out_shape,
        scratch_shapes=[pltpu.VMEM((A_NPANEL, tm, A_PANEL), F32),
                        pltpu.VMEM((A_NPANEL, tm, A_PANEL), F32),
                        pltpu.VMEM((tm, A_PANEL), F32)],
        compiler_params=_cparams(("parallel",)),
        name="sgu",
    )(x, *[_operand(w) for w in weights])


def _log_sigmoid(z):
    return jnp.minimum(z, 0.0) - jnp.log1p(jnp.exp(-jnp.abs(z)))


def _gla_intra(qh, kh, cum, same_block):
    c = qh.shape[0]
    r = lax.broadcasted_iota(jnp.int32, (c, 1), 0)
    a = jnp.zeros((c, c), F32)
    m = c
    while m >= 2:
        half = m // 2
        pos = r & (m - 1)
        upper = pos >= half
        if m >= 2 * SUBLANES:
            pieces = []
            for b in range(c // m):
                row = b * m + half - 1
                pieces.append(jnp.broadcast_to(cum[row:row + 1, :], (m, cum.shape[1])))
            ref = pieces[0] if len(pieces) == 1 else jnp.concatenate(pieces, axis=0)
        else:
            c8 = cum.reshape(c // SUBLANES, SUBLANES, cum.shape[1])
            sub = lax.broadcasted_iota(jnp.int32, (1, SUBLANES, 1), 1)
            ref = None
            for b in reversed(range(SUBLANES // m)):
                row = jnp.broadcast_to(c8[:, b * m + half - 1:b * m + half, :], c8.shape)
                ref = row if ref is None else jnp.where(sub < (b + 1) * m, row, ref)
            ref = ref.reshape(c, cum.shape[1])
        decay = jnp.exp(jnp.where(upper, cum - ref, ref - cum))
        x = jnp.where(upper, qh, kh) * decay
        qm = jnp.where(upper, x, 0.0).astype(BF16)
        km = jnp.where(upper, 0.0, x).astype(BF16)
        am = lax.dot_general(qm, km, NT_DIMS, preferred_element_type=F32)
        if m < c:
            am = jnp.where(same_block[m], am, 0.0)
        a = a + am
        m = half
    return a


def _gla_project(h, win_ref, wgl_ref, wgate_ref, bgate_ref):
    dq, dv = GLA_HEADS * GLA_DK, GLA_HEADS * GLA_DV
    q = _dot(h, win_ref[:, :dq]) * (GLA_DK ** -0.5)
    k = _dot(h, win_ref[:, dq:2 * dq])
    v = _dot(h, win_ref[:, 2 * dq:2 * dq + dv])
    g = _dot(h, win_ref[:, 2 * dq + dv:2 * dq + 2 * dv])
    gl = _dot(h, wgl_ref[...])
    z = _dot(gl.astype(BF16), wgate_ref[...]) + bgate_ref[...]
    return q, k, v, g, z


def _gla_prompt_body(x_ref, gm_ref, win_ref, wgl_ref, wgate_ref, bgate_ref,
                     gn_ref, wo_ref, o_ref, st_ref, s_scr):
    ci = pl.program_id(1)
    c = GLA_C

    @pl.when(ci == 0)
    def _():
        s_scr[...] = jnp.zeros_like(s_scr)

    x = x_ref[...]
    h = _rms(x, gm_ref[...]).astype(BF16)
    q, k, v, g, z = _gla_project(h, win_ref, wgl_ref, wgate_ref, bgate_ref)
    la = _log_sigmoid(z) * (1.0 / GLA_GATE_NORM)

    rowi = lax.broadcasted_iota(jnp.int32, (c, c), 0)
    coli = lax.broadcasted_iota(jnp.int32, (c, c), 1)
    tril = (coli <= rowi).astype(F32)
    cum = jnp.dot(tril, la, precision=lax.Precision.HIGHEST, preferred_element_type=F32)

    same_block = {}
    m = c // 2
    while m >= 2:
        sh = int(math.log2(m))
        same_block[m] = (rowi >> sh) == (coli >> sh)
        m //= 2

    outs = []
    for hd in range(GLA_HEADS):
        ks = slice(hd * GLA_DK, (hd + 1) * GLA_DK)
        vs = slice(hd * GLA_DV, (hd + 1) * GLA_DV)
        qh, kh, cumh, vh = q[:, ks], k[:, ks], cum[:, ks], v[:, vs]
        vhb = vh.astype(BF16)
        total = cumh[c - 1:c, :]
        st = s_scr[hd]
        o = lax.dot_general((qh * jnp.exp(cumh)).astype(BF16), st.astype(BF16), NT_DIMS,
                            preferred_element_type=F32)
        a = _gla_intra(qh, kh, cumh, same_block)
        o = o + _dot(a.astype(BF16), vhb)
        o = o + jnp.sum(qh * kh, axis=-1, keepdims=True) * vh
        kdec = (kh * jnp.exp(total - cumh)).astype(BF16)
        st_new = st * jnp.exp(total) + lax.dot_general(vhb, kdec, TN_DIMS,
                                                       preferred_element_type=F32)
        s_scr[hd] = st_new
        o = o * lax.rsqrt(jnp.mean(o * o, axis=-1, keepdims=True) + EPS) * gn_ref[...]
        outs.append(o)
    y = jnp.concatenate(outs, axis=1) * _silu(g)
    o_ref[...] = x + _dot(y.astype(BF16), wo_ref[...])

    @pl.when(ci == pl.num_programs(1) - 1)
    def _():
        for hd in range(GLA_HEADS):
            st_ref[0, hd] = s_scr[hd].T


def _gla_prompt(x, gm, win, wgl, wgate, bgate, gn, wo, n_seq, seq_len):
    n_chunk = seq_len // GLA_C
    weights = [gm, win, wgl, wgate, bgate, gn, wo]
    row_spec = pl.BlockSpec((GLA_C, D_MODEL), lambda b, c: (b * n_chunk + c, 0))
    return pl.pallas_call(
        _gla_prompt_body,
        grid=(n_seq, n_chunk),
        in_specs=[row_spec] + [_resident(w) for w in weights],
        out_specs=[row_spec,
                   pl.BlockSpec((1, GLA_HEADS, GLA_DK, GLA_DV), lambda b, c: (b, 0, 0, 0))],
        out_shape=[jax.ShapeDtypeStruct(x.shape, F32),
                   jax.ShapeDtypeStruct((n_seq, GLA_HEADS, GLA_DK, GLA_DV), F32)],
        scratch_shapes=[pltpu.VMEM((GLA_HEADS, GLA_DV, GLA_DK), F32)],
        compiler_params=_cparams(("parallel", "arbitrary")),
        name="gla_prompt",
    )(x, *weights)


def _gla_sample_proj_body(x_ref, gm_ref, win_ref, wgl_ref, wgate_ref, bgate_ref,
                          q_ref, k_ref, a_ref, v_ref, g_ref):
    h = _rms(x_ref[...], gm_ref[...]).astype(BF16)
    q_ref[...], k_ref[...], v_ref[...], g_ref[...], z = _gla_project(
        h, win_ref, wgl_ref, wgate_ref, bgate_ref)
    a_ref[...] = jnp.exp(_log_sigmoid(z) * (1.0 / GLA_GATE_NORM))


GLA_SEQ_PER_STEP = SUBLANES


def _gla_sample_rec_body(s_ref, qc_ref, kc_ref, ac_ref, v_ref, so_ref, o_ref):
    for j in range(GLA_SEQ_PER_STEP):
        for hd in range(GLA_HEADS):
            ks = slice(hd * GLA_DK, (hd + 1) * GLA_DK)
            vs = slice(hd * GLA_DV, (hd + 1) * GLA_DV)
            acol = ac_ref[0, ks, j:j + 1]
            kcol = kc_ref[0, ks, j:j + 1]
            qcol = qc_ref[0, ks, j:j + 1]
            vrow = v_ref[j:j + 1, vs]
            s_new = s_ref[j, hd] * acol + kcol * vrow
            so_ref[j, hd] = s_new
            o_ref[j:j + 1, vs] = jnp.sum(s_new * qcol, axis=0, keepdims=True)


def _gla_sample_out_body(x_ref, o_ref, g_ref, gn_ref, wo_ref, y_ref):
    outs = []
    for hd in range(GLA_HEADS):
        o = o_ref[:, hd * GLA_DV:(hd + 1) * GLA_DV]
        outs.append(o * lax.rsqrt(jnp.mean(o * o, axis=-1, keepdims=True) + EPS) * gn_ref[...])
    y = jnp.concatenate(outs, axis=1) * _silu(g_ref[...])
    y_ref[...] = x_ref[...] + _dot(y.astype(BF16), wo_ref[...])


def _gla_sample(x, state, gm, win, wgl, wgate, bgate, gn, wo):
    n = x.shape[0]
    dq = GLA_HEADS * GLA_DK
    weights = [gm, win, wgl, wgate, bgate]
    q, k, a, v, g = pl.pallas_call(
        _gla_sample_proj_body,
        out_shape=[jax.ShapeDtypeStruct((n, dq), F32)] * 3
        + [jax.ShapeDtypeStruct((n, D_MODEL), F32)] * 2,
        compiler_params=pltpu.CompilerParams(vmem_limit_bytes=VMEM_LIMIT),
        name="gla_sample_proj",
    )(x, *weights)

    nstep = n // GLA_SEQ_PER_STEP

    def cols(t):
        return t.reshape(nstep, GLA_SEQ_PER_STEP, dq).transpose(0, 2, 1)

    col_spec = pl.BlockSpec((1, dq, GLA_SEQ_PER_STEP), lambda i: (i, 0, 0))
    st_spec = pl.BlockSpec((GLA_SEQ_PER_STEP, GLA_HEADS, GLA_DK, GLA_DV), lambda i: (i, 0, 0, 0))
    row_spec = pl.BlockSpec((GLA_SEQ_PER_STEP, D_MODEL), lambda i: (i, 0))
    s_new, o = pl.pallas_call(
        _gla_sample_rec_body,
        grid=(nstep,),
        in_specs=[st_spec, col_spec, col_spec, col_spec, row_spec],
        out_specs=[st_spec, row_spec],
        out_shape=[jax.ShapeDtypeStruct(state.shape, F32), jax.ShapeDtypeStruct((n, D_MODEL), F32)],
        compiler_params=_cparams(("parallel",)),
        name="gla_sample_rec",
    )(state, cols(q), cols(k), cols(a), v)

    y = pl.pallas_call(
        _gla_sample_out_body,
        out_shape=jax.ShapeDtypeStruct((n, D_MODEL), F32),
        compiler_params=pltpu.CompilerParams(vmem_limit_bytes=VMEM_LIMIT),
        name="gla_sample_out",
    )(x, o, g, gn, wo)
    return y, s_new


def _t5_bucket_np(dist):
    max_exact = REL_BUCKETS // 2
    n = np.maximum(dist, 0)
    nf = np.maximum(n, max_exact).astype(np.float32)
    large = max_exact + (np.log(nf / np.float32(max_exact)) / np.float32(math.log(REL_MAX_DIST / max_exact))
                         * np.float32(REL_BUCKETS - max_exact)).astype(np.int32)
    return np.where(n < max_exact, n, np.minimum(large, REL_BUCKETS - 1)).astype(np.int32)


def _bias_tables_body(rb_ref, bd_ref, b1_ref, bdec_ref, od_ref, o1_ref, odec_ref):
    h = pl.program_id(0)
    td, t1, tdec = bd_ref[...], b1_ref[...], bdec_ref[...]
    ad = jnp.zeros(td.shape, F32)
    a1 = jnp.zeros(t1.shape, F32)
    adec = jnp.zeros(tdec.shape, F32)
    for b in range(REL_BUCKETS):
        val = rb_ref[b, h]
        ad = jnp.where(td == b, val, ad)
        a1 = jnp.where(t1 == b, val, a1)
        adec = jnp.where(tdec == b, val, adec)
    od_ref[0] = ad * LOG2E
    o1_ref[0] = a1 * LOG2E
    odec_ref[0] = adec


def _bias_tables(rel_bias, past_len):
    kc = np.arange(MOBA_BLOCK)[:, None]
    qr = np.arange(MOBA_BLOCK)[None, :]
    bd = _t5_bucket_np(qr - kc)
    b1 = _t5_bucket_np(MOBA_BLOCK + qr - kc)
    nblk = past_len // MOBA_BLOCK
    kpos = np.arange(nblk)[:, None] * MOBA_BLOCK + np.arange(MOBA_BLOCK)[None, :]
    bdec = _t5_bucket_np(past_len - kpos)
    full = lambda a: pl.BlockSpec(a.shape, lambda h: (0,) * a.ndim)
    return pl.pallas_call(
        _bias_tables_body,
        grid=(MOBA_HEADS,),
        in_specs=[pl.BlockSpec(memory_space=pltpu.SMEM), full(bd), full(b1), full(bdec)],
        out_specs=[pl.BlockSpec((1,) + bd.shape, lambda h: (h, 0, 0)),
                   pl.BlockSpec((1,) + b1.shape, lambda h: (h, 0, 0)),
                   pl.BlockSpec((1,) + bdec.shape, lambda h: (h, 0, 0))],
        out_shape=[jax.ShapeDtypeStruct((MOBA_HEADS,) + bd.shape, F32),
                   jax.ShapeDtypeStruct((MOBA_HEADS,) + b1.shape, F32),
                   jax.ShapeDtypeStruct((MOBA_HEADS,) + bdec.shape, F32)],
        compiler_params=_cparams(("parallel",)),
        name="moba_bias_tables",
    )(rel_bias, jnp.asarray(bd), jnp.asarray(b1), jnp.asarray(bdec))


def _moba_qkv_body(x_ref, g_ref, w_ref, q_ref, k_ref, v_ref, *t_refs):
    h = _rms(x_ref[...], g_ref[...]).astype(BF16)
    qkv = _dot(h, w_ref[...])
    q_ref[...] = qkv[:, :D_MODEL] * (MOBA_DH ** -0.5)
    k = qkv[:, D_MODEL:2 * D_MODEL]
    v = qkv[:, 2 * D_MODEL:]
    k_ref[...] = k
    v_ref[...] = v
    if t_refs:
        t_refs[0][0] = k.T
        t_refs[1][0] = v.T


def _moba_qkv(x, g, w, tm, seq_len=None):
    rows = x.shape[0]
    row_spec = pl.BlockSpec((tm, D_MODEL), lambda i: (i, 0))
    out_specs = [row_spec] * 3
    out_shape = [jax.ShapeDtypeStruct((rows, D_MODEL), F32)] * 3
    if seq_len is not None:
        per_seq = seq_len // tm
        t_spec = pl.BlockSpec((1, D_MODEL, tm), lambda i: (i // per_seq, 0, i % per_seq))
        out_specs += [t_spec] * 2
        out_shape += [jax.ShapeDtypeStruct((rows // seq_len, D_MODEL, seq_len), F32)] * 2
    return pl.pallas_call(
        _moba_qkv_body,
        grid=(rows // tm,),
        in_specs=[row_spec, _resident(g), _resident(w)],
        out_specs=out_specs,
        out_shape=out_shape,
        compiler_params=_cparams(("parallel",)),
        name="moba_qkv",
    )(x, g, w)


def _topk_select(gate, valid, axis):
    n = gate.shape[axis]
    idx = lax.broadcasted_iota(jnp.int32, gate.shape, axis)
    cnt = jnp.zeros(gate.shape, jnp.int32)
    for m in range(n):
        gm = lax.slice_in_dim(gate, m, m + 1, axis=axis)
        beats = (gm > gate) | ((gm == gate) & (m < idx))
        cnt = cnt + jnp.where(beats, valid(m), 0)
    return jnp.where(cnt < MOBA_TOPK, 1.0, 0.0)


def _moba_prompt_body(rb_ref, q_ref, k_ref, v_ref, bd_ref, b1_ref, o_ref,
                      kb_scr, vt_scr, km_scr):
    hg = pl.program_id(1)
    qi = pl.program_id(2)
    nblk = kb_scr.shape[0]
    blk = MOBA_BLOCK
    dh = MOBA_DH
    width = MOBA_HG * dh

    lane = lax.broadcasted_iota(jnp.int32, (1, width), 1)
    in_head = [(lane >= hh * dh) & (lane < (hh + 1) * dh) for hh in range(MOBA_HG)]

    @pl.when(qi == 0)
    def _():
        ones = jnp.ones((MOBA_ONES_ROWS, blk), BF16)
        means = []
        for n in range(nblk):
            kt = k_ref[n * blk:(n + 1) * blk, :]
            kb_scr[n] = kt.astype(BF16)
            means.append(jnp.mean(kt, axis=0, keepdims=True))
            vt = v_ref[n * blk:(n + 1) * blk, :].T.astype(BF16)
            for hh in range(MOBA_HG):
                vt_scr[n, hh, :dh, :] = vt[hh * dh:(hh + 1) * dh, :]
                vt_scr[n, hh, dh:, :] = ones
        km = jnp.concatenate(means, axis=0)
        per_head = [jnp.where(in_head[hh], km, 0.0) for hh in range(MOBA_HG)]
        pad = jnp.zeros((LANES - MOBA_HG * nblk, width), F32)
        km_scr[...] = jnp.concatenate(per_head + [pad], axis=0).T

    q2 = q_ref[...]
    ki_idx = lax.broadcasted_iota(jnp.int32, (blk, blk), 0)
    qi_idx = lax.broadcasted_iota(jnp.int32, (blk, blk), 1)
    blk_idx = lax.broadcasted_iota(jnp.int32, (nblk, blk), 0)

    hs = range(MOBA_HG)
    qhb, far_bias, neg, tile_of = [], [], [], []
    gates = jnp.dot(q2, km_scr[...], precision=lax.Precision.HIGHEST, preferred_element_type=F32).T
    for hh in hs:
        tile_of.append(slice(hh * dh // LANES * LANES, (hh * dh // LANES + 1) * LANES))
        qhb.append((jnp.where(in_head[hh], q2, 0.0)[:, tile_of[hh]] * LOG2E).astype(BF16))
        far_bias.append(rb_ref[REL_BUCKETS - 1, MOBA_HG * hg + hh] * LOG2E)
        gate = gates[hh * nblk:(hh + 1) * nblk, :]
        keep = _topk_select(gate, lambda m: (m < qi).astype(jnp.int32), 0)
        neg.append(jnp.where((keep > 0.0) & (blk_idx < qi), 0.0, NEG))

    def scores(k0, n):
        kt = kb_scr[k0:k0 + n].reshape(n * blk, width)
        return [lax.dot_general(kt[:, tile_of[hh]], qhb[hh], NT_DIMS, preferred_element_type=F32)
                for hh in hs]

    def values(ki, p):
        return [_dot(vt_scr[ki, hh], p[hh]) for hh in hs]

    def weights(x):
        return jnp.exp2(x.astype(BF16))

    def step(c):
        units = [(c - 1, 2)] if c >= 1 else [(0, 1)]
        t = 0
        while t < c - 1:
            n = min(2, c - 1 - t)
            units.append((t, n))
            t += n
        m = [None] * MOBA_HG
        acc = [None] * MOBA_HG
        s_next = scores(*units[0])
        for ui, (k0, n) in enumerate(units):
            s = s_next
            if ui + 1 < len(units):
                s_next = scores(*units[ui + 1])
            p = [[] for _ in range(n)]
            alpha = [None] * MOBA_HG
            for hh in hs:
                sh, rows, tile_max = [], [], []
                for t in range(n):
                    kb = k0 + t
                    x = s[hh][t * blk:(t + 1) * blk, :]
                    if kb == c:
                        x, r = jnp.where(ki_idx <= qi_idx, x + bd_ref[hh], NEG), None
                    elif kb == c - 1:
                        x, r = x + b1_ref[hh], neg[hh][kb:kb + 1, :]
                    else:
                        r = neg[hh][kb:kb + 1, :] + far_bias[hh]
                    mx = jnp.max(x, axis=0, keepdims=True)
                    sh.append(x)
                    rows.append(r)
                    tile_max.append(mx if r is None else mx + r)
                m_new = functools.reduce(jnp.maximum, tile_max if m[hh] is None else [m[hh]] + tile_max)
                for t in range(n):
                    p[t].append(weights(sh[t] - (m_new if rows[t] is None else m_new - rows[t])))
                if m[hh] is not None:
                    alpha[hh] = jnp.exp2(m[hh] - m_new)
                m[hh] = m_new
            pv = [values(k0 + t, p[t]) for t in range(n)]
            for hh in hs:
                tot = functools.reduce(lambda x, y: x + y, [pv[t][hh] for t in range(n)])
                acc[hh] = tot if acc[hh] is None else alpha[hh] * acc[hh] + tot
        ot = jnp.concatenate([acc[hh][:dh, :] / acc[hh][dh:dh + 1, :] for hh in hs], axis=0)
        o_ref[...] = ot.T

    for c in range(nblk):
        pl.when(qi == c)(functools.partial(step, c))


def _moba_prompt(rel_bias, q, k, v, bias_d, bias_1, n_seq, seq_len):
    nblk = seq_len // MOBA_BLOCK
    hgrp = MOBA_HG
    width = hgrp * MOBA_DH
    q_spec = pl.BlockSpec((MOBA_BLOCK, width), lambda b, hg, qi: (b * nblk + qi, hg))
    kv_spec = pl.BlockSpec((seq_len, width), lambda b, hg, qi: (b, hg))
    bias_spec = pl.BlockSpec((hgrp, MOBA_BLOCK, MOBA_BLOCK), lambda b, hg, qi: (hg, 0, 0))
    return pl.pallas_call(
        _moba_prompt_body,
        grid=(n_seq, MOBA_HEADS // hgrp, nblk),
        in_specs=[pl.BlockSpec(memory_space=pltpu.SMEM), q_spec, kv_spec, kv_spec, bias_spec, bias_spec],
        out_specs=q_spec,
        out_shape=jax.ShapeDtypeStruct(q.shape, F32),
        scratch_shapes=[
            pltpu.VMEM((nblk, MOBA_BLOCK, width), BF16),
            pltpu.VMEM((nblk, hgrp, MOBA_DH + MOBA_ONES_ROWS, MOBA_BLOCK), BF16),
            pltpu.VMEM((width, LANES), F32),
        ],
        compiler_params=_cparams(("parallel", "parallel", "arbitrary")),
        name="moba_prompt",
    )(rel_bias, q, k, v, bias_d, bias_1)


def _col_to_row(col):
    n = col.shape[0]
    r = lax.broadcasted_iota(jnp.int32, (n, n), 0)
    c = lax.broadcasted_iota(jnp.int32, (n, n), 1)
    return jnp.sum(jnp.where(r == c, jnp.broadcast_to(col, (n, n)), 0.0), axis=0, keepdims=True)


def _moba_decode_body(pt_ref, q_ref, qt_ref, kn_ref, vnt_ref, rb0_ref, bdec_ref, *refs, npg):
    del pt_ref
    ck, cv = refs[:npg], refs[npg:2 * npg]
    o_ref, qb_scr, s_scr, p_scr, t_scr = refs[2 * npg:]
    heads, dh = MOBA_HEADS, MOBA_DH
    ppb = MOBA_BLOCK // PAGE_SIZE
    nb = npg // ppb

    qt = qt_ref[0]
    for h in range(heads):
        qb_scr[h] = jnp.broadcast_to(qt[:, h:h + 1], (dh, PAGE_SIZE))

    def k_step(h, carry):
        qb = qb_scr[h]
        for pg in range(npg):
            s_scr[pg, pl.ds(h, 1), :] = jnp.sum(ck[pg][0, h] * qb, axis=0, keepdims=True)
        return carry

    lax.fori_loop(0, heads, k_step, 0)

    z, m_blk = [], []
    blk_lane = lax.broadcasted_iota(jnp.int32, (heads, nb), 1)
    gate = jnp.zeros((heads, nb), F32)
    for b in range(nb):
        zb = []
        g = jnp.zeros((heads, 1), F32)
        for j in range(ppb):
            s = s_scr[ppb * b + j]
            g = g + jnp.sum(s, axis=-1, keepdims=True)
            zb.append(s + bdec_ref[ppb * b + j])
        gate = jnp.where(blk_lane == b, g, gate)
        m_blk.append(functools.reduce(jnp.maximum, [jnp.max(t, axis=-1, keepdims=True) for t in zb]))
        z += zb
    keep = _topk_select(gate, lambda m_: 1, 1)

    l_new = jnp.sum(q_ref[0] * kn_ref[0], axis=-1, keepdims=True) + rb0_ref[...]
    m_all = l_new
    for b in range(nb):
        m_all = jnp.maximum(m_all, jnp.where(keep[:, b:b + 1] > 0.0, m_blk[b], NEG))
    w_new = jnp.exp(l_new - m_all)
    den = w_new
    for b in range(nb):
        kept = keep[:, b:b + 1] > 0.0
        for j in range(ppb):
            p = jnp.exp(jnp.where(kept, z[ppb * b + j] - m_all, NEG))
            den = den + jnp.sum(p, axis=-1, keepdims=True)
            p_scr[ppb * b + j] = p

    def v_step(h, carry):
        acc = cv[0][0, h] * p_scr[0, pl.ds(h, 1), :]
        for pg in range(1, npg):
            acc = acc + cv[pg][0, h] * p_scr[pg, pl.ds(h, 1), :]
        t_scr[h] = jnp.broadcast_to(jnp.sum(acc, axis=-1, keepdims=True), (dh, PAGE_SIZE))
        return carry

    lax.fori_loop(0, heads, v_step, 0)

    head_lane = lax.broadcasted_iota(jnp.int32, (dh, heads), 1)
    tot = jnp.zeros((dh, heads), F32)
    for h in range(heads):
        tot = jnp.where(head_lane == h, t_scr[h][:, :heads], tot)
    o_ref[0] = (tot + vnt_ref[0] * _col_to_row(w_new)) / _col_to_row(den)


def _moba_decode(page_table, q, k_new, v_new, rb0, cache_k, cache_v, bias_dec):
    n_dec, n_pages = page_table.shape
    heads, dh = MOBA_HEADS, MOBA_DH
    rows = lambda t: t.reshape(n_dec, heads, dh)
    cols = lambda t: t.reshape(n_dec, heads, dh).transpose(0, 2, 1)
    row_spec = pl.BlockSpec((1, heads, dh), lambda b, pt: (b, 0, 0))
    col_spec = pl.BlockSpec((1, dh, heads), lambda b, pt: (b, 0, 0))
    page_specs = [pl.BlockSpec((1, heads, dh, PAGE_SIZE), lambda b, pt, j=j: (pt[b, j], 0, 0, 0))
                  for j in range(n_pages)]
    grid_spec = pltpu.PrefetchScalarGridSpec(
        num_scalar_prefetch=1,
        grid=(n_dec,),
        in_specs=[row_spec, col_spec, row_spec, col_spec,
                  pl.BlockSpec((heads, 1), lambda b, pt: (0, 0)),
                  pl.BlockSpec(bias_dec.shape, lambda b, pt: (0, 0, 0))] + page_specs + page_specs,
        out_specs=col_spec,
        scratch_shapes=[pltpu.VMEM((heads, dh, PAGE_SIZE), F32),
                        pltpu.VMEM((n_pages, heads, PAGE_SIZE), F32),
                        pltpu.VMEM((n_pages, heads, PAGE_SIZE), F32),
                        pltpu.VMEM((heads, dh, PAGE_SIZE), F32)],
    )
    out = pl.pallas_call(
        functools.partial(_moba_decode_body, npg=n_pages),
        grid_spec=grid_spec,
        out_shape=jax.ShapeDtypeStruct((n_dec, dh, heads), F32),
        compiler_params=_cparams(("parallel",)),
        name="moba_decode",
    )(page_table, rows(q), cols(q), rows(k_new), cols(v_new), rb0, bias_dec,
      *([cache_k] * n_pages), *([cache_v] * n_pages))
    return out.transpose(0, 2, 1).reshape(n_dec, heads * dh)


def _prep_sgu(ln_g, ln_b, w_s, b_s):
    lng = ln_g.reshape(1, A_HALF)
    lnb = ln_b.reshape(1, A_HALF)
    causal = jnp.tril(jnp.ones((A_CHUNK, A_CHUNK), dtype=bool))
    wm_p = jnp.where(causal[None], w_s, 0.0).astype(BF16)
    sb_p = jnp.broadcast_to(b_s[:, :, None], (A_GROUPS, A_CHUNK, A_GROUP_DIM))
    eye = jnp.eye(A_CHUNK, dtype=F32)
    wm_s = (w_s[:, 0, 0][:, None, None] * eye[None]).astype(BF16)
    sb_s = jnp.broadcast_to(b_s[:, 0][:, None, None], (A_GROUPS, A_CHUNK, A_GROUP_DIM))
    return (lng, lnb), (wm_p, sb_p), (wm_s, sb_s)


def _prep_gla(w_in, w_gate, b_gate, gn, w_out):
    dq, dv = GLA_HEADS * GLA_DK, GLA_HEADS * GLA_DV
    pad = LANES - GLA_GATE_RANK
    wgl = jnp.pad(w_in[:, 2 * dq + 2 * dv:], ((0, 0), (0, pad))).astype(BF16)
    wgate = jnp.pad(w_gate, ((0, pad), (0, 0))).astype(BF16)
    return (w_in.astype(BF16), wgl, wgate, b_gate.reshape(1, dq), gn.reshape(1, GLA_DV),
            w_out.astype(BF16))


def kernel(x_prompt, x_sample, cache_k, cache_v, page_table, state_gla, norm_mix, norm_ffn, norm_final,
           w_in_a, ln_a_g, ln_a_b, w_s_a, b_s_a, w_out_a, w_in_b, w_gate_b, b_gate_b, gn_b, w_out_b,
           w_in_c, w_out_c, rel_bias, w_up, w_down):
    n_pr, l_pr, _ = x_prompt.shape
    n_dec, l_dec, _ = x_sample.shape
    assert l_dec == 1 and l_pr % MOBA_BLOCK == 0 and l_pr % GLA_C == 0
    depth = norm_mix.shape[0]
    past_len = page_table.shape[1] * PAGE_SIZE
    assert past_len % MOBA_BLOCK == 0
    tm_p, tm_s = 512, n_dec

    xp = x_prompt.reshape(n_pr * l_pr, D_MODEL)
    xs = x_sample.reshape(n_dec, D_MODEL)
    gfin = norm_final.reshape(1, D_MODEL)
    w_up_b, w_down_b = w_up.astype(BF16), w_down.astype(BF16)
    w_in_a_b, w_out_a_b = w_in_a.astype(BF16), w_out_a.astype(BF16)
    k_p, v_p, k_s, v_s, gla_p, gla_s, sgu_s = [], [], [], [], [], [], []

    for i in range(depth):
        kind, j = i % 3, i // 3
        gm = norm_mix[i].reshape(1, D_MODEL)
        gf = norm_ffn[i].reshape(1, D_MODEL)
        wup, wdn = (w_up_b, i), (w_down_b, i)
        last = gfin if i == depth - 1 else None
        proj_p = proj_s = None
        if kind == 0:
            (lng, lnb), mode_p, mode_s = _prep_sgu(ln_a_g[j], ln_a_b[j], w_s_a[j], b_s_a[j])
            win, wout = (w_in_a_b, j), (w_out_a_b, j)
            (xp,) = _sgu(xp, gm, win, lng, lnb, *mode_p, wout, tm_p, False)
            xs, v_rows = _sgu(xs, gm, win, lng, lnb, *mode_s, wout, tm_s, True)
            sgu_s.append(v_rows.reshape(n_dec, l_dec, A_HALF))
        elif kind == 1:
            wts = _prep_gla(w_in_b[j], w_gate_b[j], b_gate_b[j], gn_b[j], w_out_b[j])
            xp, sp = _gla_prompt(xp, gm, *wts, n_pr, l_pr)
            xs, ss = _gla_sample(xs, state_gla[j], gm, *wts)
            gla_p.append(sp)
            gla_s.append(ss)
        else:
            w_in = w_in_c[j].astype(BF16)
            bias_d, bias_1, bias_dec = _bias_tables(rel_bias, past_len)
            qp, kp, vp, kpt, vpt = _moba_qkv(xp, gm, w_in, tm_p, seq_len=l_pr)
            op = _moba_prompt(rel_bias, qp, kp, vp, bias_d, bias_1, n_pr, l_pr)
            qs, kn, vn = _moba_qkv(xs, gm, w_in, tm_s)
            os_ = _moba_decode(page_table, qs, kn, vn, rel_bias[0].reshape(MOBA_HEADS, 1),
                               cache_k[j].transpose(0, 2, 3, 1), cache_v[j].transpose(0, 2, 3, 1),
                               bias_dec.reshape(MOBA_HEADS, -1, PAGE_SIZE).transpose(1, 0, 2))
            wo_c = w_out_c[j].astype(BF16)
            proj_p, proj_s = (op, wo_c), (os_, wo_c)
            heads_last = lambda t: t.reshape(n_pr, MOBA_HEADS, MOBA_DH, l_pr).transpose(0, 3, 1, 2)
            k_p.append(heads_last(kpt))
            v_p.append(heads_last(vpt))
            k_s.append(kn.reshape(n_dec, l_dec, MOBA_HEADS, MOBA_DH))
            v_s.append(vn.reshape(n_dec, l_dec, MOBA_HEADS, MOBA_DH))
        if proj_p is None:
            xp = _ffn(xp, gf, wup, wdn, tm_p, gfinal=last)
            xs = _ffn(xs, gf, wup, wdn, tm_s, gfinal=last)
        else:
            xp = _ffn(xp, gf, wup, wdn, tm_p, a=proj_p[0], wo=proj_p[1], gfinal=last)
            xs = _ffn(xs, gf, wup, wdn, tm_s, a=proj_s[0], wo=proj_s[1], gfinal=last)

    return (xp.reshape(n_pr, l_pr, D_MODEL), xs.reshape(n_dec, l_dec, D_MODEL),
            jnp.stack(k_p), jnp.stack(v_p), jnp.stack(k_s), jnp.stack(v_s),
            jnp.stack(gla_p), jnp.stack(gla_s), jnp.stack(sgu_s))
```

```python
import functools
import math

import numpy as np
import jax
import jax.numpy as jnp
from jax import lax
from jax.experimental import pallas as pl
from jax.experimental.pallas import tpu as pltpu

F32 = jnp.float32
BF16 = jnp.bfloat16

D_MODEL = 1024
EPS = 1e-6
NEG = -1e30

LANES = 128
SUBLANES = 8
MXU_DIM = 256
VMEM_LIMIT = 56 * 1024 * 1024

A_CHUNK = 128
A_HALF = 3 * D_MODEL
A_GROUPS = 8
A_GROUP_DIM = A_HALF // A_GROUPS
A_PANEL = 4 * A_GROUP_DIM
A_NPANEL = A_HALF // A_PANEL

GLA_HEADS = 4
GLA_DK = 128
GLA_DV = 256
GLA_GATE_RANK = 16
GLA_GATE_NORM = 16.0
GLA_C = 256

MOBA_HEADS = 16
MOBA_DH = 64
MOBA_BLOCK = 256
MOBA_TOPK = 3
MOBA_HG = MXU_DIM // MOBA_DH
MOBA_ONES_ROWS = 2 * SUBLANES
LOG2E = math.log2(math.e)
PAGE_SIZE = 128
REL_BUCKETS = 32
REL_MAX_DIST = 128

D_FF = 2816
FF_CHUNK = MXU_DIM
FF_NCHUNK = D_FF // FF_CHUNK

NT_DIMS = (((1,), (1,)), ((), ()))
TN_DIMS = (((0,), (0,)), ((), ()))


def _cparams(sem):
    return pltpu.CompilerParams(dimension_semantics=sem, vmem_limit_bytes=VMEM_LIMIT)


def _resident(w):
    if isinstance(w, tuple):
        arr, layer = w
        nd = arr.ndim - 1
        return pl.BlockSpec((None,) + arr.shape[1:], lambda *_: (layer,) + (0,) * nd,
                            pipeline_mode=pl.Buffered(1))
    nd = w.ndim
    return pl.BlockSpec(w.shape, lambda *_: (0,) * nd, pipeline_mode=pl.Buffered(1))


def _operand(w):
    return w[0] if isinstance(w, tuple) else w


def _rms(x, g):
    return x * lax.rsqrt(jnp.mean(x * x, axis=-1, keepdims=True) + EPS) * g


def _gelu(x):
    return 0.5 * x * (1.0 + lax.erf(x * (1.0 / math.sqrt(2.0))))


def _silu(x):
    return x * jax.nn.sigmoid(x)


def _dot(a, b):
    return jnp.dot(a, b, preferred_element_type=F32)


def _bf16_pieces(x, n):
    pieces = []
    for _ in range(n):
        p = x.astype(BF16)
        pieces.append(p)
        x = x - p.astype(F32)
    return pieces


def _ffn_body(*refs, has_proj, final):
    refs = list(refs)
    x_ref = refs.pop(0)
    a_ref = wo_ref = gf_ref = None
    if has_proj:
        a_ref = refs.pop(0)
        wo_ref = refs.pop(0)
    g_ref = refs.pop(0)
    wup_ref = refs.pop(0)
    wdn_ref = refs.pop(0)
    if final:
        gf_ref = refs.pop(0)
    o_ref = refs.pop(0)

    x = x_ref[...]
    if has_proj:
        x = x + _dot(a_ref[...].astype(BF16), wo_ref[...])
    h = _rms(x, g_ref[...]).astype(BF16)
    o_ref[...] = x

    for c in range(FF_NCHUNK):
        cols = slice(c * FF_CHUNK, (c + 1) * FF_CHUNK)
        gate = _dot(h, wup_ref[:, cols])
        up = _dot(h, wup_ref[:, D_FF + c * FF_CHUNK:D_FF + (c + 1) * FF_CHUNK])
        act = (_silu(gate) * up).astype(BF16)
        o_ref[...] += _dot(act, wdn_ref[cols, :])
    if final:
        o_ref[...] = _rms(o_ref[...], gf_ref[...])


def _ffn(x, g, wup, wdn, tm, a=None, wo=None, gfinal=None):
    rows = x.shape[0]
    row_spec = pl.BlockSpec((tm, D_MODEL), lambda i: (i, 0))
    args = [x]
    specs = [row_spec]
    if a is not None:
        args += [a, wo]
        specs += [row_spec, _resident(wo)]
    args += [g, _operand(wup), _operand(wdn)]
    specs += [_resident(g), _resident(wup), _resident(wdn)]
    if gfinal is not None:
        args.append(gfinal)
        specs.append(_resident(gfinal))
    return pl.pallas_call(
        functools.partial(_ffn_body, has_proj=a is not None, final=gfinal is not None),
        grid=(rows // tm,),
        in_specs=specs,
        out_specs=row_spec,
        out_shape=jax.ShapeDtypeStruct((rows, D_MODEL), F32),
        compiler_params=_cparams(("parallel",)),
        name="ffn",
    )(*args)


def _sgu_body(*refs, tm, emit_v):
    (x_ref, g_ref, win_ref, lng_ref, lnb_ref, wm_ref, sb_ref, wout_ref) = refs[:8]
    if emit_v:
        o_ref, vout_ref, vbuf, ubuf, sbuf = refs[8:]
    else:
        o_ref, vbuf, ubuf, sbuf = refs[8:]
        vout_ref = None
    panel = lambda p, base=0: slice(base + p * A_PANEL, base + (p + 1) * A_PANEL)

    x = x_ref[...]
    h = _rms(x, g_ref[...]).astype(BF16)

    rsum = jnp.zeros((tm, 1), F32)
    for p in range(A_NPANEL):
        v = _gelu(_dot(h, win_ref[:, panel(p, A_HALF)]))
        vbuf[p] = v
        rsum = rsum + jnp.sum(v, axis=-1, keepdims=True)
    for p in range(A_NPANEL):
        ubuf[p] = _gelu(_dot(h, win_ref[:, panel(p)]))
    mean = rsum * (1.0 / A_HALF)
    ssq = jnp.zeros((tm, 1), F32)
    for p in range(A_NPANEL):
        d = vbuf[p] - mean
        ssq = ssq + jnp.sum(d * d, axis=-1, keepdims=True)
    rstd = lax.rsqrt(ssq * (1.0 / A_HALF) + EPS)

    acc = x
    for p in range(A_NPANEL):
        vn = (vbuf[p] - mean) * rstd * lng_ref[:, panel(p)] + lnb_ref[:, panel(p)]
        if emit_v:
            vout_ref[:, p * A_PANEL:(p + 1) * A_PANEL] = vn
        vnb = vn.astype(BF16)
        for c in range(tm // A_CHUNK):
            for gg in range(A_PANEL // A_GROUP_DIM):
                grp = p * (A_PANEL // A_GROUP_DIM) + gg
                rs = slice(c * A_CHUNK, (c + 1) * A_CHUNK)
                cs = slice(gg * A_GROUP_DIM, (gg + 1) * A_GROUP_DIM)
                sbuf[rs, cs] = _dot(wm_ref[grp], vnb[rs, cs]) + sb_ref[grp]
        acc = acc + _dot((ubuf[p] * sbuf[...]).astype(BF16), wout_ref[panel(p), :])
    o_ref[...] = acc


def _sgu(x, g, win, lng, lnb, wm, sb, wout, tm, emit_v):
    rows = x.shape[0]
    row_spec = pl.BlockSpec((tm, D_MODEL), lambda i: (i, 0))
    weights = [g, win, lng, lnb, wm, sb, wout]
    out_shape = [jax.ShapeDtypeStruct((rows, D_MODEL), F32)]
    out_specs = [row_spec]
    if emit_v:
        out_shape.append(jax.ShapeDtypeStruct((rows, A_HALF), F32))
        out_specs.append(pl.BlockSpec((tm, A_HALF), lambda i: (i, 0)))
    return pl.pallas_call(
        functools.partial(_sgu_body, tm=tm, emit_v=emit_v),
        grid=(rows // tm,),
        in_specs=[row_spec] + [_resident(w) for w in weights],
        out_specs=out_specs,
        out_shape=out_shape,
        scratch_shapes=[pltpu.VMEM((A_NPANEL, tm, A_PANEL), F32),
                        pltpu.VMEM((A_NPANEL, tm, A_PANEL), F32),
                        pltpu.VMEM((tm, A_PANEL), F32)],
        compiler_params=_cparams(("parallel",)),
        name="sgu",
    )(x, *[_operand(w) for w in weights])


def _log_sigmoid(z):
    return jnp.minimum(z, 0.0) - jnp.log1p(jnp.exp(-jnp.abs(z)))


def _gla_intra(qh, kh, cum, mask_ref):
    c = qh.shape[0]
    r = lax.broadcasted_iota(jnp.int32, (c, 1), 0)
    a = jnp.zeros((c, c), F32)
    m, level = c, 0
    while m >= 2:
        half = m // 2
        pos = r & (m - 1)
        upper = pos >= half
        if m >= 2 * SUBLANES:
            pieces = []
            for b in range(c // m):
                row = b * m + half - 1
                pieces.append(jnp.broadcast_to(cum[row:row + 1, :], (m, cum.shape[1])))
            ref = pieces[0] if len(pieces) == 1 else jnp.concatenate(pieces, axis=0)
        else:
            c8 = cum.reshape(c // SUBLANES, SUBLANES, cum.shape[1])
            sub = lax.broadcasted_iota(jnp.int32, (1, SUBLANES, 1), 1)
            ref = None
            for b in reversed(range(SUBLANES // m)):
                row = jnp.broadcast_to(c8[:, b * m + half - 1:b * m + half, :], c8.shape)
                ref = row if ref is None else jnp.where(sub < (b + 1) * m, row, ref)
            ref = ref.reshape(c, cum.shape[1])
        decay = jnp.exp(jnp.where(upper, cum - ref, ref - cum))
        x = (jnp.where(upper, qh, kh) * decay).astype(BF16)
        a = a + lax.dot_general(x, x, NT_DIMS, preferred_element_type=F32) * mask_ref[level]
        m = half
        level += 1
    return a


def _gla_level_masks(c):
    rowi = lax.broadcasted_iota(jnp.int32, (c, c), 0)
    coli = lax.broadcasted_iota(jnp.int32, (c, c), 1)
    masks = []
    m = c
    while m >= 2:
        sh, half = int(math.log2(m)), m // 2
        keep = ((rowi >> sh) == (coli >> sh)) & ((rowi & (m - 1)) >= half) & ((coli & (m - 1)) < half)
        masks.append(jnp.where(keep, 1.0, 0.0))
        m = half
    return masks


def _gla_project(h, win_ref, wgl_ref, wgate_ref, bgate_ref):
    dq, dv = GLA_HEADS * GLA_DK, GLA_HEADS * GLA_DV
    q = _dot(h, win_ref[:, :dq]) * (GLA_DK ** -0.5)
    k = _dot(h, win_ref[:, dq:2 * dq])
    v = _dot(h, win_ref[:, 2 * dq:2 * dq + dv])
    g = _dot(h, win_ref[:, 2 * dq + dv:2 * dq + 2 * dv])
    gl = _dot(h, wgl_ref[...])
    z = _dot(gl.astype(BF16), wgate_ref[...]) + bgate_ref[...]
    return q, k, v, g, z


def _gla_prompt_body(x_ref, gm_ref, win_ref, wgl_ref, wgate_ref, bgate_ref,
                     gn_ref, wo_ref, o_ref, st_ref, s_scr, mask_scr):
    ci = pl.program_id(1)
    c = GLA_C

    @pl.when((pl.program_id(0) == 0) & (ci == 0))
    def _():
        for level, mask in enumerate(_gla_level_masks(c)):
            mask_scr[level] = mask

    @pl.when(ci == 0)
    def _():
        s_scr[...] = jnp.zeros_like(s_scr)

    x = x_ref[...]
    h = _rms(x, gm_ref[...]).astype(BF16)
    q, k, v, g, z = _gla_project(h, win_ref, wgl_ref, wgate_ref, bgate_ref)
    la = _log_sigmoid(z) * (1.0 / GLA_GATE_NORM)

    rowi = lax.broadcasted_iota(jnp.int32, (c, c), 0)
    coli = lax.broadcasted_iota(jnp.int32, (c, c), 1)
    tril = (coli <= rowi).astype(BF16)
    cum = functools.reduce(lambda x, y: x + y, [_dot(tril, piece) for piece in _bf16_pieces(la, 3)])

    outs = []
    for hd in range(GLA_HEADS):
        ks = slice(hd * GLA_DK, (hd + 1) * GLA_DK)
        vs = slice(hd * GLA_DV, (hd + 1) * GLA_DV)
        qh, kh, cumh, vh = q[:, ks], k[:, ks], cum[:, ks], v[:, vs]
        vhb = vh.astype(BF16)
        total = cumh[c - 1:c, :]
        st = s_scr[hd]
        o = lax.dot_general((qh * jnp.exp(cumh)).astype(BF16), st.astype(BF16), NT_DIMS,
                            preferred_element_type=F32)
        a = _gla_intra(qh, kh, cumh, mask_scr)
        o = o + _dot(a.astype(BF16), vhb)
        o = o + jnp.sum(qh * kh, axis=-1, keepdims=True) * vh
        kdec = (kh * jnp.exp(total - cumh)).astype(BF16)
        st_new = st * jnp.exp(total) + lax.dot_general(vhb, kdec, TN_DIMS,
                                                       preferred_element_type=F32)
        s_scr[hd] = st_new
        o = o * lax.rsqrt(jnp.mean(o * o, axis=-1, keepdims=True) + EPS) * gn_ref[...]
        outs.append(o)
    y = jnp.concatenate(outs, axis=1) * _silu(g)
    o_ref[...] = x + _dot(y.astype(BF16), wo_ref[...])

    @pl.when(ci == pl.num_programs(1) - 1)
    def _():
        for hd in range(GLA_HEADS):
            st_ref[0, hd] = s_scr[hd].T


def _gla_prompt(x, gm, win, wgl, wgate, bgate, gn, wo, n_seq, seq_len):
    n_chunk = seq_len // GLA_C
    weights = [gm, win, wgl, wgate, bgate, gn, wo]
    row_spec = pl.BlockSpec((GLA_C, D_MODEL), lambda b, c: (b * n_chunk + c, 0))
    return pl.pallas_call(
        _gla_prompt_body,
        grid=(n_seq, n_chunk),
        in_specs=[row_spec] + [_resident(w) for w in weights],
        out_specs=[row_spec,
                   pl.BlockSpec((1, GLA_HEADS, GLA_DK, GLA_DV), lambda b, c: (b, 0, 0, 0))],
        out_shape=[jax.ShapeDtypeStruct(x.shape, F32),
                   jax.ShapeDtypeStruct((n_seq, GLA_HEADS, GLA_DK, GLA_DV), F32)],
        scratch_shapes=[pltpu.VMEM((GLA_HEADS, GLA_DV, GLA_DK), F32),
                        pltpu.VMEM((int(math.log2(GLA_C)), GLA_C, GLA_C), F32)],
        compiler_params=_cparams(("arbitrary", "arbitrary")),
        name="gla_prompt",
    )(x, *weights)


def _gla_sample_proj_body(x_ref, gm_ref, win_ref, wgl_ref, wgate_ref, bgate_ref,
                          q_ref, k_ref, a_ref, v_ref, g_ref):
    h = _rms(x_ref[...], gm_ref[...]).astype(BF16)
    q_ref[...], k_ref[...], v_ref[...], g_ref[...], z = _gla_project(
        h, win_ref, wgl_ref, wgate_ref, bgate_ref)
    a_ref[...] = jnp.exp(_log_sigmoid(z) * (1.0 / GLA_GATE_NORM))


GLA_SEQ_PER_STEP = SUBLANES


def _gla_sample_rec_body(s_ref, qc_ref, kc_ref, ac_ref, v_ref, so_ref, o_ref):
    for j in range(GLA_SEQ_PER_STEP):
        for hd in range(GLA_HEADS):
            ks = slice(hd * GLA_DK, (hd + 1) * GLA_DK)
            vs = slice(hd * GLA_DV, (hd + 1) * GLA_DV)
            acol = ac_ref[0, ks, j:j + 1]
            kcol = kc_ref[0, ks, j:j + 1]
            qcol = qc_ref[0, ks, j:j + 1]
            vrow = v_ref[j:j + 1, vs]
            s_new = s_ref[j, hd] * acol + kcol * vrow
            so_ref[j, hd] = s_new
            o_ref[j:j + 1, vs] = jnp.sum(s_new * qcol, axis=0, keepdims=True)


def _gla_sample_out_body(x_ref, o_ref, g_ref, gn_ref, wo_ref, y_ref):
    outs = []
    for hd in range(GLA_HEADS):
        o = o_ref[:, hd * GLA_DV:(hd + 1) * GLA_DV]
        outs.append(o * lax.rsqrt(jnp.mean(o * o, axis=-1, keepdims=True) + EPS) * gn_ref[...])
    y = jnp.concatenate(outs, axis=1) * _silu(g_ref[...])
    y_ref[...] = x_ref[...] + _dot(y.astype(BF16), wo_ref[...])


def _gla_sample(x, state, gm, win, wgl, wgate, bgate, gn, wo):
    n = x.shape[0]
    dq = GLA_HEADS * GLA_DK
    weights = [gm, win, wgl, wgate, bgate]
    q, k, a, v, g = pl.pallas_call(
        _gla_sample_proj_body,
        out_shape=[jax.ShapeDtypeStruct((n, dq), F32)] * 3
        + [jax.ShapeDtypeStruct((n, D_MODEL), F32)] * 2,
        compiler_params=pltpu.CompilerParams(vmem_limit_bytes=VMEM_LIMIT),
        name="gla_sample_proj",
    )(x, *weights)

    nstep = n // GLA_SEQ_PER_STEP

    def cols(t):
        return t.reshape(nstep, GLA_SEQ_PER_STEP, dq).transpose(0, 2, 1)

    col_spec = pl.BlockSpec((1, dq, GLA_SEQ_PER_STEP), lambda i: (i, 0, 0))
    st_spec = pl.BlockSpec((GLA_SEQ_PER_STEP, GLA_HEADS, GLA_DK, GLA_DV), lambda i: (i, 0, 0, 0))
    row_spec = pl.BlockSpec((GLA_SEQ_PER_STEP, D_MODEL), lambda i: (i, 0))
    s_new, o = pl.pallas_call(
        _gla_sample_rec_body,
        grid=(nstep,),
        in_specs=[st_spec, col_spec, col_spec, col_spec, row_spec],
        out_specs=[st_spec, row_spec],
        out_shape=[jax.ShapeDtypeStruct(state.shape, F32), jax.ShapeDtypeStruct((n, D_MODEL), F32)],
        compiler_params=_cparams(("parallel",)),
        name="gla_sample_rec",
    )(state, cols(q), cols(k), cols(a), v)

    y = pl.pallas_call(
        _gla_sample_out_body,
        out_shape=jax.ShapeDtypeStruct((n, D_MODEL), F32),
        compiler_params=pltpu.CompilerParams(vmem_limit_bytes=VMEM_LIMIT),
        name="gla_sample_out",
    )(x, o, g, gn, wo)
    return y, s_new


def _t5_bucket_np(dist):
    max_exact = REL_BUCKETS // 2
    n = np.maximum(dist, 0)
    nf = np.maximum(n, max_exact).astype(np.float32)
    large = max_exact + (np.log(nf / np.float32(max_exact)) / np.float32(math.log(REL_MAX_DIST / max_exact))
                         * np.float32(REL_BUCKETS - max_exact)).astype(np.int32)
    return np.where(n < max_exact, n, np.minimum(large, REL_BUCKETS - 1)).astype(np.int32)


def _bias_tables_body(rb_ref, bd_ref, b1_ref, bdec_ref, od_ref, o1_ref, odec_ref):
    h = pl.program_id(0)
    td, t1, tdec = bd_ref[...], b1_ref[...], bdec_ref[...]
    ad = jnp.zeros(td.shape, F32)
    a1 = jnp.zeros(t1.shape, F32)
    adec = jnp.zeros(tdec.shape, F32)
    for b in range(REL_BUCKETS):
        val = rb_ref[b, h]
        ad = jnp.where(td == b, val, ad)
        a1 = jnp.where(t1 == b, val, a1)
        adec = jnp.where(tdec == b, val, adec)
    od_ref[0] = ad * LOG2E
    o1_ref[0] = a1 * LOG2E
    odec_ref[0] = adec


def _bias_tables(rel_bias, past_len):
    kc = np.arange(MOBA_BLOCK)[:, None]
    qr = np.arange(MOBA_BLOCK)[None, :]
    bd = _t5_bucket_np(qr - kc)
    b1 = _t5_bucket_np(MOBA_BLOCK + qr - kc)
    nblk = past_len // MOBA_BLOCK
    kpos = np.arange(nblk)[:, None] * MOBA_BLOCK + np.arange(MOBA_BLOCK)[None, :]
    bdec = _t5_bucket_np(past_len - kpos)
    full = lambda a: pl.BlockSpec(a.shape, lambda h: (0,) * a.ndim)
    return pl.pallas_call(
        _bias_tables_body,
        grid=(MOBA_HEADS,),
        in_specs=[pl.BlockSpec(memory_space=pltpu.SMEM), full(bd), full(b1), full(bdec)],
        out_specs=[pl.BlockSpec((1,) + bd.shape, lambda h: (h, 0, 0)),
                   pl.BlockSpec((1,) + b1.shape, lambda h: (h, 0, 0)),
                   pl.BlockSpec((1,) + bdec.shape, lambda h: (h, 0, 0))],
        out_shape=[jax.ShapeDtypeStruct((MOBA_HEADS,) + bd.shape, F32),
                   jax.ShapeDtypeStruct((MOBA_HEADS,) + b1.shape, F32),
                   jax.ShapeDtypeStruct((MOBA_HEADS,) + bdec.shape, F32)],
        compiler_params=_cparams(("parallel",)),
        name="moba_bias_tables",
    )(rel_bias, jnp.asarray(bd), jnp.asarray(b1), jnp.asarray(bdec))


def _moba_qkv_body(x_ref, g_ref, w_ref, q_ref, k_ref, v_ref, *t_refs):
    h = _rms(x_ref[...], g_ref[...]).astype(BF16)
    qkv = _dot(h, w_ref[...])
    q_ref[...] = qkv[:, :D_MODEL] * (MOBA_DH ** -0.5)
    k = qkv[:, D_MODEL:2 * D_MODEL]
    v = qkv[:, 2 * D_MODEL:]
    k_ref[...] = k
    v_ref[...] = v
    if t_refs:
        t_refs[0][0] = k.T
        t_refs[1][0] = v.T


def _moba_qkv(x, g, w, tm, seq_len=None):
    rows = x.shape[0]
    row_spec = pl.BlockSpec((tm, D_MODEL), lambda i: (i, 0))
    out_specs = [row_spec] * 3
    out_shape = [jax.ShapeDtypeStruct((rows, D_MODEL), F32)] * 3
    if seq_len is not None:
        per_seq = seq_len // tm
        t_spec = pl.BlockSpec((1, D_MODEL, tm), lambda i: (i // per_seq, 0, i % per_seq))
        out_specs += [t_spec] * 2
        out_shape += [jax.ShapeDtypeStruct((rows // seq_len, D_MODEL, seq_len), F32)] * 2
    return pl.pallas_call(
        _moba_qkv_body,
        grid=(rows // tm,),
        in_specs=[row_spec, _resident(g), _resident(w)],
        out_specs=out_specs,
        out_shape=out_shape,
        compiler_params=_cparams(("parallel",)),
        name="moba_qkv",
    )(x, g, w)


def _topk_select(gate, valid, axis):
    n = gate.shape[axis]
    idx = lax.broadcasted_iota(jnp.int32, gate.shape, axis)
    cnt = jnp.zeros(gate.shape, jnp.int32)
    for m in range(n):
        gm = lax.slice_in_dim(gate, m, m + 1, axis=axis)
        beats = (gm > gate) | ((gm == gate) & (m < idx))
        cnt = cnt + jnp.where(beats, valid(m), 0)
    return jnp.where(cnt < MOBA_TOPK, 1.0, 0.0)


def _moba_prompt_body(rb_ref, q_ref, k_ref, v_ref, bd_ref, b1_ref, o_ref,
                      kb_scr, vt_scr, km_scr):
    hg = pl.program_id(1)
    qi = pl.program_id(2)
    nblk = kb_scr.shape[0]
    blk = MOBA_BLOCK
    dh = MOBA_DH
    width = MOBA_HG * dh

    lane = lax.broadcasted_iota(jnp.int32, (1, width), 1)
    in_head = [(lane >= hh * dh) & (lane < (hh + 1) * dh) for hh in range(MOBA_HG)]

    @pl.when(qi == 0)
    def _():
        ones = jnp.ones((MOBA_ONES_ROWS, blk), BF16)
        means = []
        for n in range(nblk):
            kt = k_ref[n * blk:(n + 1) * blk, :]
            kb_scr[n] = kt.astype(BF16)
            means.append(jnp.mean(kt, axis=0, keepdims=True))
            vt = v_ref[n * blk:(n + 1) * blk, :].T.astype(BF16)
            for hh in range(MOBA_HG):
                vt_scr[n, hh, :dh, :] = vt[hh * dh:(hh + 1) * dh, :]
                vt_scr[n, hh, dh:, :] = ones
        km = jnp.concatenate(means, axis=0)
        per_head = [jnp.where(in_head[hh], km, 0.0) for hh in range(MOBA_HG)]
        pad = jnp.zeros((LANES - MOBA_HG * nblk, width), F32)
        km_t = jnp.concatenate(per_head + [pad], axis=0).T
        km_scr[0], km_scr[1] = _bf16_pieces(km_t, 2)

    q2 = q_ref[...]
    ki_idx = lax.broadcasted_iota(jnp.int32, (blk, blk), 0)
    qi_idx = lax.broadcasted_iota(jnp.int32, (blk, blk), 1)
    blk_idx = lax.broadcasted_iota(jnp.int32, (nblk, blk), 0)

    hs = range(MOBA_HG)
    qhb, far_bias, neg, tile_of = [], [], [], []
    q_hi, q_lo = _bf16_pieces(q2, 2)
    gates = (_dot(q_hi, km_scr[0]) + _dot(q_hi, km_scr[1]) + _dot(q_lo, km_scr[0])).T
    for hh in hs:
        tile_of.append(slice(hh * dh // LANES * LANES, (hh * dh // LANES + 1) * LANES))
        qhb.append((jnp.where(in_head[hh], q2, 0.0)[:, tile_of[hh]] * LOG2E).astype(BF16))
        far_bias.append(rb_ref[REL_BUCKETS - 1, MOBA_HG * hg + hh] * LOG2E)
        gate = gates[hh * nblk:(hh + 1) * nblk, :]
        keep = _topk_select(gate, lambda m: (m < qi).astype(jnp.int32), 0)
        neg.append(jnp.where((keep > 0.0) & (blk_idx < qi), 0.0, NEG))

    def scores(k0, n):
        kt = kb_scr[k0:k0 + n].reshape(n * blk, width)
        return [lax.dot_general(kt[:, tile_of[hh]], qhb[hh], NT_DIMS, preferred_element_type=F32)
                for hh in hs]

    def values(ki, p):
        return [_dot(vt_scr[ki, hh], p[hh]) for hh in hs]

    def weights(x):
        return jnp.exp2(x.astype(BF16))

    def step(c):
        units = [(c - 1, 2)] if c >= 1 else [(0, 1)]
        t = 0
        while t < c - 1:
            n = min(2, c - 1 - t)
            units.append((t, n))
            t += n
        m = [None] * MOBA_HG
        acc = [None] * MOBA_HG
        s_next = scores(*units[0])
        for ui, (k0, n) in enumerate(units):
            s = s_next
            if ui + 1 < len(units):
                s_next = scores(*units[ui + 1])
            p = [[] for _ in range(n)]
            alpha = [None] * MOBA_HG
            for hh in hs:
                sh, rows, tile_max = [], [], []
                for t in range(n):
                    kb = k0 + t
                    x = s[hh][t * blk:(t + 1) * blk, :]
                    if kb == c:
                        x, r = jnp.where(ki_idx <= qi_idx, x + bd_ref[hh], NEG), None
                    elif kb == c - 1:
                        x, r = x + b1_ref[hh], neg[hh][kb:kb + 1, :]
                    else:
                        r = neg[hh][kb:kb + 1, :] + far_bias[hh]
                    mx = jnp.max(x, axis=0, keepdims=True)
                    sh.append(x)
                    rows.append(r)
                    tile_max.append(mx if r is None else mx + r)
                m_new = functools.reduce(jnp.maximum, tile_max if m[hh] is None else [m[hh]] + tile_max)
                for t in range(n):
                    p[t].append(weights(sh[t] - (m_new if rows[t] is None else m_new - rows[t])))
                if m[hh] is not None:
                    alpha[hh] = jnp.exp2(m[hh] - m_new)
                m[hh] = m_new
            pv = [values(k0 + t, p[t]) for t in range(n)]
            for hh in hs:
                tot = functools.reduce(lambda x, y: x + y, [pv[t][hh] for t in range(n)])
                acc[hh] = tot if acc[hh] is None else alpha[hh] * acc[hh] + tot
        ot = jnp.concatenate([acc[hh][:dh, :] / acc[hh][dh:dh + 1, :] for hh in hs], axis=0)
        o_ref[...] = ot.T

    for c in range(nblk):
        pl.when(qi == c)(functools.partial(step, c))


def _moba_prompt(rel_bias, q, k, v, bias_d, bias_1, n_seq, seq_len):
    nblk = seq_len // MOBA_BLOCK
    hgrp = MOBA_HG
    width = hgrp * MOBA_DH
    q_spec = pl.BlockSpec((MOBA_BLOCK, width), lambda b, hg, qi: (b * nblk + qi, hg))
    kv_spec = pl.BlockSpec((seq_len, width), lambda b, hg, qi: (b, hg))
    bias_spec = pl.BlockSpec((hgrp, MOBA_BLOCK, MOBA_BLOCK), lambda b, hg, qi: (hg, 0, 0))
    return pl.pallas_call(
        _moba_prompt_body,
        grid=(n_seq, MOBA_HEADS // hgrp, nblk),
        in_specs=[pl.BlockSpec(memory_space=pltpu.SMEM), q_spec, kv_spec, kv_spec, bias_spec, bias_spec],
        out_specs=q_spec,
        out_shape=jax.ShapeDtypeStruct(q.shape, F32),
        scratch_shapes=[
            pltpu.VMEM((nblk, MOBA_BLOCK, width), BF16),
            pltpu.VMEM((nblk, hgrp, MOBA_DH + MOBA_ONES_ROWS, MOBA_BLOCK), BF16),
            pltpu.VMEM((2, width, LANES), BF16),
        ],
        compiler_params=_cparams(("parallel", "parallel", "arbitrary")),
        name="moba_prompt",
    )(rel_bias, q, k, v, bias_d, bias_1)


def _col_to_row(col):
    n = col.shape[0]
    r = lax.broadcasted_iota(jnp.int32, (n, n), 0)
    c = lax.broadcasted_iota(jnp.int32, (n, n), 1)
    return jnp.sum(jnp.where(r == c, jnp.broadcast_to(col, (n, n)), 0.0), axis=0, keepdims=True)


def _moba_decode_body(pt_ref, q_ref, qt_ref, kn_ref, vnt_ref, rb0_ref, bdec_ref, *refs, npg):
    del pt_ref
    ck, cv = refs[:npg], refs[npg:2 * npg]
    o_ref, qb_scr, s_scr, p_scr, t_scr = refs[2 * npg:]
    heads, dh = MOBA_HEADS, MOBA_DH
    ppb = MOBA_BLOCK // PAGE_SIZE
    nb = npg // ppb

    qt = qt_ref[0]
    for h in range(heads):
        qb_scr[h] = jnp.broadcast_to(qt[:, h:h + 1], (dh, PAGE_SIZE))

    def k_step(h, carry):
        qb = qb_scr[h]
        for pg in range(npg):
            s_scr[pg, pl.ds(h, 1), :] = jnp.sum(ck[pg][0, h] * qb, axis=0, keepdims=True)
        return carry

    lax.fori_loop(0, heads, k_step, 0)

    z, m_blk = [], []
    blk_lane = lax.broadcasted_iota(jnp.int32, (heads, nb), 1)
    gate = jnp.zeros((heads, nb), F32)
    for b in range(nb):
        zb = []
        g = jnp.zeros((heads, 1), F32)
        for j in range(ppb):
            s = s_scr[ppb * b + j]
            g = g + jnp.sum(s, axis=-1, keepdims=True)
            zb.append(s + bdec_ref[ppb * b + j])
        gate = jnp.where(blk_lane == b, g, gate)
        m_blk.append(functools.reduce(jnp.maximum, [jnp.max(t, axis=-1, keepdims=True) for t in zb]))
        z += zb
    keep = _topk_select(gate, lambda m_: 1, 1)

    l_new = jnp.sum(q_ref[0] * kn_ref[0], axis=-1, keepdims=True) + rb0_ref[...]
    m_all = l_new
    for b in range(nb):
        m_all = jnp.maximum(m_all, jnp.where(keep[:, b:b + 1] > 0.0, m_blk[b], NEG))
    w_new = jnp.exp(l_new - m_all)
    den = w_new
    for b in range(nb):
        kept = keep[:, b:b + 1] > 0.0
        for j in range(ppb):
            p = jnp.exp(jnp.where(kept, z[ppb * b + j] - m_all, NEG))
            den = den + jnp.sum(p, axis=-1, keepdims=True)
            p_scr[ppb * b + j] = p

    def v_step(h, carry):
        acc = cv[0][0, h] * p_scr[0, pl.ds(h, 1), :]
        for pg in range(1, npg):
            acc = acc + cv[pg][0, h] * p_scr[pg, pl.ds(h, 1), :]
        t_scr[h] = jnp.broadcast_to(jnp.sum(acc, axis=-1, keepdims=True), (dh, PAGE_SIZE))
        return carry

    lax.fori_loop(0, heads, v_step, 0)

    head_lane = lax.broadcasted_iota(jnp.int32, (dh, heads), 1)
    tot = jnp.zeros((dh, heads), F32)
    for h in range(heads):
        tot = jnp.where(head_lane == h, t_scr[h][:, :heads], tot)
    o_ref[0] = (tot + vnt_ref[0] * _col_to_row(w_new)) / _col_to_row(den)


def _moba_decode(page_table, q, k_new, v_new, rb0, cache_k, cache_v, bias_dec):
    n_dec, n_pages = page_table.shape
    heads, dh = MOBA_HEADS, MOBA_DH
    rows = lambda t: t.reshape(n_dec, heads, dh)
    cols = lambda t: t.reshape(n_dec, heads, dh).transpose(0, 2, 1)
    row_spec = pl.BlockSpec((1, heads, dh), lambda b, pt: (b, 0, 0))
    col_spec = pl.BlockSpec((1, dh, heads), lambda b, pt: (b, 0, 0))
    page_specs = [pl.BlockSpec((1, heads, dh, PAGE_SIZE), lambda b, pt, j=j: (pt[b, j], 0, 0, 0))
                  for j in range(n_pages)]
    grid_spec = pltpu.PrefetchScalarGridSpec(
        num_scalar_prefetch=1,
        grid=(n_dec,),
        in_specs=[row_spec, col_spec, row_spec, col_spec,
                  pl.BlockSpec((heads, 1), lambda b, pt: (0, 0)),
                  pl.BlockSpec(bias_dec.shape, lambda b, pt: (0, 0, 0))] + page_specs + page_specs,
        out_specs=col_spec,
        scratch_shapes=[pltpu.VMEM((heads, dh, PAGE_SIZE), F32),
                        pltpu.VMEM((n_pages, heads, PAGE_SIZE), F32),
                        pltpu.VMEM((n_pages, heads, PAGE_SIZE), F32),
                        pltpu.VMEM((heads, dh, PAGE_SIZE), F32)],
    )
    out = pl.pallas_call(
        functools.partial(_moba_decode_body, npg=n_pages),
        grid_spec=grid_spec,
        out_shape=jax.ShapeDtypeStruct((n_dec, dh, heads), F32),
        compiler_params=_cparams(("parallel",)),
        name="moba_decode",
    )(page_table, rows(q), cols(q), rows(k_new), cols(v_new), rb0, bias_dec,
      *([cache_k] * n_pages), *([cache_v] * n_pages))
    return out.transpose(0, 2, 1).reshape(n_dec, heads * dh)


def _prep_sgu(ln_g, ln_b, w_s, b_s):
    lng = ln_g.reshape(1, A_HALF)
    lnb = ln_b.reshape(1, A_HALF)
    causal = jnp.tril(jnp.ones((A_CHUNK, A_CHUNK), dtype=bool))
    wm_p = jnp.where(causal[None], w_s, 0.0).astype(BF16)
    sb_p = jnp.broadcast_to(b_s[:, :, None], (A_GROUPS, A_CHUNK, A_GROUP_DIM))
    eye = jnp.eye(A_CHUNK, dtype=F32)
    wm_s = (w_s[:, 0, 0][:, None, None] * eye[None]).astype(BF16)
    sb_s = jnp.broadcast_to(b_s[:, 0][:, None, None], (A_GROUPS, A_CHUNK, A_GROUP_DIM))
    return (lng, lnb), (wm_p, sb_p), (wm_s, sb_s)


def _prep_gla(w_in, w_gate, b_gate, gn, w_out):
    dq, dv = GLA_HEADS * GLA_DK, GLA_HEADS * GLA_DV
    pad = LANES - GLA_GATE_RANK
    wgl = jnp.pad(w_in[:, 2 * dq + 2 * dv:], ((0, 0), (0, pad))).astype(BF16)
    wgate = jnp.pad(w_gate, ((0, pad), (0, 0))).astype(BF16)
    return (w_in.astype(BF16), wgl, wgate, b_gate.reshape(1, dq), gn.reshape(1, GLA_DV),
            w_out.astype(BF16))


def kernel(x_prompt, x_sample, cache_k, cache_v, page_table, state_gla, norm_mix, norm_ffn, norm_final,
           w_in_a, ln_a_g, ln_a_b, w_s_a, b_s_a, w_out_a, w_in_b, w_gate_b, b_gate_b, gn_b, w_out_b,
           w_in_c, w_out_c, rel_bias, w_up, w_down):
    n_pr, l_pr, _ = x_prompt.shape
    n_dec, l_dec, _ = x_sample.shape
    assert l_dec == 1 and l_pr % MOBA_BLOCK == 0 and l_pr % GLA_C == 0
    depth = norm_mix.shape[0]
    past_len = page_table.shape[1] * PAGE_SIZE
    assert past_len % MOBA_BLOCK == 0
    tm_p, tm_s = 512, n_dec

    xp = x_prompt.reshape(n_pr * l_pr, D_MODEL)
    xs = x_sample.reshape(n_dec, D_MODEL)
    gfin = norm_final.reshape(1, D_MODEL)
    w_up_b, w_down_b = w_up.astype(BF16), w_down.astype(BF16)
    w_in_a_b, w_out_a_b = w_in_a.astype(BF16), w_out_a.astype(BF16)
    k_p, v_p, k_s, v_s, gla_p, gla_s, sgu_s = [], [], [], [], [], [], []

    for i in range(depth):
        kind, j = i % 3, i // 3
        gm = norm_mix[i].reshape(1, D_MODEL)
        gf = norm_ffn[i].reshape(1, D_MODEL)
        wup, wdn = (w_up_b, i), (w_down_b, i)
        last = gfin if i == depth - 1 else None
        proj_p = proj_s = None
        if kind == 0:
            (lng, lnb), mode_p, mode_s = _prep_sgu(ln_a_g[j], ln_a_b[j], w_s_a[j], b_s_a[j])
            win, wout = (w_in_a_b, j), (w_out_a_b, j)
            (xp,) = _sgu(xp, gm, win, lng, lnb, *mode_p, wout, tm_p, False)
            xs, v_rows = _sgu(xs, gm, win, lng, lnb, *mode_s, wout, tm_s, True)
            sgu_s.append(v_rows.reshape(n_dec, l_dec, A_HALF))
        elif kind == 1:
            wts = _prep_gla(w_in_b[j], w_gate_b[j], b_gate_b[j], gn_b[j], w_out_b[j])
            xp, sp = _gla_prompt(xp, gm, *wts, n_pr, l_pr)
            xs, ss = _gla_sample(xs, state_gla[j], gm, *wts)
            gla_p.append(sp)
            gla_s.append(ss)
        else:
            w_in = w_in_c[j].astype(BF16)
            bias_d, bias_1, bias_dec = _bias_tables(rel_bias, past_len)
            qp, kp, vp, kpt, vpt = _moba_qkv(xp, gm, w_in, tm_p, seq_len=l_pr)
            op = _moba_prompt(rel_bias, qp, kp, vp, bias_d, bias_1, n_pr, l_pr)
            qs, kn, vn = _moba_qkv(xs, gm, w_in, tm_s)
            os_ = _moba_decode(page_table, qs, kn, vn, rel_bias[0].reshape(MOBA_HEADS, 1),
                               cache_k[j].transpose(0, 2, 3, 1), cache_v[j].transpose(0, 2, 3, 1),
                               bias_dec.reshape(MOBA_HEADS, -1, PAGE_SIZE).transpose(1, 0, 2))
            wo_c = w_out_c[j].astype(BF16)
            proj_p, proj_s = (op, wo_c), (os_, wo_c)
            heads_last = lambda t: t.reshape(n_pr, MOBA_HEADS, MOBA_DH, l_pr).transpose(0, 3, 1, 2)
            k_p.append(heads_last(kpt))
            v_p.append(heads_last(vpt))
            k_s.append(kn.reshape(n_dec, l_dec, MOBA_HEADS, MOBA_DH))
            v_s.append(vn.reshape(n_dec, l_dec, MOBA_HEADS, MOBA_DH))
        if proj_p is None:
            xp = _ffn(xp, gf, wup, wdn, tm_p, gfinal=last)
            xs = _ffn(xs, gf, wup, wdn, tm_s, gfinal=last)
        else:
            xp = _ffn(xp, gf, wup, wdn, tm_p, a=proj_p[0], wo=proj_p[1], gfinal=last)
            xs = _ffn(xs, gf, wup, wdn, tm_s, a=proj_s[0], wo=proj_s[1], gfinal=last)

    return (xp.reshape(n_pr, l_pr, D_MODEL), xs.reshape(n_dec, l_dec, D_MODEL),
            jnp.stack(k_p), jnp.stack(v_p), jnp.stack(k_s), jnp.stack(v_s),
            jnp.stack(gla_p), jnp.stack(gla_s), jnp.stack(sgu_s))
```

```python
import functools
import math

import numpy as np
import jax
import jax.numpy as jnp
from jax import lax
from jax.experimental import pallas as pl
from jax.experimental.pallas import tpu as pltpu

F32 = jnp.float32
BF16 = jnp.bfloat16

D_MODEL = 1024
EPS = 1e-6
NEG = -1e30

LANES = 128
SUBLANES = 8
MXU_DIM = 256
VMEM_LIMIT = 56 * 1024 * 1024

A_CHUNK = 128
A_HALF = 3 * D_MODEL
A_GROUPS = 8
A_GROUP_DIM = A_HALF // A_GROUPS
A_PANEL = 4 * A_GROUP_DIM
A_NPANEL = A_HALF // A_PANEL

GLA_HEADS = 4
GLA_DK = 128
GLA_DV = 256
GLA_GATE_RANK = 16
GLA_GATE_NORM = 16.0
GLA_C = 256
GLA_STEP_ROWS = 2 * GLA_C

MOBA_HEADS = 16
MOBA_DH = 64
MOBA_BLOCK = 256
MOBA_TOPK = 3
MOBA_HG = MXU_DIM // MOBA_DH
MOBA_ONES_ROWS = 2 * SUBLANES
MOBA_LOOKAHEAD = 1
LOG2E = math.log2(math.e)
PAGE_SIZE = 128
REL_BUCKETS = 32
REL_MAX_DIST = 128

D_FF = 2816
FF_CHUNK = MXU_DIM
FF_NCHUNK = D_FF // FF_CHUNK

NT_DIMS = (((1,), (1,)), ((), ()))
TN_DIMS = (((0,), (0,)), ((), ()))


def _cparams(sem, **kwargs):
    return pltpu.CompilerParams(dimension_semantics=sem, vmem_limit_bytes=VMEM_LIMIT, **kwargs)


def _resident(w):
    if isinstance(w, tuple):
        arr, layer = w
        nd = arr.ndim - 1
        return pl.BlockSpec((None,) + arr.shape[1:], lambda *_: (layer,) + (0,) * nd,
                            pipeline_mode=pl.Buffered(1))
    nd = w.ndim
    return pl.BlockSpec(w.shape, lambda *_: (0,) * nd, pipeline_mode=pl.Buffered(1))


def _operand(w):
    return w[0] if isinstance(w, tuple) else w


def _rms(x, g):
    return x * lax.rsqrt(jnp.mean(x * x, axis=-1, keepdims=True) + EPS) * g


def _gelu(x):
    return 0.5 * x * (1.0 + lax.erf(x * (1.0 / math.sqrt(2.0))))


def _silu(x):
    return x * jax.nn.sigmoid(x)


def _dot(a, b):
    return jnp.dot(a, b, preferred_element_type=F32)


def _bf16_pieces(x, n):
    pieces = []
    for _ in range(n):
        p = x.astype(BF16)
        pieces.append(p)
        x = x - p.astype(F32)
    return pieces


def _ffn_body(*refs, has_proj, final):
    refs = list(refs)
    x_ref = refs.pop(0)
    a_ref = wo_ref = gf_ref = None
    if has_proj:
        a_ref = refs.pop(0)
        wo_ref = refs.pop(0)
    g_ref = refs.pop(0)
    wup_ref = refs.pop(0)
    wdn_ref = refs.pop(0)
    if final:
        gf_ref = refs.pop(0)
    o_ref = refs.pop(0)

    x = x_ref[...]
    if has_proj:
        x = x + _dot(a_ref[...].astype(BF16), wo_ref[...])
    h = _rms(x, g_ref[...]).astype(BF16)
    o_ref[...] = x

    for c in range(FF_NCHUNK):
        cols = slice(c * FF_CHUNK, (c + 1) * FF_CHUNK)
        gate = _dot(h, wup_ref[:, cols])
        up = _dot(h, wup_ref[:, D_FF + c * FF_CHUNK:D_FF + (c + 1) * FF_CHUNK])
        act = (_silu(gate) * up).astype(BF16)
        o_ref[...] += _dot(act, wdn_ref[cols, :])
    if final:
        o_ref[...] = _rms(o_ref[...], gf_ref[...])


def _ffn(x, g, wup, wdn, tm, a=None, wo=None, gfinal=None):
    rows = x.shape[0]
    row_spec = pl.BlockSpec((tm, D_MODEL), lambda i: (i, 0))
    args = [x]
    specs = [row_spec]
    if a is not None:
        args += [a, wo]
        specs += [row_spec, _resident(wo)]
    args += [g, _operand(wup), _operand(wdn)]
    specs += [_resident(g), _resident(wup), _resident(wdn)]
    if gfinal is not None:
        args.append(gfinal)
        specs.append(_resident(gfinal))
    return pl.pallas_call(
        functools.partial(_ffn_body, has_proj=a is not None, final=gfinal is not None),
        grid=(rows // tm,),
        in_specs=specs,
        out_specs=row_spec,
        out_shape=jax.ShapeDtypeStruct((rows, D_MODEL), F32),
        compiler_params=_cparams(("parallel",)),
        name="ffn",
    )(*args)


def _sgu_body(*refs, tm, emit_v):
    (x_ref, g_ref, win_ref, lng_ref, lnb_ref, wm_ref, sb_ref, wout_ref) = refs[:8]
    if emit_v:
        o_ref, vout_ref, vbuf, ubuf, sbuf = refs[8:]
    else:
        o_ref, vbuf, ubuf, sbuf = refs[8:]
        vout_ref = None
    panel = lambda p, base=0: slice(base + p * A_PANEL, base + (p + 1) * A_PANEL)

    x = x_ref[...]
    h = _rms(x, g_ref[...]).astype(BF16)

    rsum = jnp.zeros((tm, 1), F32)
    for p in range(A_NPANEL):
        v = _gelu(_dot(h, win_ref[:, panel(p, A_HALF)]))
        vbuf[p] = v
        rsum = rsum + jnp.sum(v, axis=-1, keepdims=True)
    for p in range(A_NPANEL):
        ubuf[p] = _gelu(_dot(h, win_ref[:, panel(p)]))
    mean = rsum * (1.0 / A_HALF)
    ssq = jnp.zeros((tm, 1), F32)
    for p in range(A_NPANEL):
        d = vbuf[p] - mean
        ssq = ssq + jnp.sum(d * d, axis=-1, keepdims=True)
    rstd = lax.rsqrt(ssq * (1.0 / A_HALF) + EPS)

    acc = x
    for p in range(A_NPANEL):
        vn = (vbuf[p] - mean) * rstd * lng_ref[:, panel(p)] + lnb_ref[:, panel(p)]
        if emit_v:
            vout_ref[:, p * A_PANEL:(p + 1) * A_PANEL] = vn
        vnb = vn.astype(BF16)
        for c in range(tm // A_CHUNK):
            for gg in range(A_PANEL // A_GROUP_DIM):
                grp = p * (A_PANEL // A_GROUP_DIM) + gg
                rs = slice(c * A_CHUNK, (c + 1) * A_CHUNK)
                cs = slice(gg * A_GROUP_DIM, (gg + 1) * A_GROUP_DIM)
                sbuf[rs, cs] = _dot(wm_ref[grp], vnb[rs, cs]) + sb_ref[grp]
        acc = acc + _dot((ubuf[p] * sbuf[...]).astype(BF16), wout_ref[panel(p), :])
    o_ref[...] = acc


def _sgu(x, g, win, lng, lnb, wm, sb, wout, tm, emit_v):
    rows = x.shape[0]
    row_spec = pl.BlockSpec((tm, D_MODEL), lambda i: (i, 0))
    weights = [g, win, lng, lnb, wm, sb, wout]
    out_shape = [jax.ShapeDtypeStruct((rows, D_MODEL), F32)]
    out_specs = [row_spec]
    if emit_v:
        out_shape.append(jax.ShapeDtypeStruct((rows, A_HALF), F32))
        out_specs.append(pl.BlockSpec((tm, A_HALF), lambda i: (i, 0)))
    return pl.pallas_call(
        functools.partial(_sgu_body, tm=tm, emit_v=emit_v),
        grid=(rows // tm,),
        in_specs=[row_spec] + [_resident(w) for w in weights],
        out_specs=out_specs,
        out_shape=out_shape,
        scratch_shapes=[pltpu.VMEM((A_NPANEL, tm, A_PANEL), F32),
                        pltpu.VMEM((A_NPANEL, tm, A_PANEL), F32),
                        pltpu.VMEM((tm, A_PANEL), F32)],
        compiler_params=_cparams(("parallel",)),
        name="sgu",
    )(x, *[_operand(w) for w in weights])


def _log_sigmoid(z):
    return jnp.minimum(z, 0.0) - jnp.log1p(jnp.exp(-jnp.abs(z)))


def _gla_intra(qh, kh, cum, mask_ref):
    c = qh.shape[0]
    r = lax.broadcasted_iota(jnp.int32, (c, 1), 0)
    a = jnp.zeros((c, c), F32)
    m, level = c, 0
    while m >= 2:
        half = m // 2
        pos = r & (m - 1)
        upper = pos >= half
        if m >= 2 * SUBLANES:
            pieces = []
            for b in range(c // m):
                row = b * m + half - 1
                pieces.append(jnp.broadcast_to(cum[row:row + 1, :], (m, cum.shape[1])))
            ref = pieces[0] if len(pieces) == 1 else jnp.concatenate(pieces, axis=0)
        else:
            c8 = cum.reshape(c // SUBLANES, SUBLANES, cum.shape[1])
            sub = lax.broadcasted_iota(jnp.int32, (1, SUBLANES, 1), 1)
            ref = None
            for b in reversed(range(SUBLANES // m)):
                row = jnp.broadcast_to(c8[:, b * m + half - 1:b * m + half, :], c8.shape)
                ref = row if ref is None else jnp.where(sub < (b + 1) * m, row, ref)
            ref = ref.reshape(c, cum.shape[1])
        decay = jnp.exp(jnp.where(upper, cum - ref, ref - cum))
        x = (jnp.where(upper, qh, kh) * decay).astype(BF16)
        a = a + lax.dot_general(x, x, NT_DIMS, preferred_element_type=F32) * mask_ref[level]
        m = half
        level += 1
    return a


def _gla_level_masks(c):
    rowi = lax.broadcasted_iota(jnp.int32, (c, c), 0)
    coli = lax.broadcasted_iota(jnp.int32, (c, c), 1)
    masks = []
    m = c
    while m >= 2:
        sh, half = int(math.log2(m)), m // 2
        keep = ((rowi >> sh) == (coli >> sh)) & ((rowi & (m - 1)) >= half) & ((coli & (m - 1)) < half)
        masks.append(jnp.where(keep, 1.0, 0.0))
        m = half
    return masks


def _gla_project(h, win_ref, wgl_ref, wgate_ref, bgate_ref):
    dq, dv = GLA_HEADS * GLA_DK, GLA_HEADS * GLA_DV
    q = _dot(h, win_ref[:, :dq]) * (GLA_DK ** -0.5)
    k = _dot(h, win_ref[:, dq:2 * dq])
    v = _dot(h, win_ref[:, 2 * dq:2 * dq + dv])
    g = _dot(h, win_ref[:, 2 * dq + dv:2 * dq + 2 * dv])
    gl = _dot(h, wgl_ref[...])
    z = _dot(gl.astype(BF16), wgate_ref[...]) + bgate_ref[...]
    return q, k, v, g, z


def _gla_prompt_body(x_ref, gm_ref, win_ref, wgl_ref, wgate_ref, bgate_ref,
                     gn_ref, wo_ref, o_ref, st_ref, s_scr, mask_scr):
    ci = pl.program_id(1)
    c = GLA_C

    @pl.when((pl.program_id(0) == 0) & (ci == 0))
    def _():
        for level, mask in enumerate(_gla_level_masks(c)):
            mask_scr[level] = mask

    @pl.when(ci == 0)
    def _():
        s_scr[...] = jnp.zeros_like(s_scr)

    x = x_ref[...]
    h = _rms(x, gm_ref[...]).astype(BF16)
    q, k, v, g, z = _gla_project(h, win_ref, wgl_ref, wgate_ref, bgate_ref)
    la = _log_sigmoid(z) * (1.0 / GLA_GATE_NORM)

    rowi = lax.broadcasted_iota(jnp.int32, (c, c), 0)
    coli = lax.broadcasted_iota(jnp.int32, (c, c), 1)
    tril = (coli <= rowi).astype(BF16)

    ys = []
    for sc in range(x.shape[0] // c):
        rs = slice(sc * c, (sc + 1) * c)
        cum = functools.reduce(lambda x, y: x + y,
                               [_dot(tril, piece) for piece in _bf16_pieces(la[rs], 3)])
        outs = []
        for hd in range(GLA_HEADS):
            ks = slice(hd * GLA_DK, (hd + 1) * GLA_DK)
            vs = slice(hd * GLA_DV, (hd + 1) * GLA_DV)
            qh, kh, cumh, vh = q[rs, ks], k[rs, ks], cum[:, ks], v[rs, vs]
            vhb = vh.astype(BF16)
            total = cumh[c - 1:c, :]
            st = s_scr[hd]
            o = lax.dot_general((qh * jnp.exp(cumh)).astype(BF16), st.astype(BF16), NT_DIMS,
                                preferred_element_type=F32)
            a = _gla_intra(qh, kh, cumh, mask_scr)
            o = o + _dot(a.astype(BF16), vhb)
            o = o + jnp.sum(qh * kh, axis=-1, keepdims=True) * vh
            kdec = (kh * jnp.exp(total - cumh)).astype(BF16)
            st_new = st * jnp.exp(total) + lax.dot_general(vhb, kdec, TN_DIMS,
                                                           preferred_element_type=F32)
            s_scr[hd] = st_new
            o = o * lax.rsqrt(jnp.mean(o * o, axis=-1, keepdims=True) + EPS) * gn_ref[...]
            outs.append(o)
        ys.append(jnp.concatenate(outs, axis=1))
    y = jnp.concatenate(ys, axis=0) * _silu(g)
    o_ref[...] = x + _dot(y.astype(BF16), wo_ref[...])

    @pl.when(ci == pl.num_programs(1) - 1)
    def _():
        for hd in range(GLA_HEADS):
            st_ref[0, hd] = s_scr[hd].T


def _gla_prompt(x, gm, win, wgl, wgate, bgate, gn, wo, n_seq, seq_len):
    n_chunk = seq_len // GLA_STEP_ROWS
    weights = [gm, win, wgl, wgate, bgate, gn, wo]
    row_spec = pl.BlockSpec((GLA_STEP_ROWS, D_MODEL), lambda b, c: (b * n_chunk + c, 0))
    return pl.pallas_call(
        _gla_prompt_body,
        grid=(n_seq, n_chunk),
        in_specs=[row_spec] + [_resident(w) for w in weights],
        out_specs=[row_spec,
                   pl.BlockSpec((1, GLA_HEADS, GLA_DK, GLA_DV), lambda b, c: (b, 0, 0, 0))],
        out_shape=[jax.ShapeDtypeStruct(x.shape, F32),
                   jax.ShapeDtypeStruct((n_seq, GLA_HEADS, GLA_DK, GLA_DV), F32)],
        scratch_shapes=[pltpu.VMEM((GLA_HEADS, GLA_DV, GLA_DK), F32),
                        pltpu.VMEM((int(math.log2(GLA_C)), GLA_C, GLA_C), F32)],
        compiler_params=_cparams(("arbitrary", "arbitrary")),
        name="gla_prompt",
    )(x, *weights)


def _gla_sample_proj_body(x_ref, gm_ref, win_ref, wgl_ref, wgate_ref, bgate_ref,
                          q_ref, k_ref, a_ref, v_ref, g_ref):
    h = _rms(x_ref[...], gm_ref[...]).astype(BF16)
    q_ref[...], k_ref[...], v_ref[...], g_ref[...], z = _gla_project(
        h, win_ref, wgl_ref, wgate_ref, bgate_ref)
    a_ref[...] = jnp.exp(_log_sigmoid(z) * (1.0 / GLA_GATE_NORM))


GLA_SEQ_PER_STEP = SUBLANES


def _gla_sample_rec_body(s_ref, qc_ref, kc_ref, ac_ref, v_ref, so_ref, o_ref):
    for j in range(GLA_SEQ_PER_STEP):
        for hd in range(GLA_HEADS):
            ks = slice(hd * GLA_DK, (hd + 1) * GLA_DK)
            vs = slice(hd * GLA_DV, (hd + 1) * GLA_DV)
            acol = ac_ref[0, ks, j:j + 1]
            kcol = kc_ref[0, ks, j:j + 1]
            qcol = qc_ref[0, ks, j:j + 1]
            vrow = v_ref[j:j + 1, vs]
            s_new = s_ref[j, hd] * acol + kcol * vrow
            so_ref[j, hd] = s_new
            o_ref[j:j + 1, vs] = jnp.sum(s_new * qcol, axis=0, keepdims=True)


def _gla_sample_out_body(x_ref, o_ref, g_ref, gn_ref, wo_ref, y_ref):
    outs = []
    for hd in range(GLA_HEADS):
        o = o_ref[:, hd * GLA_DV:(hd + 1) * GLA_DV]
        outs.append(o * lax.rsqrt(jnp.mean(o * o, axis=-1, keepdims=True) + EPS) * gn_ref[...])
    y = jnp.concatenate(outs, axis=1) * _silu(g_ref[...])
    y_ref[...] = x_ref[...] + _dot(y.astype(BF16), wo_ref[...])


def _gla_sample(x, state, gm, win, wgl, wgate, bgate, gn, wo):
    n = x.shape[0]
    dq = GLA_HEADS * GLA_DK
    weights = [gm, win, wgl, wgate, bgate]
    q, k, a, v, g = pl.pallas_call(
        _gla_sample_proj_body,
        out_shape=[jax.ShapeDtypeStruct((n, dq), F32)] * 3
        + [jax.ShapeDtypeStruct((n, D_MODEL), F32)] * 2,
        compiler_params=pltpu.CompilerParams(vmem_limit_bytes=VMEM_LIMIT),
        name="gla_sample_proj",
    )(x, *weights)

    nstep = n // GLA_SEQ_PER_STEP

    def cols(t):
        return t.reshape(nstep, GLA_SEQ_PER_STEP, dq).transpose(0, 2, 1)

    col_spec = pl.BlockSpec((1, dq, GLA_SEQ_PER_STEP), lambda i: (i, 0, 0))
    st_spec = pl.BlockSpec((GLA_SEQ_PER_STEP, GLA_HEADS, GLA_DK, GLA_DV), lambda i: (i, 0, 0, 0))
    row_spec = pl.BlockSpec((GLA_SEQ_PER_STEP, D_MODEL), lambda i: (i, 0))
    s_new, o = pl.pallas_call(
        _gla_sample_rec_body,
        grid=(nstep,),
        in_specs=[st_spec, col_spec, col_spec, col_spec, row_spec],
        out_specs=[st_spec, row_spec],
        out_shape=[jax.ShapeDtypeStruct(state.shape, F32), jax.ShapeDtypeStruct((n, D_MODEL), F32)],
        compiler_params=_cparams(("parallel",)),
        name="gla_sample_rec",
    )(state, cols(q), cols(k), cols(a), v)

    y = pl.pallas_call(
        _gla_sample_out_body,
        out_shape=jax.ShapeDtypeStruct((n, D_MODEL), F32),
        compiler_params=pltpu.CompilerParams(vmem_limit_bytes=VMEM_LIMIT),
        name="gla_sample_out",
    )(x, o, g, gn, wo)
    return y, s_new


def _t5_bucket_np(dist):
    max_exact = REL_BUCKETS // 2
    n = np.maximum(dist, 0)
    nf = np.maximum(n, max_exact).astype(np.float32)
    large = max_exact + (np.log(nf / np.float32(max_exact)) / np.float32(math.log(REL_MAX_DIST / max_exact))
                         * np.float32(REL_BUCKETS - max_exact)).astype(np.int32)
    return np.where(n < max_exact, n, np.minimum(large, REL_BUCKETS - 1)).astype(np.int32)


def _bias_tables_body(rb_ref, bd_ref, b1_ref, bdec_ref, od_ref, o1_ref, odec_ref):
    h = pl.program_id(0)
    td, t1, tdec = bd_ref[...], b1_ref[...], bdec_ref[...]
    ad = jnp.zeros(td.shape, F32)
    a1 = jnp.zeros(t1.shape, F32)
    adec = jnp.zeros(tdec.shape, F32)
    for b in range(REL_BUCKETS):
        val = rb_ref[b, h]
        ad = jnp.where(td == b, val, ad)
        a1 = jnp.where(t1 == b, val, a1)
        adec = jnp.where(tdec == b, val, adec)
    od_ref[0] = ad * LOG2E
    o1_ref[0] = a1 * LOG2E
    odec_ref[0] = adec


def _bias_tables(rel_bias, past_len):
    kc = np.arange(MOBA_BLOCK)[:, None]
    qr = np.arange(MOBA_BLOCK)[None, :]
    bd = _t5_bucket_np(qr - kc)
    b1 = _t5_bucket_np(MOBA_BLOCK + qr - kc)
    nblk = past_len // MOBA_BLOCK
    kpos = np.arange(nblk)[:, None] * MOBA_BLOCK + np.arange(MOBA_BLOCK)[None, :]
    bdec = _t5_bucket_np(past_len - kpos)
    full = lambda a: pl.BlockSpec(a.shape, lambda h: (0,) * a.ndim)
    return pl.pallas_call(
        _bias_tables_body,
        grid=(MOBA_HEADS,),
        in_specs=[pl.BlockSpec(memory_space=pltpu.SMEM), full(bd), full(b1), full(bdec)],
        out_specs=[pl.BlockSpec((1,) + bd.shape, lambda h: (h, 0, 0)),
                   pl.BlockSpec((1,) + b1.shape, lambda h: (h, 0, 0)),
                   pl.BlockSpec((1,) + bdec.shape, lambda h: (h, 0, 0))],
        out_shape=[jax.ShapeDtypeStruct((MOBA_HEADS,) + bd.shape, F32),
                   jax.ShapeDtypeStruct((MOBA_HEADS,) + b1.shape, F32),
                   jax.ShapeDtypeStruct((MOBA_HEADS,) + bdec.shape, F32)],
        compiler_params=_cparams(("parallel",)),
        name="moba_bias_tables",
    )(rel_bias, jnp.asarray(bd), jnp.asarray(b1), jnp.asarray(bdec))


def _moba_qkv_body(x_ref, g_ref, w_ref, q_ref, k_ref, v_ref, *t_refs):
    h = _rms(x_ref[...], g_ref[...]).astype(BF16)
    qkv = _dot(h, w_ref[...])
    q_ref[...] = qkv[:, :D_MODEL] * (MOBA_DH ** -0.5)
    k = qkv[:, D_MODEL:2 * D_MODEL]
    v = qkv[:, 2 * D_MODEL:]
    k_ref[...] = k
    v_ref[...] = v
    if t_refs:
        t_refs[0][0] = k.T
        t_refs[1][0] = v.T


def _moba_qkv(x, g, w, tm, seq_len=None):
    rows = x.shape[0]
    row_spec = pl.BlockSpec((tm, D_MODEL), lambda i: (i, 0))
    out_specs = [row_spec] * 3
    out_shape = [jax.ShapeDtypeStruct((rows, D_MODEL), F32)] * 3
    if seq_len is not None:
        per_seq = seq_len // tm
        t_spec = pl.BlockSpec((1, D_MODEL, tm), lambda i: (i // per_seq, 0, i % per_seq))
        out_specs += [t_spec] * 2
        out_shape += [jax.ShapeDtypeStruct((rows // seq_len, D_MODEL, seq_len), F32)] * 2
    return pl.pallas_call(
        _moba_qkv_body,
        grid=(rows // tm,),
        in_specs=[row_spec, _resident(g), _resident(w)],
        out_specs=out_specs,
        out_shape=out_shape,
        compiler_params=_cparams(("parallel",)),
        name="moba_qkv",
    )(x, g, w)


def _topk_select(gate, valid, axis):
    n = gate.shape[axis]
    idx = lax.broadcasted_iota(jnp.int32, gate.shape, axis)
    cnt = jnp.zeros(gate.shape, jnp.int32)
    for m in range(n):
        gm = lax.slice_in_dim(gate, m, m + 1, axis=axis)
        beats = (gm > gate) | ((gm == gate) & (m < idx))
        cnt = cnt + jnp.where(beats, valid(m), 0)
    return jnp.where(cnt < MOBA_TOPK, 1.0, 0.0)


def _moba_prompt_body(rb_ref, q_ref, k_ref, v_ref, bd_ref, b1_ref, o_ref,
                      kb_scr, vt_scr, km_scr):
    hg = pl.program_id(1)
    qi = pl.program_id(2)
    nblk = kb_scr.shape[0]
    blk = MOBA_BLOCK
    dh = MOBA_DH
    width = MOBA_HG * dh

    lane = lax.broadcasted_iota(jnp.int32, (1, width), 1)
    in_head = [(lane >= hh * dh) & (lane < (hh + 1) * dh) for hh in range(MOBA_HG)]

    @pl.when(qi == 0)
    def _():
        ones = jnp.ones((MOBA_ONES_ROWS, blk), BF16)
        means = []
        for n in range(nblk):
            kt = k_ref[n * blk:(n + 1) * blk, :]
            kb_scr[n] = kt.astype(BF16)
            means.append(jnp.mean(kt, axis=0, keepdims=True))
            vt = v_ref[n * blk:(n + 1) * blk, :].T.astype(BF16)
            for hh in range(MOBA_HG):
                vt_scr[n, hh, :dh, :] = vt[hh * dh:(hh + 1) * dh, :]
                vt_scr[n, hh, dh:, :] = ones
        km = jnp.concatenate(means, axis=0)
        per_head = [jnp.where(in_head[hh], km, 0.0) for hh in range(MOBA_HG)]
        pad = jnp.zeros((LANES - MOBA_HG * nblk, width), F32)
        km_t = jnp.concatenate(per_head + [pad], axis=0).T
        km_scr[0], km_scr[1] = _bf16_pieces(km_t, 2)

    q2 = q_ref[...]
    ki_idx = lax.broadcasted_iota(jnp.int32, (blk, blk), 0)
    qi_idx = lax.broadcasted_iota(jnp.int32, (blk, blk), 1)
    blk_idx = lax.broadcasted_iota(jnp.int32, (nblk, blk), 0)

    hs = range(MOBA_HG)
    qhb, far_bias, neg, tile_of = [], [], [], []
    q_hi, q_lo = _bf16_pieces(q2, 2)
    gates = (_dot(q_hi, km_scr[0]) + _dot(q_hi, km_scr[1]) + _dot(q_lo, km_scr[0])).T
    for hh in hs:
        tile_of.append(slice(hh * dh // LANES * LANES, (hh * dh // LANES + 1) * LANES))
        qhb.append((jnp.where(in_head[hh], q2, 0.0)[:, tile_of[hh]] * LOG2E).astype(BF16))
        far_bias.append(rb_ref[REL_BUCKETS - 1, MOBA_HG * hg + hh] * LOG2E)
        gate = gates[hh * nblk:(hh + 1) * nblk, :]
        keep = _topk_select(gate, lambda m: (m < qi).astype(jnp.int32), 0)
        neg.append(jnp.where((keep > 0.0) & (blk_idx < qi), 0.0, NEG))

    def scores(k0, n):
        kt = kb_scr[k0:k0 + n].reshape(n * blk, width)
        return [lax.dot_general(kt[:, tile_of[hh]], qhb[hh], NT_DIMS, preferred_element_type=F32)
                for hh in hs]

    def values(ki, p):
        return [_dot(vt_scr[ki, hh], p[hh]) for hh in hs]

    def weights(x):
        return jnp.exp2(x.astype(BF16))

    def step(c):
        units = [(c - 1, 2)] if c >= 1 else [(0, 1)]
        t = 0
        while t < c - 1:
            n = min(2, c - 1 - t)
            units.append((t, n))
            t += n
        m = [None] * MOBA_HG
        acc = [None] * MOBA_HG
        ahead = [scores(*u) for u in units[:MOBA_LOOKAHEAD]]
        for ui, (k0, n) in enumerate(units):
            s = ahead.pop(0)
            if ui + MOBA_LOOKAHEAD < len(units):
                ahead.append(scores(*units[ui + MOBA_LOOKAHEAD]))
            p = [[] for _ in range(n)]
            alpha = [None] * MOBA_HG
            for hh in hs:
                sh, rows, tile_max = [], [], []
                for t in range(n):
                    kb = k0 + t
                    x = s[hh][t * blk:(t + 1) * blk, :]
                    if kb == c:
                        x, r = jnp.where(ki_idx <= qi_idx, x + bd_ref[hh], NEG), None
                    elif kb == c - 1:
                        x, r = x + b1_ref[hh], neg[hh][kb:kb + 1, :]
                    else:
                        r = neg[hh][kb:kb + 1, :] + far_bias[hh]
                    mx = jnp.max(x, axis=0, keepdims=True)
                    sh.append(x)
                    rows.append(r)
                    tile_max.append(mx if r is None else mx + r)
                m_new = functools.reduce(jnp.maximum, tile_max if m[hh] is None else [m[hh]] + tile_max)
                for t in range(n):
                    p[t].append(weights(sh[t] - (m_new if rows[t] is None else m_new - rows[t])))
                if m[hh] is not None:
                    alpha[hh] = jnp.exp2(m[hh] - m_new)
                m[hh] = m_new
            pv = [values(k0 + t, p[t]) for t in range(n)]
            for hh in hs:
                tot = functools.reduce(lambda x, y: x + y, [pv[t][hh] for t in range(n)])
                acc[hh] = tot if acc[hh] is None else alpha[hh] * acc[hh] + tot
        ot = jnp.concatenate([acc[hh][:dh, :] / acc[hh][dh:dh + 1, :] for hh in hs], axis=0)
        o_ref[...] = ot.T

    for c in range(nblk):
        pl.when(qi == c)(functools.partial(step, c))


def _moba_prompt(rel_bias, q, k, v, bias_d, bias_1, n_seq, seq_len):
    nblk = seq_len // MOBA_BLOCK
    hgrp = MOBA_HG
    width = hgrp * MOBA_DH
    q_spec = pl.BlockSpec((MOBA_BLOCK, width), lambda b, hg, qi: (b * nblk + qi, hg))
    kv_spec = pl.BlockSpec((seq_len, width), lambda b, hg, qi: (b, hg))
    bias_spec = pl.BlockSpec((hgrp, MOBA_BLOCK, MOBA_BLOCK), lambda b, hg, qi: (hg, 0, 0))
    return pl.pallas_call(
        _moba_prompt_body,
        grid=(n_seq, MOBA_HEADS // hgrp, nblk),
        in_specs=[pl.BlockSpec(memory_space=pltpu.SMEM), q_spec, kv_spec, kv_spec, bias_spec, bias_spec],
        out_specs=q_spec,
        out_shape=jax.ShapeDtypeStruct(q.shape, F32),
        scratch_shapes=[
            pltpu.VMEM((nblk, MOBA_BLOCK, width), BF16),
            pltpu.VMEM((nblk, hgrp, MOBA_DH + MOBA_ONES_ROWS, MOBA_BLOCK), BF16),
            pltpu.VMEM((2, width, LANES), BF16),
        ],
        compiler_params=_cparams(("parallel", "parallel", "arbitrary")),
        name="moba_prompt",
    )(rel_bias, q, k, v, bias_d, bias_1)


def _col_to_row(col):
    n = col.shape[0]
    r = lax.broadcasted_iota(jnp.int32, (n, n), 0)
    c = lax.broadcasted_iota(jnp.int32, (n, n), 1)
    return jnp.sum(jnp.where(r == c, jnp.broadcast_to(col, (n, n)), 0.0), axis=0, keepdims=True)


def _moba_decode_body(pt_ref, q_ref, qt_ref, kn_ref, vnt_ref, rb0_ref, bdec_ref, *refs, npg):
    del pt_ref
    ck, cv = refs[:npg], refs[npg:2 * npg]
    o_ref, qb_scr, s_scr, p_scr, t_scr = refs[2 * npg:]
    heads, dh = MOBA_HEADS, MOBA_DH
    ppb = MOBA_BLOCK // PAGE_SIZE
    nb = npg // ppb

    qt = qt_ref[0]
    for h in range(heads):
        qb_scr[h] = jnp.broadcast_to(qt[:, h:h + 1], (dh, PAGE_SIZE))

    def k_step(h, carry):
        qb = qb_scr[h]
        for pg in range(npg):
            s_scr[pg, pl.ds(h, 1), :] = jnp.sum(ck[pg][0, h] * qb, axis=0, keepdims=True)
        return carry

    lax.fori_loop(0, heads, k_step, 0)

    z, m_blk = [], []
    blk_lane = lax.broadcasted_iota(jnp.int32, (heads, nb), 1)
    gate = jnp.zeros((heads, nb), F32)
    for b in range(nb):
        zb = []
        g = jnp.zeros((heads, 1), F32)
        for j in range(ppb):
            s = s_scr[ppb * b + j]
            g = g + jnp.sum(s, axis=-1, keepdims=True)
            zb.append(s + bdec_ref[ppb * b + j])
        gate = jnp.where(blk_lane == b, g, gate)
        m_blk.append(functools.reduce(jnp.maximum, [jnp.max(t, axis=-1, keepdims=True) for t in zb]))
        z += zb
    keep = _topk_select(gate, lambda m_: 1, 1)

    l_new = jnp.sum(q_ref[0] * kn_ref[0], axis=-1, keepdims=True) + rb0_ref[...]
    m_all = l_new
    for b in range(nb):
        m_all = jnp.maximum(m_all, jnp.where(keep[:, b:b + 1] > 0.0, m_blk[b], NEG))
    w_new = jnp.exp(l_new - m_all)
    den = w_new
    for b in range(nb):
        kept = keep[:, b:b + 1] > 0.0
        for j in range(ppb):
            p = jnp.exp(jnp.where(kept, z[ppb * b + j] - m_all, NEG))
            den = den + jnp.sum(p, axis=-1, keepdims=True)
            p_scr[ppb * b + j] = p

    def v_step(h, carry):
        acc = cv[0][0, h] * p_scr[0, pl.ds(h, 1), :]
        for pg in range(1, npg):
            acc = acc + cv[pg][0, h] * p_scr[pg, pl.ds(h, 1), :]
        t_scr[h] = jnp.broadcast_to(jnp.sum(acc, axis=-1, keepdims=True), (dh, PAGE_SIZE))
        return carry

    lax.fori_loop(0, heads, v_step, 0)

    head_lane = lax.broadcasted_iota(jnp.int32, (dh, heads), 1)
    tot = jnp.zeros((dh, heads), F32)
    for h in range(heads):
        tot = jnp.where(head_lane == h, t_scr[h][:, :heads], tot)
    o_ref[0] = (tot + vnt_ref[0] * _col_to_row(w_new)) / _col_to_row(den)


def _moba_decode(page_table, q, k_new, v_new, rb0, cache_k, cache_v, bias_dec):
    n_dec, n_pages = page_table.shape
    heads, dh = MOBA_HEADS, MOBA_DH
    rows = lambda t: t.reshape(n_dec, heads, dh)
    cols = lambda t: t.reshape(n_dec, heads, dh).transpose(0, 2, 1)
    row_spec = pl.BlockSpec((1, heads, dh), lambda b, pt: (b, 0, 0))
    col_spec = pl.BlockSpec((1, dh, heads), lambda b, pt: (b, 0, 0))
    page_specs = [pl.BlockSpec((1, heads, dh, PAGE_SIZE), lambda b, pt, j=j: (pt[b, j], 0, 0, 0))
                  for j in range(n_pages)]
    grid_spec = pltpu.PrefetchScalarGridSpec(
        num_scalar_prefetch=1,
        grid=(n_dec,),
        in_specs=[row_spec, col_spec, row_spec, col_spec,
                  pl.BlockSpec((heads, 1), lambda b, pt: (0, 0)),
                  pl.BlockSpec(bias_dec.shape, lambda b, pt: (0, 0, 0))] + page_specs + page_specs,
        out_specs=col_spec,
        scratch_shapes=[pltpu.VMEM((heads, dh, PAGE_SIZE), F32),
                        pltpu.VMEM((n_pages, heads, PAGE_SIZE), F32),
                        pltpu.VMEM((n_pages, heads, PAGE_SIZE), F32),
                        pltpu.VMEM((heads, dh, PAGE_SIZE), F32)],
    )
    out = pl.pallas_call(
        functools.partial(_moba_decode_body, npg=n_pages),
        grid_spec=grid_spec,
        out_shape=jax.ShapeDtypeStruct((n_dec, dh, heads), F32),
        compiler_params=_cparams(("parallel",)),
        name="moba_decode",
    )(page_table, rows(q), cols(q), rows(k_new), cols(v_new), rb0, bias_dec,
      *([cache_k] * n_pages), *([cache_v] * n_pages))
    return out.transpose(0, 2, 1).reshape(n_dec, heads * dh)


def _prep_sgu(ln_g, ln_b, w_s, b_s):
    lng = ln_g.reshape(1, A_HALF)
    lnb = ln_b.reshape(1, A_HALF)
    causal = jnp.tril(jnp.ones((A_CHUNK, A_CHUNK), dtype=bool))
    wm_p = jnp.where(causal[None], w_s, 0.0).astype(BF16)
    sb_p = jnp.broadcast_to(b_s[:, :, None], (A_GROUPS, A_CHUNK, A_GROUP_DIM))
    eye = jnp.eye(A_CHUNK, dtype=F32)
    wm_s = (w_s[:, 0, 0][:, None, None] * eye[None]).astype(BF16)
    sb_s = jnp.broadcast_to(b_s[:, 0][:, None, None], (A_GROUPS, A_CHUNK, A_GROUP_DIM))
    return (lng, lnb), (wm_p, sb_p), (wm_s, sb_s)


def _prep_gla(w_in, w_gate, b_gate, gn, w_out):
    dq, dv = GLA_HEADS * GLA_DK, GLA_HEADS * GLA_DV
    pad = LANES - GLA_GATE_RANK
    wgl = jnp.pad(w_in[:, 2 * dq + 2 * dv:], ((0, 0), (0, pad))).astype(BF16)
    wgate = jnp.pad(w_gate, ((0, pad), (0, 0))).astype(BF16)
    return (w_in.astype(BF16), wgl, wgate, b_gate.reshape(1, dq), gn.reshape(1, GLA_DV),
            w_out.astype(BF16))


def kernel(x_prompt, x_sample, cache_k, cache_v, page_table, state_gla, norm_mix, norm_ffn, norm_final,
           w_in_a, ln_a_g, ln_a_b, w_s_a, b_s_a, w_out_a, w_in_b, w_gate_b, b_gate_b, gn_b, w_out_b,
           w_in_c, w_out_c, rel_bias, w_up, w_down):
    n_pr, l_pr, _ = x_prompt.shape
    n_dec, l_dec, _ = x_sample.shape
    assert l_dec == 1 and l_pr % MOBA_BLOCK == 0 and l_pr % GLA_STEP_ROWS == 0
    depth = norm_mix.shape[0]
    past_len = page_table.shape[1] * PAGE_SIZE
    assert past_len % MOBA_BLOCK == 0
    tm_p, tm_s = 512, n_dec

    xp = x_prompt.reshape(n_pr * l_pr, D_MODEL)
    xs = x_sample.reshape(n_dec, D_MODEL)
    gfin = norm_final.reshape(1, D_MODEL)
    w_up_b, w_down_b = w_up.astype(BF16), w_down.astype(BF16)
    w_in_a_b, w_out_a_b = w_in_a.astype(BF16), w_out_a.astype(BF16)
    k_p, v_p, k_s, v_s, gla_p, gla_s, sgu_s = [], [], [], [], [], [], []

    for i in range(depth):
        kind, j = i % 3, i // 3
        gm = norm_mix[i].reshape(1, D_MODEL)
        gf = norm_ffn[i].reshape(1, D_MODEL)
        wup, wdn = (w_up_b, i), (w_down_b, i)
        last = gfin if i == depth - 1 else None
        proj_p = proj_s = None
        if kind == 0:
            (lng, lnb), mode_p, mode_s = _prep_sgu(ln_a_g[j], ln_a_b[j], w_s_a[j], b_s_a[j])
            win, wout = (w_in_a_b, j), (w_out_a_b, j)
            (xp,) = _sgu(xp, gm, win, lng, lnb, *mode_p, wout, tm_p, False)
            xs, v_rows = _sgu(xs, gm, win, lng, lnb, *mode_s, wout, tm_s, True)
            sgu_s.append(v_rows.reshape(n_dec, l_dec, A_HALF))
        elif kind == 1:
            wts = _prep_gla(w_in_b[j], w_gate_b[j], b_gate_b[j], gn_b[j], w_out_b[j])
            xp, sp = _gla_prompt(xp, gm, *wts, n_pr, l_pr)
            xs, ss = _gla_sample(xs, state_gla[j], gm, *wts)
            gla_p.append(sp)
            gla_s.append(ss)
        else:
            w_in = w_in_c[j].astype(BF16)
            bias_d, bias_1, bias_dec = _bias_tables(rel_bias, past_len)
            qp, kp, vp, kpt, vpt = _moba_qkv(xp, gm, w_in, tm_p, seq_len=l_pr)
            op = _moba_prompt(rel_bias, qp, kp, vp, bias_d, bias_1, n_pr, l_pr)
            qs, kn, vn = _moba_qkv(xs, gm, w_in, tm_s)
            os_ = _moba_decode(page_table, qs, kn, vn, rel_bias[0].reshape(MOBA_HEADS, 1),
                               cache_k[j].transpose(0, 2, 3, 1), cache_v[j].transpose(0, 2, 3, 1),
                               bias_dec.reshape(MOBA_HEADS, -1, PAGE_SIZE).transpose(1, 0, 2))
            wo_c = w_out_c[j].astype(BF16)
            proj_p, proj_s = (op, wo_c), (os_, wo_c)
            heads_last = lambda t: t.reshape(n_pr, MOBA_HEADS, MOBA_DH, l_pr).transpose(0, 3, 1, 2)
            k_p.append(heads_last(kpt))
            v_p.append(heads_last(vpt))
            k_s.append(kn.reshape(n_dec, l_dec, MOBA_HEADS, MOBA_DH))
            v_s.append(vn.reshape(n_dec, l_dec, MOBA_HEADS, MOBA_DH))
        if proj_p is None:
            xp = _ffn(xp, gf, wup, wdn, tm_p, gfinal=last)
            xs = _ffn(xs, gf, wup, wdn, tm_s, gfinal=last)
        else:
            xp = _ffn(xp, gf, wup, wdn, tm_p, a=proj_p[0], wo=proj_p[1], gfinal=last)
            xs = _ffn(xs, gf, wup, wdn, tm_s, a=proj_s[0], wo=proj_s[1], gfinal=last)

    return (xp.reshape(n_pr, l_pr, D_MODEL), xs.reshape(n_dec, l_dec, D_MODEL),
            jnp.stack(k_p), jnp.stack(v_p), jnp.stack(k_s), jnp.stack(v_s),
            jnp.stack(gla_p), jnp.stack(gla_s), jnp.stack(sgu_s))
```

```python
import functools
import math

import numpy as np
import jax
import jax.numpy as jnp
from jax import lax
from jax.experimental import pallas as pl
from jax.experimental.pallas import tpu as pltpu

F32 = jnp.float32
BF16 = jnp.bfloat16

D_MODEL = 1024
EPS = 1e-6
NEG = -1e30

LANES = 128
SUBLANES = 8
MXU_DIM = 256
VMEM_LIMIT = 56 * 1024 * 1024

A_CHUNK = 128
A_HALF = 3 * D_MODEL
A_GROUPS = 8
A_GROUP_DIM = A_HALF // A_GROUPS
A_PANEL = 4 * A_GROUP_DIM
A_NPANEL = A_HALF // A_PANEL

GLA_HEADS = 4
GLA_DK = 128
GLA_DV = 256
GLA_GATE_RANK = 16
GLA_GATE_NORM = 16.0
GLA_C = 256
GLA_STEP_ROWS = 2 * GLA_C

MOBA_HEADS = 16
MOBA_DH = 64
MOBA_BLOCK = 256
MOBA_TOPK = 3
MOBA_HG = MXU_DIM // MOBA_DH
MOBA_ONES_ROWS = 2 * SUBLANES
MOBA_LOOKAHEAD = 1
LOG2E = math.log2(math.e)
PAGE_SIZE = 128
REL_BUCKETS = 32
REL_MAX_DIST = 128

D_FF = 2816
FF_CHUNK = MXU_DIM
FF_NCHUNK = D_FF // FF_CHUNK

NT_DIMS = (((1,), (1,)), ((), ()))
TN_DIMS = (((0,), (0,)), ((), ()))


def _cparams(sem, **kwargs):
    return pltpu.CompilerParams(dimension_semantics=sem, vmem_limit_bytes=VMEM_LIMIT, **kwargs)


def _resident(w):
    if isinstance(w, tuple):
        arr, layer = w
        nd = arr.ndim - 1
        return pl.BlockSpec((None,) + arr.shape[1:], lambda *_: (layer,) + (0,) * nd,
                            pipeline_mode=pl.Buffered(1))
    nd = w.ndim
    return pl.BlockSpec(w.shape, lambda *_: (0,) * nd, pipeline_mode=pl.Buffered(1))


def _operand(w):
    return w[0] if isinstance(w, tuple) else w


def _rms(x, g):
    return x * lax.rsqrt(jnp.mean(x * x, axis=-1, keepdims=True) + EPS) * g


def _gelu(x):
    return 0.5 * x * (1.0 + lax.erf(x * (1.0 / math.sqrt(2.0))))


def _silu(x):
    return x * jax.nn.sigmoid(x)


def _dot(a, b):
    return jnp.dot(a, b, preferred_element_type=F32)


def _bf16_pieces(x, n):
    pieces = []
    for _ in range(n):
        p = x.astype(BF16)
        pieces.append(p)
        x = x - p.astype(F32)
    return pieces


def _ffn_body(*refs, has_proj, final):
    refs = list(refs)
    x_ref = refs.pop(0)
    a_ref = wo_ref = gf_ref = None
    if has_proj:
        a_ref = refs.pop(0)
        wo_ref = refs.pop(0)
    g_ref = refs.pop(0)
    wup_ref = refs.pop(0)
    wdn_ref = refs.pop(0)
    if final:
        gf_ref = refs.pop(0)
    o_ref = refs.pop(0)

    x = x_ref[...]
    if has_proj:
        x = x + _dot(a_ref[...].astype(BF16), wo_ref[...])
    h = _rms(x, g_ref[...]).astype(BF16)
    o_ref[...] = x

    for c in range(FF_NCHUNK):
        cols = slice(c * FF_CHUNK, (c + 1) * FF_CHUNK)
        gate = _dot(h, wup_ref[:, cols])
        up = _dot(h, wup_ref[:, D_FF + c * FF_CHUNK:D_FF + (c + 1) * FF_CHUNK])
        act = (_silu(gate) * up).astype(BF16)
        o_ref[...] += _dot(act, wdn_ref[cols, :])
    if final:
        o_ref[...] = _rms(o_ref[...], gf_ref[...])


def _ffn(x, g, wup, wdn, tm, a=None, wo=None, gfinal=None):
    rows = x.shape[0]
    row_spec = pl.BlockSpec((tm, D_MODEL), lambda i: (i, 0))
    args = [x]
    specs = [row_spec]
    if a is not None:
        args += [a, wo]
        specs += [row_spec, _resident(wo)]
    args += [g, _operand(wup), _operand(wdn)]
    specs += [_resident(g), _resident(wup), _resident(wdn)]
    if gfinal is not None:
        args.append(gfinal)
        specs.append(_resident(gfinal))
    return pl.pallas_call(
        functools.partial(_ffn_body, has_proj=a is not None, final=gfinal is not None),
        grid=(rows // tm,),
        in_specs=specs,
        out_specs=row_spec,
        out_shape=jax.ShapeDtypeStruct((rows, D_MODEL), F32),
        compiler_params=_cparams(("parallel",)),
        name="ffn",
    )(*args)


def _sgu_body(*refs, tm, emit_v):
    (x_ref, g_ref, win_ref, lng_ref, lnb_ref, wm_ref, sb_ref, wout_ref) = refs[:8]
    if emit_v:
        o_ref, vout_ref, vbuf, ubuf, sbuf = refs[8:]
    else:
        o_ref, vbuf, ubuf, sbuf = refs[8:]
        vout_ref = None
    panel = lambda p, base=0: slice(base + p * A_PANEL, base + (p + 1) * A_PANEL)

    x = x_ref[...]
    h = _rms(x, g_ref[...]).astype(BF16)

    rsum = jnp.zeros((tm, 1), F32)
    for p in range(A_NPANEL):
        v = _gelu(_dot(h, win_ref[:, panel(p, A_HALF)]))
        vbuf[p] = v
        rsum = rsum + jnp.sum(v, axis=-1, keepdims=True)
    for p in range(A_NPANEL):
        ubuf[p] = _gelu(_dot(h, win_ref[:, panel(p)]))
    mean = rsum * (1.0 / A_HALF)
    ssq = jnp.zeros((tm, 1), F32)
    for p in range(A_NPANEL):
        d = vbuf[p] - mean
        ssq = ssq + jnp.sum(d * d, axis=-1, keepdims=True)
    rstd = lax.rsqrt(ssq * (1.0 / A_HALF) + EPS)

    acc = x
    for p in range(A_NPANEL):
        vn = (vbuf[p] - mean) * rstd * lng_ref[:, panel(p)] + lnb_ref[:, panel(p)]
        if emit_v:
            vout_ref[:, p * A_PANEL:(p + 1) * A_PANEL] = vn
        vnb = vn.astype(BF16)
        for c in range(tm // A_CHUNK):
            for gg in range(A_PANEL // A_GROUP_DIM):
                grp = p * (A_PANEL // A_GROUP_DIM) + gg
                rs = slice(c * A_CHUNK, (c + 1) * A_CHUNK)
                cs = slice(gg * A_GROUP_DIM, (gg + 1) * A_GROUP_DIM)
                sbuf[rs, cs] = _dot(wm_ref[grp], vnb[rs, cs]) + sb_ref[grp]
        acc = acc + _dot((ubuf[p] * sbuf[...]).astype(BF16), wout_ref[panel(p), :])
    o_ref[...] = acc


def _sgu(x, g, win, lng, lnb, wm, sb, wout, tm, emit_v):
    rows = x.shape[0]
    row_spec = pl.BlockSpec((tm, D_MODEL), lambda i: (i, 0))
    weights = [g, win, lng, lnb, wm, sb, wout]
    out_shape = [jax.ShapeDtypeStruct((rows, D_MODEL), F32)]
    out_specs = [row_spec]
    if emit_v:
        out_shape.append(jax.ShapeDtypeStruct((rows, A_HALF), F32))
        out_specs.append(pl.BlockSpec((tm, A_HALF), lambda i: (i, 0)))
    return pl.pallas_call(
        functools.partial(_sgu_body, tm=tm, emit_v=emit_v),
        grid=(rows // tm,),
        in_specs=[row_spec] + [_resident(w) for w in weights],
        out_specs=out_specs,
        out_shape=out_shape,
        scratch_shapes=[pltpu.VMEM((A_NPANEL, tm, A_PANEL), F32),
                        pltpu.VMEM((A_NPANEL, tm, A_PANEL), F32),
                        pltpu.VMEM((tm, A_PANEL), F32)],
        compiler_params=_cparams(("parallel",)),
        name="sgu",
    )(x, *[_operand(w) for w in weights])


def _log_sigmoid(z):
    return jnp.minimum(z, 0.0) - jnp.log1p(jnp.exp(-jnp.abs(z)))


def _gla_intra(qh, kh, cum, mask_ref):
    c = qh.shape[0]
    r = lax.broadcasted_iota(jnp.int32, (c, 1), 0)
    a = jnp.zeros((c, c), F32)
    m, level = c, 0
    while m >= 2:
        half = m // 2
        pos = r & (m - 1)
        upper = pos >= half
        if m >= 2 * SUBLANES:
            pieces = []
            for b in range(c // m):
                row = b * m + half - 1
                pieces.append(jnp.broadcast_to(cum[row:row + 1, :], (m, cum.shape[1])))
            ref = pieces[0] if len(pieces) == 1 else jnp.concatenate(pieces, axis=0)
        else:
            c8 = cum.reshape(c // SUBLANES, SUBLANES, cum.shape[1])
            sub = lax.broadcasted_iota(jnp.int32, (1, SUBLANES, 1), 1)
            ref = None
            for b in reversed(range(SUBLANES // m)):
                row = jnp.broadcast_to(c8[:, b * m + half - 1:b * m + half, :], c8.shape)
                ref = row if ref is None else jnp.where(sub < (b + 1) * m, row, ref)
            ref = ref.reshape(c, cum.shape[1])
        decay = jnp.exp(jnp.where(upper, cum - ref, ref - cum))
        x = (jnp.where(upper, qh, kh) * decay).astype(BF16)
        a = a + lax.dot_general(x, x, NT_DIMS, preferred_element_type=F32) * mask_ref[level]
        m = half
        level += 1
    return a


def _gla_level_masks(c):
    rowi = lax.broadcasted_iota(jnp.int32, (c, c), 0)
    coli = lax.broadcasted_iota(jnp.int32, (c, c), 1)
    masks = []
    m = c
    while m >= 2:
        sh, half = int(math.log2(m)), m // 2
        keep = ((rowi >> sh) == (coli >> sh)) & ((rowi & (m - 1)) >= half) & ((coli & (m - 1)) < half)
        masks.append(jnp.where(keep, 1.0, 0.0))
        m = half
    return masks


def _gla_project(h, win_ref, wgl_ref, wgate_ref, bgate_ref):
    dq, dv = GLA_HEADS * GLA_DK, GLA_HEADS * GLA_DV
    q = _dot(h, win_ref[:, :dq]) * (GLA_DK ** -0.5)
    k = _dot(h, win_ref[:, dq:2 * dq])
    v = _dot(h, win_ref[:, 2 * dq:2 * dq + dv])
    g = _dot(h, win_ref[:, 2 * dq + dv:2 * dq + 2 * dv])
    gl = _dot(h, wgl_ref[...])
    z = _dot(gl.astype(BF16), wgate_ref[...]) + bgate_ref[...]
    return q, k, v, g, z


def _gla_prompt_body(x_ref, gm_ref, win_ref, wgl_ref, wgate_ref, bgate_ref,
                     gn_ref, wo_ref, o_ref, st_ref, s_scr, mask_scr):
    ci = pl.program_id(1)
    c = GLA_C

    @pl.when((pl.program_id(0) == 0) & (ci == 0))
    def _():
        for level, mask in enumerate(_gla_level_masks(c)):
            mask_scr[level] = mask

    @pl.when(ci == 0)
    def _():
        s_scr[...] = jnp.zeros_like(s_scr)

    x = x_ref[...]
    h = _rms(x, gm_ref[...]).astype(BF16)
    q, k, v, g, z = _gla_project(h, win_ref, wgl_ref, wgate_ref, bgate_ref)
    la = _log_sigmoid(z) * (1.0 / GLA_GATE_NORM)

    rowi = lax.broadcasted_iota(jnp.int32, (c, c), 0)
    coli = lax.broadcasted_iota(jnp.int32, (c, c), 1)
    tril = (coli <= rowi).astype(BF16)

    ys = []
    for sc in range(x.shape[0] // c):
        rs = slice(sc * c, (sc + 1) * c)
        cum = functools.reduce(lambda x, y: x + y,
                               [_dot(tril, piece) for piece in _bf16_pieces(la[rs], 3)])
        outs = []
        for hd in range(GLA_HEADS):
            ks = slice(hd * GLA_DK, (hd + 1) * GLA_DK)
            vs = slice(hd * GLA_DV, (hd + 1) * GLA_DV)
            qh, kh, cumh, vh = q[rs, ks], k[rs, ks], cum[:, ks], v[rs, vs]
            vhb = vh.astype(BF16)
            total = cumh[c - 1:c, :]
            st = s_scr[hd]
            o = lax.dot_general((qh * jnp.exp(cumh)).astype(BF16), st.astype(BF16), NT_DIMS,
                                preferred_element_type=F32)
            a = _gla_intra(qh, kh, cumh, mask_scr)
            o = o + _dot(a.astype(BF16), vhb)
            o = o + jnp.sum(qh * kh, axis=-1, keepdims=True) * vh
            kdec = (kh * jnp.exp(total - cumh)).astype(BF16)
            st_new = st * jnp.exp(total) + lax.dot_general(vhb, kdec, TN_DIMS,
                                                           preferred_element_type=F32)
            s_scr[hd] = st_new
            o = o * lax.rsqrt(jnp.mean(o * o, axis=-1, keepdims=True) + EPS) * gn_ref[...]
            outs.append(o)
        ys.append(jnp.concatenate(outs, axis=1))
    y = jnp.concatenate(ys, axis=0) * _silu(g)
    o_ref[...] = x + _dot(y.astype(BF16), wo_ref[...])

    @pl.when(ci == pl.num_programs(1) - 1)
    def _():
        for hd in range(GLA_HEADS):
            st_ref[0, hd] = s_scr[hd].T


def _gla_prompt(x, gm, win, wgl, wgate, bgate, gn, wo, n_seq, seq_len):
    n_chunk = seq_len // GLA_STEP_ROWS
    weights = [gm, win, wgl, wgate, bgate, gn, wo]
    row_spec = pl.BlockSpec((GLA_STEP_ROWS, D_MODEL), lambda b, c: (b * n_chunk + c, 0))
    return pl.pallas_call(
        _gla_prompt_body,
        grid=(n_seq, n_chunk),
        in_specs=[row_spec] + [_resident(w) for w in weights],
        out_specs=[row_spec,
                   pl.BlockSpec((1, GLA_HEADS, GLA_DK, GLA_DV), lambda b, c: (b, 0, 0, 0))],
        out_shape=[jax.ShapeDtypeStruct(x.shape, F32),
                   jax.ShapeDtypeStruct((n_seq, GLA_HEADS, GLA_DK, GLA_DV), F32)],
        scratch_shapes=[pltpu.VMEM((GLA_HEADS, GLA_DV, GLA_DK), F32),
                        pltpu.VMEM((int(math.log2(GLA_C)), GLA_C, GLA_C), F32)],
        compiler_params=_cparams(("arbitrary", "arbitrary")),
        name="gla_prompt",
    )(x, *weights)


def _gla_sample_proj_body(x_ref, gm_ref, win_ref, wgl_ref, wgate_ref, bgate_ref,
                          q_ref, k_ref, a_ref, v_ref, g_ref):
    h = _rms(x_ref[...], gm_ref[...]).astype(BF16)
    q_ref[...], k_ref[...], v_ref[...], g_ref[...], z = _gla_project(
        h, win_ref, wgl_ref, wgate_ref, bgate_ref)
    a_ref[...] = jnp.exp(_log_sigmoid(z) * (1.0 / GLA_GATE_NORM))


GLA_SEQ_PER_STEP = SUBLANES


def _gla_sample_rec_body(s_ref, qc_ref, kc_ref, ac_ref, v_ref, so_ref, o_ref):
    for j in range(GLA_SEQ_PER_STEP):
        for hd in range(GLA_HEADS):
            ks = slice(hd * GLA_DK, (hd + 1) * GLA_DK)
            vs = slice(hd * GLA_DV, (hd + 1) * GLA_DV)
            acol = ac_ref[0, ks, j:j + 1]
            kcol = kc_ref[0, ks, j:j + 1]
            qcol = qc_ref[0, ks, j:j + 1]
            vrow = v_ref[j:j + 1, vs]
            s_new = s_ref[j, hd] * acol + kcol * vrow
            so_ref[j, hd] = s_new
            o_ref[j:j + 1, vs] = jnp.sum(s_new * qcol, axis=0, keepdims=True)


def _gla_sample_out_body(x_ref, o_ref, g_ref, gn_ref, wo_ref, y_ref):
    outs = []
    for hd in range(GLA_HEADS):
        o = o_ref[:, hd * GLA_DV:(hd + 1) * GLA_DV]
        outs.append(o * lax.rsqrt(jnp.mean(o * o, axis=-1, keepdims=True) + EPS) * gn_ref[...])
    y = jnp.concatenate(outs, axis=1) * _silu(g_ref[...])
    y_ref[...] = x_ref[...] + _dot(y.astype(BF16), wo_ref[...])


def _gla_sample(x, state, gm, win, wgl, wgate, bgate, gn, wo):
    n = x.shape[0]
    dq = GLA_HEADS * GLA_DK
    weights = [gm, win, wgl, wgate, bgate]
    q, k, a, v, g = pl.pallas_call(
        _gla_sample_proj_body,
        out_shape=[jax.ShapeDtypeStruct((n, dq), F32)] * 3
        + [jax.ShapeDtypeStruct((n, D_MODEL), F32)] * 2,
        compiler_params=pltpu.CompilerParams(vmem_limit_bytes=VMEM_LIMIT),
        name="gla_sample_proj",
    )(x, *weights)

    nstep = n // GLA_SEQ_PER_STEP

    def cols(t):
        return t.reshape(nstep, GLA_SEQ_PER_STEP, dq).transpose(0, 2, 1)

    col_spec = pl.BlockSpec((1, dq, GLA_SEQ_PER_STEP), lambda i: (i, 0, 0))
    st_spec = pl.BlockSpec((GLA_SEQ_PER_STEP, GLA_HEADS, GLA_DK, GLA_DV), lambda i: (i, 0, 0, 0))
    row_spec = pl.BlockSpec((GLA_SEQ_PER_STEP, D_MODEL), lambda i: (i, 0))
    s_new, o = pl.pallas_call(
        _gla_sample_rec_body,
        grid=(nstep,),
        in_specs=[st_spec, col_spec, col_spec, col_spec, row_spec],
        out_specs=[st_spec, row_spec],
        out_shape=[jax.ShapeDtypeStruct(state.shape, F32), jax.ShapeDtypeStruct((n, D_MODEL), F32)],
        compiler_params=_cparams(("parallel",)),
        name="gla_sample_rec",
    )(state, cols(q), cols(k), cols(a), v)

    y = pl.pallas_call(
        _gla_sample_out_body,
        out_shape=jax.ShapeDtypeStruct((n, D_MODEL), F32),
        compiler_params=pltpu.CompilerParams(vmem_limit_bytes=VMEM_LIMIT),
        name="gla_sample_out",
    )(x, o, g, gn, wo)
    return y, s_new


def _t5_bucket_np(dist):
    max_exact = REL_BUCKETS // 2
    n = np.maximum(dist, 0)
    nf = np.maximum(n, max_exact).astype(np.float32)
    large = max_exact + (np.log(nf / np.float32(max_exact)) / np.float32(math.log(REL_MAX_DIST / max_exact))
                         * np.float32(REL_BUCKETS - max_exact)).astype(np.int32)
    return np.where(n < max_exact, n, np.minimum(large, REL_BUCKETS - 1)).astype(np.int32)


def _bias_tables_body(rb_ref, bd_ref, b1_ref, bdec_ref, od_ref, o1_ref, odec_ref):
    h = pl.program_id(0)
    td, t1, tdec = bd_ref[...], b1_ref[...], bdec_ref[...]
    ad = jnp.zeros(td.shape, F32)
    a1 = jnp.zeros(t1.shape, F32)
    adec = jnp.zeros(tdec.shape, F32)
    for b in range(REL_BUCKETS):
        val = rb_ref[b, h]
        ad = jnp.where(td == b, val, ad)
        a1 = jnp.where(t1 == b, val, a1)
        adec = jnp.where(tdec == b, val, adec)
    od_ref[0] = ad * LOG2E
    o1_ref[0] = a1 * LOG2E
    odec_ref[0] = adec


def _bias_tables(rel_bias, past_len):
    kc = np.arange(MOBA_BLOCK)[:, None]
    qr = np.arange(MOBA_BLOCK)[None, :]
    bd = _t5_bucket_np(qr - kc)
    b1 = _t5_bucket_np(MOBA_BLOCK + qr - kc)
    nblk = past_len // MOBA_BLOCK
    kpos = np.arange(nblk)[:, None] * MOBA_BLOCK + np.arange(MOBA_BLOCK)[None, :]
    bdec = _t5_bucket_np(past_len - kpos)
    full = lambda a: pl.BlockSpec(a.shape, lambda h: (0,) * a.ndim)
    return pl.pallas_call(
        _bias_tables_body,
        grid=(MOBA_HEADS,),
        in_specs=[pl.BlockSpec(memory_space=pltpu.SMEM), full(bd), full(b1), full(bdec)],
        out_specs=[pl.BlockSpec((1,) + bd.shape, lambda h: (h, 0, 0)),
                   pl.BlockSpec((1,) + b1.shape, lambda h: (h, 0, 0)),
                   pl.BlockSpec((1,) + bdec.shape, lambda h: (h, 0, 0))],
        out_shape=[jax.ShapeDtypeStruct((MOBA_HEADS,) + bd.shape, F32),
                   jax.ShapeDtypeStruct((MOBA_HEADS,) + b1.shape, F32),
                   jax.ShapeDtypeStruct((MOBA_HEADS,) + bdec.shape, F32)],
        compiler_params=_cparams(("parallel",)),
        name="moba_bias_tables",
    )(rel_bias, jnp.asarray(bd), jnp.asarray(b1), jnp.asarray(bdec))


def _moba_qkv_body(x_ref, g_ref, w_ref, q_ref, k_ref, v_ref, *t_refs):
    h = _rms(x_ref[...], g_ref[...]).astype(BF16)
    qkv = _dot(h, w_ref[...])
    q_ref[...] = qkv[:, :D_MODEL] * (MOBA_DH ** -0.5)
    k = qkv[:, D_MODEL:2 * D_MODEL]
    v = qkv[:, 2 * D_MODEL:]
    k_ref[...] = k
    v_ref[...] = v
    if t_refs:
        t_refs[0][0] = k.T
        t_refs[1][0] = v.T


def _moba_qkv(x, g, w, tm, seq_len=None):
    rows = x.shape[0]
    row_spec = pl.BlockSpec((tm, D_MODEL), lambda i: (i, 0))
    out_specs = [row_spec] * 3
    out_shape = [jax.ShapeDtypeStruct((rows, D_MODEL), F32)] * 3
    if seq_len is not None:
        per_seq = seq_len // tm
        t_spec = pl.BlockSpec((1, D_MODEL, tm), lambda i: (i // per_seq, 0, i % per_seq))
        out_specs += [t_spec] * 2
        out_shape += [jax.ShapeDtypeStruct((rows // seq_len, D_MODEL, seq_len), F32)] * 2
    return pl.pallas_call(
        _moba_qkv_body,
        grid=(rows // tm,),
        in_specs=[row_spec, _resident(g), _resident(w)],
        out_specs=out_specs,
        out_shape=out_shape,
        compiler_params=_cparams(("parallel",)),
        name="moba_qkv",
    )(x, g, w)


def _topk_select(gate, valid, axis):
    n = gate.shape[axis]
    idx = lax.broadcasted_iota(jnp.int32, gate.shape, axis)
    cnt = jnp.zeros(gate.shape, jnp.int32)
    for m in range(n):
        gm = lax.slice_in_dim(gate, m, m + 1, axis=axis)
        beats = (gm > gate) | ((gm == gate) & (m < idx))
        cnt = cnt + jnp.where(beats, valid(m), 0)
    return jnp.where(cnt < MOBA_TOPK, 1.0, 0.0)


def _moba_prompt_body(rb_ref, q_ref, k_ref, v_ref, bd_ref, b1_ref, o_ref,
                      kb_scr, vt_scr, km_scr):
    hg = pl.program_id(1)
    qi = pl.program_id(2)
    nblk = kb_scr.shape[0]
    blk = MOBA_BLOCK
    dh = MOBA_DH
    width = MOBA_HG * dh

    lane = lax.broadcasted_iota(jnp.int32, (1, width), 1)
    in_head = [(lane >= hh * dh) & (lane < (hh + 1) * dh) for hh in range(MOBA_HG)]

    @pl.when(qi == 0)
    def _():
        ones = jnp.ones((MOBA_ONES_ROWS, blk), BF16)
        means = []
        for n in range(nblk):
            kt = k_ref[n * blk:(n + 1) * blk, :]
            kb_scr[n] = kt.astype(BF16)
            means.append(jnp.mean(kt, axis=0, keepdims=True))
            vt = v_ref[n * blk:(n + 1) * blk, :].T.astype(BF16)
            for hh in range(MOBA_HG):
                vt_scr[n, hh, :dh, :] = vt[hh * dh:(hh + 1) * dh, :]
                vt_scr[n, hh, dh:, :] = ones
        km = jnp.concatenate(means, axis=0)
        per_head = [jnp.where(in_head[hh], km, 0.0) for hh in range(MOBA_HG)]
        pad = jnp.zeros((LANES - MOBA_HG * nblk, width), F32)
        km_t = jnp.concatenate(per_head + [pad], axis=0).T
        km_scr[0], km_scr[1] = _bf16_pieces(km_t, 2)

    q2 = q_ref[...]
    ki_idx = lax.broadcasted_iota(jnp.int32, (blk, blk), 0)
    qi_idx = lax.broadcasted_iota(jnp.int32, (blk, blk), 1)
    blk_idx = lax.broadcasted_iota(jnp.int32, (nblk, blk), 0)

    hs = range(MOBA_HG)
    qhb, far_bias, neg, tile_of = [], [], [], []
    q_hi, q_lo = _bf16_pieces(q2, 2)
    gates = (_dot(q_hi, km_scr[0]) + _dot(q_hi, km_scr[1]) + _dot(q_lo, km_scr[0])).T
    for hh in hs:
        tile_of.append(slice(hh * dh // LANES * LANES, (hh * dh // LANES + 1) * LANES))
        qhb.append((jnp.where(in_head[hh], q2, 0.0)[:, tile_of[hh]] * LOG2E).astype(BF16))
        far_bias.append(rb_ref[REL_BUCKETS - 1, MOBA_HG * hg + hh] * LOG2E)
        gate = gates[hh * nblk:(hh + 1) * nblk, :]
        keep = _topk_select(gate, lambda m: (m < qi).astype(jnp.int32), 0)
        neg.append(jnp.where((keep > 0.0) & (blk_idx < qi), 0.0, NEG))

    def scores(k0, n):
        kt = kb_scr[k0:k0 + n].reshape(n * blk, width)
        return [lax.dot_general(kt[:, tile_of[hh]], qhb[hh], NT_DIMS, preferred_element_type=F32)
                for hh in hs]

    def values(ki, p):
        return [_dot(vt_scr[ki, hh], p[hh]) for hh in hs]

    def weights(x):
        return jnp.exp2(x.astype(BF16))

    def step(c):
        units = [(c - 1, 2)] if c >= 1 else [(0, 1)]
        t = 0
        while t < c - 1:
            n = min(2, c - 1 - t)
            units.append((t, n))
            t += n
        m = [None] * MOBA_HG
        acc = [None] * MOBA_HG
        ahead = [scores(*u) for u in units[:MOBA_LOOKAHEAD]]
        for ui, (k0, n) in enumerate(units):
            s = ahead.pop(0)
            if ui + MOBA_LOOKAHEAD < len(units):
                ahead.append(scores(*units[ui + MOBA_LOOKAHEAD]))
            p = [[] for _ in range(n)]
            alpha = [None] * MOBA_HG
            for hh in hs:
                sh, rows, tile_max = [], [], []
                for t in range(n):
                    kb = k0 + t
                    x = s[hh][t * blk:(t + 1) * blk, :]
                    if kb == c:
                        x, r = jnp.where(ki_idx <= qi_idx, x + bd_ref[hh], NEG), None
                    elif kb == c - 1:
                        x, r = x + b1_ref[hh], neg[hh][kb:kb + 1, :]
                    else:
                        r = neg[hh][kb:kb + 1, :] + far_bias[hh]
                    mx = jnp.max(x, axis=0, keepdims=True)
                    sh.append(x)
                    rows.append(r)
                    tile_max.append(mx if r is None else mx + r)
                m_new = functools.reduce(jnp.maximum, tile_max if m[hh] is None else [m[hh]] + tile_max)
                for t in range(n):
                    p[t].append(weights(sh[t] - (m_new if rows[t] is None else m_new - rows[t])))
                if m[hh] is not None:
                    alpha[hh] = jnp.exp2(m[hh] - m_new)
                m[hh] = m_new
            pv = [values(k0 + t, p[t]) for t in range(n)]
            for hh in hs:
                tot = functools.reduce(lambda x, y: x + y, [pv[t][hh] for t in range(n)])
                acc[hh] = tot if acc[hh] is None else alpha[hh] * acc[hh] + tot
        ot = jnp.concatenate([acc[hh][:dh, :] / acc[hh][dh:dh + 1, :] for hh in hs], axis=0)
        o_ref[...] = ot.T

    for c in range(nblk):
        pl.when(qi == c)(functools.partial(step, c))


def _moba_prompt(rel_bias, q, k, v, bias_d, bias_1, n_seq, seq_len):
    nblk = seq_len // MOBA_BLOCK
    hgrp = MOBA_HG
    width = hgrp * MOBA_DH
    q_spec = pl.BlockSpec((MOBA_BLOCK, width), lambda b, hg, qi: (b * nblk + qi, hg))
    kv_spec = pl.BlockSpec((seq_len, width), lambda b, hg, qi: (b, hg))
    bias_spec = pl.BlockSpec((hgrp, MOBA_BLOCK, MOBA_BLOCK), lambda b, hg, qi: (hg, 0, 0))
    return pl.pallas_call(
        _moba_prompt_body,
        grid=(n_seq, MOBA_HEADS // hgrp, nblk),
        in_specs=[pl.BlockSpec(memory_space=pltpu.SMEM), q_spec, kv_spec, kv_spec, bias_spec, bias_spec],
        out_specs=q_spec,
        out_shape=jax.ShapeDtypeStruct(q.shape, F32),
        scratch_shapes=[
            pltpu.VMEM((nblk, MOBA_BLOCK, width), BF16),
            pltpu.VMEM((nblk, hgrp, MOBA_DH + MOBA_ONES_ROWS, MOBA_BLOCK), BF16),
            pltpu.VMEM((2, width, LANES), BF16),
        ],
        compiler_params=_cparams(("parallel", "parallel", "arbitrary")),
        name="moba_prompt",
    )(rel_bias, q, k, v, bias_d, bias_1)


def _col_to_row(col):
    n = col.shape[0]
    r = lax.broadcasted_iota(jnp.int32, (n, n), 0)
    c = lax.broadcasted_iota(jnp.int32, (n, n), 1)
    return jnp.sum(jnp.where(r == c, jnp.broadcast_to(col, (n, n)), 0.0), axis=0, keepdims=True)


def _moba_decode_body(pt_ref, q_ref, qt_ref, kn_ref, vnt_ref, rb0_ref, bdec_ref, *refs, npg):
    del pt_ref
    ck, cv = refs[:npg], refs[npg:2 * npg]
    o_ref, qb_scr, s_scr, p_scr, t_scr = refs[2 * npg:]
    heads, dh = MOBA_HEADS, MOBA_DH
    ppb = MOBA_BLOCK // PAGE_SIZE
    nb = npg // ppb

    qt = qt_ref[0]
    for h in range(heads):
        qb_scr[h] = jnp.broadcast_to(qt[:, h:h + 1], (dh, PAGE_SIZE))

    def k_step(h, carry):
        qb = qb_scr[h]
        for pg in range(npg):
            s_scr[pg, pl.ds(h, 1), :] = jnp.sum(ck[pg][0, h] * qb, axis=0, keepdims=True)
        return carry

    lax.fori_loop(0, heads, k_step, 0, unroll=2)

    z, m_blk = [], []
    blk_lane = lax.broadcasted_iota(jnp.int32, (heads, nb), 1)
    gate = jnp.zeros((heads, nb), F32)
    for b in range(nb):
        zb = []
        g = jnp.zeros((heads, 1), F32)
        for j in range(ppb):
            s = s_scr[ppb * b + j]
            g = g + jnp.sum(s, axis=-1, keepdims=True)
            zb.append(s + bdec_ref[ppb * b + j])
        gate = jnp.where(blk_lane == b, g, gate)
        m_blk.append(functools.reduce(jnp.maximum, [jnp.max(t, axis=-1, keepdims=True) for t in zb]))
        z += zb
    keep = _topk_select(gate, lambda m_: 1, 1)

    l_new = jnp.sum(q_ref[0] * kn_ref[0], axis=-1, keepdims=True) + rb0_ref[...]
    m_all = l_new
    for b in range(nb):
        m_all = jnp.maximum(m_all, jnp.where(keep[:, b:b + 1] > 0.0, m_blk[b], NEG))
    w_new = jnp.exp(l_new - m_all)
    den = w_new
    for b in range(nb):
        kept = keep[:, b:b + 1] > 0.0
        for j in range(ppb):
            p = jnp.exp(jnp.where(kept, z[ppb * b + j] - m_all, NEG))
            den = den + jnp.sum(p, axis=-1, keepdims=True)
            p_scr[ppb * b + j] = p

    def v_step(h, carry):
        acc = cv[0][0, h] * p_scr[0, pl.ds(h, 1), :]
        for pg in range(1, npg):
            acc = acc + cv[pg][0, h] * p_scr[pg, pl.ds(h, 1), :]
        t_scr[h] = jnp.broadcast_to(jnp.sum(acc, axis=-1, keepdims=True), (dh, PAGE_SIZE))
        return carry

    lax.fori_loop(0, heads, v_step, 0, unroll=4)

    head_lane = lax.broadcasted_iota(jnp.int32, (dh, heads), 1)
    tot = jnp.zeros((dh, heads), F32)
    for h in range(heads):
        tot = jnp.where(head_lane == h, t_scr[h][:, :heads], tot)
    o_ref[0] = (tot + vnt_ref[0] * _col_to_row(w_new)) / _col_to_row(den)


def _moba_decode(page_table, q, k_new, v_new, rb0, cache_k, cache_v, bias_dec):
    n_dec, n_pages = page_table.shape
    heads, dh = MOBA_HEADS, MOBA_DH
    rows = lambda t: t.reshape(n_dec, heads, dh)
    cols = lambda t: t.reshape(n_dec, heads, dh).transpose(0, 2, 1)
    row_spec = pl.BlockSpec((1, heads, dh), lambda b, pt: (b, 0, 0))
    col_spec = pl.BlockSpec((1, dh, heads), lambda b, pt: (b, 0, 0))
    page_specs = [pl.BlockSpec((1, heads, dh, PAGE_SIZE), lambda b, pt, j=j: (pt[b, j], 0, 0, 0))
                  for j in range(n_pages)]
    grid_spec = pltpu.PrefetchScalarGridSpec(
        num_scalar_prefetch=1,
        grid=(n_dec,),
        in_specs=[row_spec, col_spec, row_spec, col_spec,
                  pl.BlockSpec((heads, 1), lambda b, pt: (0, 0)),
                  pl.BlockSpec(bias_dec.shape, lambda b, pt: (0, 0, 0))] + page_specs + page_specs,
        out_specs=col_spec,
        scratch_shapes=[pltpu.VMEM((heads, dh, PAGE_SIZE), F32),
                        pltpu.VMEM((n_pages, heads, PAGE_SIZE), F32),
                        pltpu.VMEM((n_pages, heads, PAGE_SIZE), F32),
                        pltpu.VMEM((heads, dh, PAGE_SIZE), F32)],
    )
    out = pl.pallas_call(
        functools.partial(_moba_decode_body, npg=n_pages),
        grid_spec=grid_spec,
        out_shape=jax.ShapeDtypeStruct((n_dec, dh, heads), F32),
        compiler_params=_cparams(("parallel",)),
        name="moba_decode",
    )(page_table, rows(q), cols(q), rows(k_new), cols(v_new), rb0, bias_dec,
      *([cache_k] * n_pages), *([cache_v] * n_pages))
    return out.transpose(0, 2, 1).reshape(n_dec, heads * dh)


def _prep_sgu(ln_g, ln_b, w_s, b_s):
    lng = ln_g.reshape(1, A_HALF)
    lnb = ln_b.reshape(1, A_HALF)
    causal = jnp.tril(jnp.ones((A_CHUNK, A_CHUNK), dtype=bool))
    wm_p = jnp.where(causal[None], w_s, 0.0).astype(BF16)
    sb_p = jnp.broadcast_to(b_s[:, :, None], (A_GROUPS, A_CHUNK, A_GROUP_DIM))
    eye = jnp.eye(A_CHUNK, dtype=F32)
    wm_s = (w_s[:, 0, 0][:, None, None] * eye[None]).astype(BF16)
    sb_s = jnp.broadcast_to(b_s[:, 0][:, None, None], (A_GROUPS, A_CHUNK, A_GROUP_DIM))
    return (lng, lnb), (wm_p, sb_p), (wm_s, sb_s)


def _prep_gla(w_in, w_gate, b_gate, gn, w_out):
    dq, dv = GLA_HEADS * GLA_DK, GLA_HEADS * GLA_DV
    pad = LANES - GLA_GATE_RANK
    wgl = jnp.pad(w_in[:, 2 * dq + 2 * dv:], ((0, 0), (0, pad))).astype(BF16)
    wgate = jnp.pad(w_gate, ((0, pad), (0, 0))).astype(BF16)
    return (w_in.astype(BF16), wgl, wgate, b_gate.reshape(1, dq), gn.reshape(1, GLA_DV),
            w_out.astype(BF16))


def kernel(x_prompt, x_sample, cache_k, cache_v, page_table, state_gla, norm_mix, norm_ffn, norm_final,
           w_in_a, ln_a_g, ln_a_b, w_s_a, b_s_a, w_out_a, w_in_b, w_gate_b, b_gate_b, gn_b, w_out_b,
           w_in_c, w_out_c, rel_bias, w_up, w_down):
    n_pr, l_pr, _ = x_prompt.shape
    n_dec, l_dec, _ = x_sample.shape
    assert l_dec == 1 and l_pr % MOBA_BLOCK == 0 and l_pr % GLA_STEP_ROWS == 0
    depth = norm_mix.shape[0]
    past_len = page_table.shape[1] * PAGE_SIZE
    assert past_len % MOBA_BLOCK == 0
    tm_p, tm_s = 512, n_dec

    xp = x_prompt.reshape(n_pr * l_pr, D_MODEL)
    xs = x_sample.reshape(n_dec, D_MODEL)
    gfin = norm_final.reshape(1, D_MODEL)
    w_up_b, w_down_b = w_up.astype(BF16), w_down.astype(BF16)
    w_in_a_b, w_out_a_b = w_in_a.astype(BF16), w_out_a.astype(BF16)
    k_p, v_p, k_s, v_s, gla_p, gla_s, sgu_s = [], [], [], [], [], [], []

    for i in range(depth):
        kind, j = i % 3, i // 3
        gm = norm_mix[i].reshape(1, D_MODEL)
        gf = norm_ffn[i].reshape(1, D_MODEL)
        wup, wdn = (w_up_b, i), (w_down_b, i)
        last = gfin if i == depth - 1 else None
        proj_p = proj_s = None
        if kind == 0:
            (lng, lnb), mode_p, mode_s = _prep_sgu(ln_a_g[j], ln_a_b[j], w_s_a[j], b_s_a[j])
            win, wout = (w_in_a_b, j), (w_out_a_b, j)
            (xp,) = _sgu(xp, gm, win, lng, lnb, *mode_p, wout, tm_p, False)
            xs, v_rows = _sgu(xs, gm, win, lng, lnb, *mode_s, wout, tm_s, True)
            sgu_s.append(v_rows.reshape(n_dec, l_dec, A_HALF))
        elif kind == 1:
            wts = _prep_gla(w_in_b[j], w_gate_b[j], b_gate_b[j], gn_b[j], w_out_b[j])
            xp, sp = _gla_prompt(xp, gm, *wts, n_pr, l_pr)
            xs, ss = _gla_sample(xs, state_gla[j], gm, *wts)
            gla_p.append(sp)
            gla_s.append(ss)
        else:
            w_in = w_in_c[j].astype(BF16)
            bias_d, bias_1, bias_dec = _bias_tables(rel_bias, past_len)
            qp, kp, vp, kpt, vpt = _moba_qkv(xp, gm, w_in, tm_p, seq_len=l_pr)
            op = _moba_prompt(rel_bias, qp, kp, vp, bias_d, bias_1, n_pr, l_pr)
            qs, kn, vn = _moba_qkv(xs, gm, w_in, tm_s)
            os_ = _moba_decode(page_table, qs, kn, vn, rel_bias[0].reshape(MOBA_HEADS, 1),
                               cache_k[j].transpose(0, 2, 3, 1), cache_v[j].transpose(0, 2, 3, 1),
                               bias_dec.reshape(MOBA_HEADS, -1, PAGE_SIZE).transpose(1, 0, 2))
            wo_c = w_out_c[j].astype(BF16)
            proj_p, proj_s = (op, wo_c), (os_, wo_c)
            heads_last = lambda t: t.reshape(n_pr, MOBA_HEADS, MOBA_DH, l_pr).transpose(0, 3, 1, 2)
            k_p.append(heads_last(kpt))
            v_p.append(heads_last(vpt))
            k_s.append(kn.reshape(n_dec, l_dec, MOBA_HEADS, MOBA_DH))
            v_s.append(vn.reshape(n_dec, l_dec, MOBA_HEADS, MOBA_DH))
        if proj_p is None:
            xp = _ffn(xp, gf, wup, wdn, tm_p, gfinal=last)
            xs = _ffn(xs, gf, wup, wdn, tm_s, gfinal=last)
        else:
            xp = _ffn(xp, gf, wup, wdn, tm_p, a=proj_p[0], wo=proj_p[1], gfinal=last)
            xs = _ffn(xs, gf, wup, wdn, tm_s, a=proj_s[0], wo=proj_s[1], gfinal=last)

    return (xp.reshape(n_pr, l_pr, D_MODEL), xs.reshape(n_dec, l_dec, D_MODEL),
            jnp.stack(k_p), jnp.stack(v_p), jnp.stack(k_s), jnp.stack(v_s),
            jnp.stack(gla_p), jnp.stack(gla_s), jnp.stack(sgu_s))
```

```python
import functools
import math

import numpy as np
import jax
import jax.numpy as jnp
from jax import lax
from jax.experimental import pallas as pl
from jax.experimental.pallas import tpu as pltpu

F32 = jnp.float32
BF16 = jnp.bfloat16

D_MODEL = 1024
EPS = 1e-6
NEG = -1e30

LANES = 128
SUBLANES = 8
MXU_DIM = 256
VMEM_LIMIT = 56 * 1024 * 1024

A_CHUNK = 128
A_HALF = 3 * D_MODEL
A_GROUPS = 8
A_GROUP_DIM = A_HALF // A_GROUPS
A_PANEL = 4 * A_GROUP_DIM
A_NPANEL = A_HALF // A_PANEL

GLA_HEADS = 4
GLA_DK = 128
GLA_DV = 256
GLA_GATE_RANK = 16
GLA_GATE_NORM = 16.0
GLA_C = 256
GLA_STEP_ROWS = 2 * GLA_C

MOBA_HEADS = 16
MOBA_DH = 64
MOBA_BLOCK = 256
MOBA_TOPK = 3
MOBA_HG = MXU_DIM // MOBA_DH
MOBA_ONES_ROWS = 2 * SUBLANES
MOBA_LOOKAHEAD = 1
LOG2E = math.log2(math.e)
PAGE_SIZE = 128
REL_BUCKETS = 32
REL_MAX_DIST = 128

D_FF = 2816
FF_CHUNK = MXU_DIM
FF_NCHUNK = D_FF // FF_CHUNK

NT_DIMS = (((1,), (1,)), ((), ()))
TN_DIMS = (((0,), (0,)), ((), ()))


def _cparams(sem, **kwargs):
    return pltpu.CompilerParams(dimension_semantics=sem, vmem_limit_bytes=VMEM_LIMIT, **kwargs)


def _resident(w):
    if isinstance(w, tuple):
        arr, layer = w
        nd = arr.ndim - 1
        return pl.BlockSpec((None,) + arr.shape[1:], lambda *_: (layer,) + (0,) * nd,
                            pipeline_mode=pl.Buffered(1))
    nd = w.ndim
    return pl.BlockSpec(w.shape, lambda *_: (0,) * nd, pipeline_mode=pl.Buffered(1))


def _operand(w):
    return w[0] if isinstance(w, tuple) else w


def _rms(x, g):
    return x * lax.rsqrt(jnp.mean(x * x, axis=-1, keepdims=True) + EPS) * g


def _gelu(x):
    return 0.5 * x * (1.0 + lax.erf(x * (1.0 / math.sqrt(2.0))))


def _silu(x):
    return x * jax.nn.sigmoid(x)


def _dot(a, b):
    return jnp.dot(a, b, preferred_element_type=F32)


def _bf16_pieces(x, n):
    pieces = []
    for _ in range(n):
        p = x.astype(BF16)
        pieces.append(p)
        x = x - p.astype(F32)
    return pieces


def _ffn_body(*refs, has_proj, final):
    refs = list(refs)
    x_ref = refs.pop(0)
    a_ref = wo_ref = gf_ref = None
    if has_proj:
        a_ref = refs.pop(0)
        wo_ref = refs.pop(0)
    g_ref = refs.pop(0)
    wup_ref = refs.pop(0)
    wdn_ref = refs.pop(0)
    if final:
        gf_ref = refs.pop(0)
    o_ref = refs.pop(0)

    x = x_ref[...]
    if has_proj:
        x = x + _dot(a_ref[...].astype(BF16), wo_ref[...])
    h = _rms(x, g_ref[...]).astype(BF16)
    o_ref[...] = x

    for c in range(FF_NCHUNK):
        cols = slice(c * FF_CHUNK, (c + 1) * FF_CHUNK)
        gate = _dot(h, wup_ref[:, cols])
        up = _dot(h, wup_ref[:, D_FF + c * FF_CHUNK:D_FF + (c + 1) * FF_CHUNK])
        act = (_silu(gate) * up).astype(BF16)
        o_ref[...] += _dot(act, wdn_ref[cols, :])
    if final:
        o_ref[...] = _rms(o_ref[...], gf_ref[...])


def _ffn(x, g, wup, wdn, tm, a=None, wo=None, gfinal=None):
    rows = x.shape[0]
    row_spec = pl.BlockSpec((tm, D_MODEL), lambda i: (i, 0))
    args = [x]
    specs = [row_spec]
    if a is not None:
        args += [a, wo]
        specs += [row_spec, _resident(wo)]
    args += [g, _operand(wup), _operand(wdn)]
    specs += [_resident(g), _resident(wup), _resident(wdn)]
    if gfinal is not None:
        args.append(gfinal)
        specs.append(_resident(gfinal))
    return pl.pallas_call(
        functools.partial(_ffn_body, has_proj=a is not None, final=gfinal is not None),
        grid=(rows // tm,),
        in_specs=specs,
        out_specs=row_spec,
        out_shape=jax.ShapeDtypeStruct((rows, D_MODEL), F32),
        compiler_params=_cparams(("parallel",)),
        name="ffn",
    )(*args)


def _sgu_body(*refs, tm, emit_v):
    (x_ref, g_ref, win_ref, lng_ref, lnb_ref, wm_ref, sb_ref, wout_ref) = refs[:8]
    if emit_v:
        o_ref, vout_ref, vbuf, ubuf, sbuf = refs[8:]
    else:
        o_ref, vbuf, ubuf, sbuf = refs[8:]
        vout_ref = None
    panel = lambda p, base=0: slice(base + p * A_PANEL, base + (p + 1) * A_PANEL)

    x = x_ref[...]
    h = _rms(x, g_ref[...]).astype(BF16)

    rsum = jnp.zeros((tm, 1), F32)
    for p in range(A_NPANEL):
        v = _gelu(_dot(h, win_ref[:, panel(p, A_HALF)]))
        vbuf[p] = v
        rsum = rsum + jnp.sum(v, axis=-1, keepdims=True)
    for p in range(A_NPANEL):
        ubuf[p] = _gelu(_dot(h, win_ref[:, panel(p)]))
    mean = rsum * (1.0 / A_HALF)
    ssq = jnp.zeros((tm, 1), F32)
    for p in range(A_NPANEL):
        d = vbuf[p] - mean
        ssq = ssq + jnp.sum(d * d, axis=-1, keepdims=True)
    rstd = lax.rsqrt(ssq * (1.0 / A_HALF) + EPS)

    acc = x
    for p in range(A_NPANEL):
        vn = (vbuf[p] - mean) * rstd * lng_ref[:, panel(p)] + lnb_ref[:, panel(p)]
        if emit_v:
            vout_ref[:, p * A_PANEL:(p + 1) * A_PANEL] = vn
        vnb = vn.astype(BF16)
        for c in range(tm // A_CHUNK):
            for gg in range(A_PANEL // A_GROUP_DIM):
                grp = p * (A_PANEL // A_GROUP_DIM) + gg
                rs = slice(c * A_CHUNK, (c + 1) * A_CHUNK)
                cs = slice(gg * A_GROUP_DIM, (gg + 1) * A_GROUP_DIM)
                sbuf[rs, cs] = _dot(wm_ref[grp], vnb[rs, cs]) + sb_ref[grp]
        acc = acc + _dot((ubuf[p] * sbuf[...]).astype(BF16), wout_ref[panel(p), :])
    o_ref[...] = acc


def _sgu(x, g, win, lng, lnb, wm, sb, wout, tm, emit_v):
    rows = x.shape[0]
    row_spec = pl.BlockSpec((tm, D_MODEL), lambda i: (i, 0))
    weights = [g, win, lng, lnb, wm, sb, wout]
    out_shape = [jax.ShapeDtypeStruct((rows, D_MODEL), F32)]
    out_specs = [row_spec]
    if emit_v:
        out_shape.append(jax.ShapeDtypeStruct((rows, A_HALF), F32))
        out_specs.append(pl.BlockSpec((tm, A_HALF), lambda i: (i, 0)))
    return pl.pallas_call(
        functools.partial(_sgu_body, tm=tm, emit_v=emit_v),
        grid=(rows // tm,),
        in_specs=[row_spec] + [_resident(w) for w in weights],
        out_specs=out_specs,
        out_shape=out_shape,
        scratch_shapes=[pltpu.VMEM((A_NPANEL, tm, A_PANEL), F32),
                        pltpu.VMEM((A_NPANEL, tm, A_PANEL), F32),
                        pltpu.VMEM((tm, A_PANEL), F32)],
        compiler_params=_cparams(("parallel",)),
        name="sgu",
    )(x, *[_operand(w) for w in weights])


def _log_sigmoid(z):
    return jnp.minimum(z, 0.0) - jnp.log1p(jnp.exp(-jnp.abs(z)))


def _gla_intra(qh, kh, cum, mask_ref):
    c = qh.shape[0]
    r = lax.broadcasted_iota(jnp.int32, (c, 1), 0)
    a = jnp.zeros((c, c), F32)
    m, level = c, 0
    while m >= 2:
        half = m // 2
        pos = r & (m - 1)
        upper = pos >= half
        if m >= 2 * SUBLANES:
            pieces = []
            for b in range(c // m):
                row = b * m + half - 1
                pieces.append(jnp.broadcast_to(cum[row:row + 1, :], (m, cum.shape[1])))
            ref = pieces[0] if len(pieces) == 1 else jnp.concatenate(pieces, axis=0)
        else:
            c8 = cum.reshape(c // SUBLANES, SUBLANES, cum.shape[1])
            sub = lax.broadcasted_iota(jnp.int32, (1, SUBLANES, 1), 1)
            ref = None
            for b in reversed(range(SUBLANES // m)):
                row = jnp.broadcast_to(c8[:, b * m + half - 1:b * m + half, :], c8.shape)
                ref = row if ref is None else jnp.where(sub < (b + 1) * m, row, ref)
            ref = ref.reshape(c, cum.shape[1])
        decay = jnp.exp(jnp.where(upper, cum - ref, ref - cum))
        x = (jnp.where(upper, qh, kh) * decay).astype(BF16)
        a = a + lax.dot_general(x, x, NT_DIMS, preferred_element_type=F32) * mask_ref[level]
        m = half
        level += 1
    return a


def _gla_level_masks(c):
    rowi = lax.broadcasted_iota(jnp.int32, (c, c), 0)
    coli = lax.broadcasted_iota(jnp.int32, (c, c), 1)
    masks = []
    m = c
    while m >= 2:
        sh, half = int(math.log2(m)), m // 2
        keep = ((rowi >> sh) == (coli >> sh)) & ((rowi & (m - 1)) >= half) & ((coli & (m - 1)) < half)
        masks.append(jnp.where(keep, 1.0, 0.0))
        m = half
    return masks


def _gla_project(h, win_ref, wgl_ref, wgate_ref, bgate_ref):
    dq, dv = GLA_HEADS * GLA_DK, GLA_HEADS * GLA_DV
    q = _dot(h, win_ref[:, :dq]) * (GLA_DK ** -0.5)
    k = _dot(h, win_ref[:, dq:2 * dq])
    v = _dot(h, win_ref[:, 2 * dq:2 * dq + dv])
    g = _dot(h, win_ref[:, 2 * dq + dv:2 * dq + 2 * dv])
    gl = _dot(h, wgl_ref[...])
    z = _dot(gl.astype(BF16), wgate_ref[...]) + bgate_ref[...]
    return q, k, v, g, z


def _gla_prompt_body(x_ref, gm_ref, win_ref, wgl_ref, wgate_ref, bgate_ref,
                     gn_ref, wo_ref, o_ref, st_ref, s_scr, mask_scr):
    ci = pl.program_id(1)
    c = GLA_C

    @pl.when((pl.program_id(0) == 0) & (ci == 0))
    def _():
        for level, mask in enumerate(_gla_level_masks(c)):
            mask_scr[level] = mask

    @pl.when(ci == 0)
    def _():
        s_scr[...] = jnp.zeros_like(s_scr)

    x = x_ref[...]
    h = _rms(x, gm_ref[...]).astype(BF16)
    q, k, v, g, z = _gla_project(h, win_ref, wgl_ref, wgate_ref, bgate_ref)
    la = _log_sigmoid(z) * (1.0 / GLA_GATE_NORM)

    rowi = lax.broadcasted_iota(jnp.int32, (c, c), 0)
    coli = lax.broadcasted_iota(jnp.int32, (c, c), 1)
    tril = (coli <= rowi).astype(BF16)

    ys = []
    for sc in range(x.shape[0] // c):
        rs = slice(sc * c, (sc + 1) * c)
        cum = functools.reduce(lambda x, y: x + y,
                               [_dot(tril, piece) for piece in _bf16_pieces(la[rs], 3)])
        outs = []
        for hd in range(GLA_HEADS):
            ks = slice(hd * GLA_DK, (hd + 1) * GLA_DK)
            vs = slice(hd * GLA_DV, (hd + 1) * GLA_DV)
            qh, kh, cumh, vh = q[rs, ks], k[rs, ks], cum[:, ks], v[rs, vs]
            vhb = vh.astype(BF16)
            total = cumh[c - 1:c, :]
            st = s_scr[hd]
            o = lax.dot_general((qh * jnp.exp(cumh)).astype(BF16), st.astype(BF16), NT_DIMS,
                                preferred_element_type=F32)
            a = _gla_intra(qh, kh, cumh, mask_scr)
            o = o + _dot(a.astype(BF16), vhb)
            o = o + jnp.sum(qh * kh, axis=-1, keepdims=True) * vh
            kdec = (kh * jnp.exp(total - cumh)).astype(BF16)
            st_new = st * jnp.exp(total) + lax.dot_general(vhb, kdec, TN_DIMS,
                                                           preferred_element_type=F32)
            s_scr[hd] = st_new
            o = o * lax.rsqrt(jnp.mean(o * o, axis=-1, keepdims=True) + EPS) * gn_ref[...]
            outs.append(o)
        ys.append(jnp.concatenate(outs, axis=1))
    y = jnp.concatenate(ys, axis=0) * _silu(g)
    o_ref[...] = x + _dot(y.astype(BF16), wo_ref[...])

    @pl.when(ci == pl.num_programs(1) - 1)
    def _():
        for hd in range(GLA_HEADS):
            st_ref[0, hd] = s_scr[hd].T


def _gla_prompt(x, gm, win, wgl, wgate, bgate, gn, wo, n_seq, seq_len):
    n_chunk = seq_len // GLA_STEP_ROWS
    weights = [gm, win, wgl, wgate, bgate, gn, wo]
    row_spec = pl.BlockSpec((GLA_STEP_ROWS, D_MODEL), lambda b, c: (b * n_chunk + c, 0))
    return pl.pallas_call(
        _gla_prompt_body,
        grid=(n_seq, n_chunk),
        in_specs=[row_spec] + [_resident(w) for w in weights],
        out_specs=[row_spec,
                   pl.BlockSpec((1, GLA_HEADS, GLA_DK, GLA_DV), lambda b, c: (b, 0, 0, 0))],
        out_shape=[jax.ShapeDtypeStruct(x.shape, F32),
                   jax.ShapeDtypeStruct((n_seq, GLA_HEADS, GLA_DK, GLA_DV), F32)],
        scratch_shapes=[pltpu.VMEM((GLA_HEADS, GLA_DV, GLA_DK), F32),
                        pltpu.VMEM((int(math.log2(GLA_C)), GLA_C, GLA_C), F32)],
        compiler_params=_cparams(("arbitrary", "arbitrary")),
        name="gla_prompt",
    )(x, *weights)


def _gla_sample_proj_body(x_ref, gm_ref, win_ref, wgl_ref, wgate_ref, bgate_ref,
                          q_ref, k_ref, a_ref, v_ref, g_ref):
    h = _rms(x_ref[...], gm_ref[...]).astype(BF16)
    q_ref[...], k_ref[...], v_ref[...], g_ref[...], z = _gla_project(
        h, win_ref, wgl_ref, wgate_ref, bgate_ref)
    a_ref[...] = jnp.exp(_log_sigmoid(z) * (1.0 / GLA_GATE_NORM))


GLA_SEQ_PER_STEP = SUBLANES


def _gla_sample_rec_body(s_ref, qc_ref, kc_ref, ac_ref, v_ref, so_ref, o_ref):
    for j in range(GLA_SEQ_PER_STEP):
        for hd in range(GLA_HEADS):
            ks = slice(hd * GLA_DK, (hd + 1) * GLA_DK)
            vs = slice(hd * GLA_DV, (hd + 1) * GLA_DV)
            acol = ac_ref[0, ks, j:j + 1]
            kcol = kc_ref[0, ks, j:j + 1]
            qcol = qc_ref[0, ks, j:j + 1]
            vrow = v_ref[j:j + 1, vs]
            s_new = s_ref[j, hd] * acol + kcol * vrow
            so_ref[j, hd] = s_new
            o_ref[j:j + 1, vs] = jnp.sum(s_new * qcol, axis=0, keepdims=True)


def _gla_sample_out_body(x_ref, o_ref, g_ref, gn_ref, wo_ref, y_ref):
    outs = []
    for hd in range(GLA_HEADS):
        o = o_ref[:, hd * GLA_DV:(hd + 1) * GLA_DV]
        outs.append(o * lax.rsqrt(jnp.mean(o * o, axis=-1, keepdims=True) + EPS) * gn_ref[...])
    y = jnp.concatenate(outs, axis=1) * _silu(g_ref[...])
    y_ref[...] = x_ref[...] + _dot(y.astype(BF16), wo_ref[...])


def _gla_sample(x, state, gm, win, wgl, wgate, bgate, gn, wo):
    n = x.shape[0]
    dq = GLA_HEADS * GLA_DK
    weights = [gm, win, wgl, wgate, bgate]
    q, k, a, v, g = pl.pallas_call(
        _gla_sample_proj_body,
        out_shape=[jax.ShapeDtypeStruct((n, dq), F32)] * 3
        + [jax.ShapeDtypeStruct((n, D_MODEL), F32)] * 2,
        compiler_params=pltpu.CompilerParams(vmem_limit_bytes=VMEM_LIMIT),
        name="gla_sample_proj",
    )(x, *weights)

    nstep = n // GLA_SEQ_PER_STEP

    def cols(t):
        return t.reshape(nstep, GLA_SEQ_PER_STEP, dq).transpose(0, 2, 1)

    col_spec = pl.BlockSpec((1, dq, GLA_SEQ_PER_STEP), lambda i: (i, 0, 0))
    st_spec = pl.BlockSpec((GLA_SEQ_PER_STEP, GLA_HEADS, GLA_DK, GLA_DV), lambda i: (i, 0, 0, 0))
    row_spec = pl.BlockSpec((GLA_SEQ_PER_STEP, D_MODEL), lambda i: (i, 0))
    s_new, o = pl.pallas_call(
        _gla_sample_rec_body,
        grid=(nstep,),
        in_specs=[st_spec, col_spec, col_spec, col_spec, row_spec],
        out_specs=[st_spec, row_spec],
        out_shape=[jax.ShapeDtypeStruct(state.shape, F32), jax.ShapeDtypeStruct((n, D_MODEL), F32)],
        compiler_params=_cparams(("parallel",)),
        name="gla_sample_rec",
    )(state, cols(q), cols(k), cols(a), v)

    y = pl.pallas_call(
        _gla_sample_out_body,
        out_shape=jax.ShapeDtypeStruct((n, D_MODEL), F32),
        compiler_params=pltpu.CompilerParams(vmem_limit_bytes=VMEM_LIMIT),
        name="gla_sample_out",
    )(x, o, g, gn, wo)
    return y, s_new


def _t5_bucket_np(dist):
    max_exact = REL_BUCKETS // 2
    n = np.maximum(dist, 0)
    nf = np.maximum(n, max_exact).astype(np.float32)
    large = max_exact + (np.log(nf / np.float32(max_exact)) / np.float32(math.log(REL_MAX_DIST / max_exact))
                         * np.float32(REL_BUCKETS - max_exact)).astype(np.int32)
    return np.where(n < max_exact, n, np.minimum(large, REL_BUCKETS - 1)).astype(np.int32)


def _bias_tables_body(rb_ref, bd_ref, b1_ref, bdec_ref, od_ref, o1_ref, odec_ref):
    h = pl.program_id(0)
    td, t1, tdec = bd_ref[...], b1_ref[...], bdec_ref[...]
    ad = jnp.zeros(td.shape, F32)
    a1 = jnp.zeros(t1.shape, F32)
    adec = jnp.zeros(tdec.shape, F32)
    for b in range(REL_BUCKETS):
        val = rb_ref[b, h]
        ad = jnp.where(td == b, val, ad)
        a1 = jnp.where(t1 == b, val, a1)
        adec = jnp.where(tdec == b, val, adec)
    od_ref[0] = ad * LOG2E
    o1_ref[0] = a1 * LOG2E
    odec_ref[0] = adec


def _bias_tables(rel_bias, past_len):
    kc = np.arange(MOBA_BLOCK)[:, None]
    qr = np.arange(MOBA_BLOCK)[None, :]
    bd = _t5_bucket_np(qr - kc)
    b1 = _t5_bucket_np(MOBA_BLOCK + qr - kc)
    nblk = past_len // MOBA_BLOCK
    kpos = np.arange(nblk)[:, None] * MOBA_BLOCK + np.arange(MOBA_BLOCK)[None, :]
    bdec = _t5_bucket_np(past_len - kpos)
    full = lambda a: pl.BlockSpec(a.shape, lambda h: (0,) * a.ndim)
    return pl.pallas_call(
        _bias_tables_body,
        grid=(MOBA_HEADS,),
        in_specs=[pl.BlockSpec(memory_space=pltpu.SMEM), full(bd), full(b1), full(bdec)],
        out_specs=[pl.BlockSpec((1,) + bd.shape, lambda h: (h, 0, 0)),
                   pl.BlockSpec((1,) + b1.shape, lambda h: (h, 0, 0)),
                   pl.BlockSpec((1,) + bdec.shape, lambda h: (h, 0, 0))],
        out_shape=[jax.ShapeDtypeStruct((MOBA_HEADS,) + bd.shape, F32),
                   jax.ShapeDtypeStruct((MOBA_HEADS,) + b1.shape, F32),
                   jax.ShapeDtypeStruct((MOBA_HEADS,) + bdec.shape, F32)],
        compiler_params=_cparams(("parallel",)),
        name="moba_bias_tables",
    )(rel_bias, jnp.asarray(bd), jnp.asarray(b1), jnp.asarray(bdec))


def _moba_qkv_body(x_ref, g_ref, w_ref, q_ref, k_ref, v_ref, *t_refs):
    h = _rms(x_ref[...], g_ref[...]).astype(BF16)
    qkv = _dot(h, w_ref[...])
    q_ref[...] = qkv[:, :D_MODEL] * (MOBA_DH ** -0.5)
    k = qkv[:, D_MODEL:2 * D_MODEL]
    v = qkv[:, 2 * D_MODEL:]
    k_ref[...] = k
    v_ref[...] = v
    if t_refs:
        t_refs[0][0] = k.T
        t_refs[1][0] = v.T


def _moba_qkv(x, g, w, tm, seq_len=None):
    rows = x.shape[0]
    row_spec = pl.BlockSpec((tm, D_MODEL), lambda i: (i, 0))
    out_specs = [row_spec] * 3
    out_shape = [jax.ShapeDtypeStruct((rows, D_MODEL), F32)] * 3
    if seq_len is not None:
        per_seq = seq_len // tm
        t_spec = pl.BlockSpec((1, D_MODEL, tm), lambda i: (i // per_seq, 0, i % per_seq))
        out_specs += [t_spec] * 2
        out_shape += [jax.ShapeDtypeStruct((rows // seq_len, D_MODEL, seq_len), F32)] * 2
    return pl.pallas_call(
        _moba_qkv_body,
        grid=(rows // tm,),
        in_specs=[row_spec, _resident(g), _resident(w)],
        out_specs=out_specs,
        out_shape=out_shape,
        compiler_params=_cparams(("parallel",)),
        name="moba_qkv",
    )(x, g, w)


def _topk_select(gate, valid, axis):
    n = gate.shape[axis]
    idx = lax.broadcasted_iota(jnp.int32, gate.shape, axis)
    cnt = jnp.zeros(gate.shape, jnp.int32)
    for m in range(n):
        gm = lax.slice_in_dim(gate, m, m + 1, axis=axis)
        beats = (gm > gate) | ((gm == gate) & (m < idx))
        cnt = cnt + jnp.where(beats, valid(m), 0)
    return jnp.where(cnt < MOBA_TOPK, 1.0, 0.0)


def _moba_prompt_body(rb_ref, q_ref, k_ref, v_ref, bd_ref, b1_ref, o_ref,
                      kb_scr, vt_scr, km_scr):
    hg = pl.program_id(1)
    qi = pl.program_id(2)
    nblk = kb_scr.shape[0]
    blk = MOBA_BLOCK
    dh = MOBA_DH
    width = MOBA_HG * dh

    lane = lax.broadcasted_iota(jnp.int32, (1, width), 1)
    in_head = [(lane >= hh * dh) & (lane < (hh + 1) * dh) for hh in range(MOBA_HG)]

    @pl.when(qi == 0)
    def _():
        ones = jnp.ones((MOBA_ONES_ROWS, blk), BF16)
        means = []
        for n in range(nblk):
            kt = k_ref[n * blk:(n + 1) * blk, :]
            kb_scr[n] = kt.astype(BF16)
            means.append(jnp.mean(kt, axis=0, keepdims=True))
            vt = v_ref[n * blk:(n + 1) * blk, :].T.astype(BF16)
            for hh in range(MOBA_HG):
                vt_scr[n, hh, :dh, :] = vt[hh * dh:(hh + 1) * dh, :]
                vt_scr[n, hh, dh:, :] = ones
        km = jnp.concatenate(means, axis=0)
        per_head = [jnp.where(in_head[hh], km, 0.0) for hh in range(MOBA_HG)]
        pad = jnp.zeros((LANES - MOBA_HG * nblk, width), F32)
        km_t = jnp.concatenate(per_head + [pad], axis=0).T
        km_scr[0], km_scr[1] = _bf16_pieces(km_t, 2)

    q2 = q_ref[...]
    ki_idx = lax.broadcasted_iota(jnp.int32, (blk, blk), 0)
    qi_idx = lax.broadcasted_iota(jnp.int32, (blk, blk), 1)
    blk_idx = lax.broadcasted_iota(jnp.int32, (nblk, blk), 0)

    hs = range(MOBA_HG)
    qhb, far_bias, neg, tile_of = [], [], [], []
    q_hi, q_lo = _bf16_pieces(q2, 2)
    gates = (_dot(q_hi, km_scr[0]) + _dot(q_hi, km_scr[1]) + _dot(q_lo, km_scr[0])).T
    for hh in hs:
        tile_of.append(slice(hh * dh // LANES * LANES, (hh * dh // LANES + 1) * LANES))
        qhb.append((jnp.where(in_head[hh], q2, 0.0)[:, tile_of[hh]] * LOG2E).astype(BF16))
        far_bias.append(rb_ref[REL_BUCKETS - 1, MOBA_HG * hg + hh] * LOG2E)
        gate = gates[hh * nblk:(hh + 1) * nblk, :]
        keep = _topk_select(gate, lambda m: (m < qi).astype(jnp.int32), 0)
        neg.append(jnp.where((keep > 0.0) & (blk_idx < qi), 0.0, NEG))

    def scores(k0, n):
        kt = kb_scr[k0:k0 + n].reshape(n * blk, width)
        return [lax.dot_general(kt[:, tile_of[hh]], qhb[hh], NT_DIMS, preferred_element_type=F32)
                for hh in hs]

    def values(ki, p):
        return [_dot(vt_scr[ki, hh], p[hh]) for hh in hs]

    def weights(x):
        return jnp.exp2(x.astype(BF16))

    def step(c):
        units = [(c - 1, 2)] if c >= 1 else [(0, 1)]
        t = 0
        while t < c - 1:
            n = min(2, c - 1 - t)
            units.append((t, n))
            t += n
        m = [None] * MOBA_HG
        acc = [None] * MOBA_HG
        ahead = [scores(*u) for u in units[:MOBA_LOOKAHEAD]]
        for ui, (k0, n) in enumerate(units):
            s = ahead.pop(0)
            if ui + MOBA_LOOKAHEAD < len(units):
                ahead.append(scores(*units[ui + MOBA_LOOKAHEAD]))
            p = [[] for _ in range(n)]
            alpha = [None] * MOBA_HG
            for hh in hs:
                sh, rows, tile_max = [], [], []
                for t in range(n):
                    kb = k0 + t
                    x = s[hh][t * blk:(t + 1) * blk, :]
                    if kb == c:
                        x, r = jnp.where(ki_idx <= qi_idx, x + bd_ref[hh], NEG), None
                    elif kb == c - 1:
                        x, r = x + b1_ref[hh], neg[hh][kb:kb + 1, :]
                    else:
                        r = neg[hh][kb:kb + 1, :] + far_bias[hh]
                    mx = jnp.max(x, axis=0, keepdims=True)
                    sh.append(x)
                    rows.append(r)
                    tile_max.append(mx if r is None else mx + r)
                m_new = functools.reduce(jnp.maximum, tile_max if m[hh] is None else [m[hh]] + tile_max)
                for t in range(n):
                    p[t].append(weights(sh[t] - (m_new if rows[t] is None else m_new - rows[t])))
                if m[hh] is not None:
                    alpha[hh] = jnp.exp2(m[hh] - m_new)
                m[hh] = m_new
            pv = [values(k0 + t, p[t]) for t in range(n)]
            for hh in hs:
                tot = functools.reduce(lambda x, y: x + y, [pv[t][hh] for t in range(n)])
                acc[hh] = tot if acc[hh] is None else alpha[hh] * acc[hh] + tot
        ot = jnp.concatenate([acc[hh][:dh, :] / acc[hh][dh:dh + 1, :] for hh in hs], axis=0)
        o_ref[...] = ot.T

    for c in range(nblk):
        pl.when(qi == c)(functools.partial(step, c))


def _moba_prompt(rel_bias, q, k, v, bias_d, bias_1, n_seq, seq_len):
    nblk = seq_len // MOBA_BLOCK
    hgrp = MOBA_HG
    width = hgrp * MOBA_DH
    q_spec = pl.BlockSpec((MOBA_BLOCK, width), lambda b, hg, qi: (b * nblk + qi, hg))
    kv_spec = pl.BlockSpec((seq_len, width), lambda b, hg, qi: (b, hg))
    bias_spec = pl.BlockSpec((hgrp, MOBA_BLOCK, MOBA_BLOCK), lambda b, hg, qi: (hg, 0, 0))
    return pl.pallas_call(
        _moba_prompt_body,
        grid=(n_seq, MOBA_HEADS // hgrp, nblk),
        in_specs=[pl.BlockSpec(memory_space=pltpu.SMEM), q_spec, kv_spec, kv_spec, bias_spec, bias_spec],
        out_specs=q_spec,
        out_shape=jax.ShapeDtypeStruct(q.shape, F32),
        scratch_shapes=[
            pltpu.VMEM((nblk, MOBA_BLOCK, width), BF16),
            pltpu.VMEM((nblk, hgrp, MOBA_DH + MOBA_ONES_ROWS, MOBA_BLOCK), BF16),
            pltpu.VMEM((2, width, LANES), BF16),
        ],
        compiler_params=_cparams(("parallel", "parallel", "arbitrary")),
        name="moba_prompt",
    )(rel_bias, q, k, v, bias_d, bias_1)


def _col_to_row(col):
    n = col.shape[0]
    r = lax.broadcasted_iota(jnp.int32, (n, n), 0)
    c = lax.broadcasted_iota(jnp.int32, (n, n), 1)
    return jnp.sum(jnp.where(r == c, jnp.broadcast_to(col, (n, n)), 0.0), axis=0, keepdims=True)


def _moba_decode_body(pt_ref, q_ref, qt_ref, kn_ref, vnt_ref, rb0_ref, bdec_ref, *refs, npg):
    del pt_ref
    ck, cv = refs[:npg], refs[npg:2 * npg]
    o_ref, qb_scr, s_scr, p_scr, t_scr = refs[2 * npg:]
    heads, dh = MOBA_HEADS, MOBA_DH
    ppb = MOBA_BLOCK // PAGE_SIZE
    nb = npg // ppb

    qt = qt_ref[0]
    for h in range(heads):
        qb_scr[h] = jnp.broadcast_to(qt[:, h:h + 1], (dh, PAGE_SIZE))

    def k_step(h, carry):
        qb = qb_scr[h]
        for pg in range(npg):
            s_scr[pg, pl.ds(h, 1), :] = jnp.sum(ck[pg][0, h] * qb, axis=0, keepdims=True)
        return carry

    lax.fori_loop(0, heads, k_step, 0, unroll=2)

    z, m_blk = [], []
    blk_lane = lax.broadcasted_iota(jnp.int32, (heads, nb), 1)
    gate = jnp.zeros((heads, nb), F32)
    for b in range(nb):
        zb = []
        g = jnp.zeros((heads, 1), F32)
        for j in range(ppb):
            s = s_scr[ppb * b + j]
            g = g + jnp.sum(s, axis=-1, keepdims=True)
            zb.append(s + bdec_ref[ppb * b + j])
        gate = jnp.where(blk_lane == b, g, gate)
        m_blk.append(functools.reduce(jnp.maximum, [jnp.max(t, axis=-1, keepdims=True) for t in zb]))
        z += zb
    keep = _topk_select(gate, lambda m_: 1, 1)

    l_new = jnp.sum(q_ref[0] * kn_ref[0], axis=-1, keepdims=True) + rb0_ref[...]
    m_all = l_new
    for b in range(nb):
        m_all = jnp.maximum(m_all, jnp.where(keep[:, b:b + 1] > 0.0, m_blk[b], NEG))
    w_new = jnp.exp(l_new - m_all)
    den = w_new
    for b in range(nb):
        kept = keep[:, b:b + 1] > 0.0
        for j in range(ppb):
            p = jnp.exp(jnp.where(kept, z[ppb * b + j] - m_all, NEG))
            den = den + jnp.sum(p, axis=-1, keepdims=True)
            p_scr[ppb * b + j] = p

    def v_step(h, carry):
        acc = cv[0][0, h] * p_scr[0, pl.ds(h, 1), :]
        for pg in range(1, npg):
            acc = acc + cv[pg][0, h] * p_scr[pg, pl.ds(h, 1), :]
        t_scr[h] = jnp.broadcast_to(jnp.sum(acc, axis=-1, keepdims=True), (dh, PAGE_SIZE))
        return carry

    lax.fori_loop(0, heads, v_step, 0, unroll=4)

    head_lane = lax.broadcasted_iota(jnp.int32, (dh, heads), 1)
    tot = jnp.zeros((dh, heads), F32)
    for h in range(heads):
        tot = jnp.where(head_lane == h, t_scr[h][:, :heads], tot)
    o_ref[0] = (tot + vnt_ref[0] * _col_to_row(w_new)) / _col_to_row(den)


def _moba_decode(page_table, q, k_new, v_new, rb0, cache_k, cache_v, bias_dec):
    n_dec, n_pages = page_table.shape
    heads, dh = MOBA_HEADS, MOBA_DH
    rows = lambda t: t.reshape(n_dec, heads, dh)
    cols = lambda t: t.reshape(n_dec, heads, dh).transpose(0, 2, 1)
    row_spec = pl.BlockSpec((1, heads, dh), lambda b, pt: (b, 0, 0))
    col_spec = pl.BlockSpec((1, dh, heads), lambda b, pt: (b, 0, 0))
    page_specs = [pl.BlockSpec((1, heads, dh, PAGE_SIZE), lambda b, pt, j=j: (pt[b, j], 0, 0, 0))
                  for j in range(n_pages)]
    grid_spec = pltpu.PrefetchScalarGridSpec(
        num_scalar_prefetch=1,
        grid=(n_dec,),
        in_specs=[row_spec, col_spec, row_spec, col_spec,
                  pl.BlockSpec((heads, 1), lambda b, pt: (0, 0)),
                  pl.BlockSpec(bias_dec.shape, lambda b, pt: (0, 0, 0))] + page_specs + page_specs,
        out_specs=col_spec,
        scratch_shapes=[pltpu.VMEM((heads, dh, PAGE_SIZE), F32),
                        pltpu.VMEM((n_pages, heads, PAGE_SIZE), F32),
                        pltpu.VMEM((n_pages, heads, PAGE_SIZE), F32),
                        pltpu.VMEM((heads, dh, PAGE_SIZE), F32)],
    )
    out = pl.pallas_call(
        functools.partial(_moba_decode_body, npg=n_pages),
        grid_spec=grid_spec,
        out_shape=jax.ShapeDtypeStruct((n_dec, dh, heads), F32),
        compiler_params=_cparams(("parallel",)),
        name="moba_decode",
    )(page_table, rows(q), cols(q), rows(k_new), cols(v_new), rb0, bias_dec,
      *([cache_k] * n_pages), *([cache_v] * n_pages))
    return out.transpose(0, 2, 1).reshape(n_dec, heads * dh)


def _prep_sgu(ln_g, ln_b, w_s, b_s):
    lng = ln_g.reshape(1, A_HALF)
    lnb = ln_b.reshape(1, A_HALF)
    causal = jnp.tril(jnp.ones((A_CHUNK, A_CHUNK), dtype=bool))
    wm_p = jnp.where(causal[None], w_s, 0.0).astype(BF16)
    sb_p = jnp.broadcast_to(b_s[:, :, None], (A_GROUPS, A_CHUNK, A_GROUP_DIM))
    eye = jnp.eye(A_CHUNK, dtype=F32)
    wm_s = (w_s[:, 0, 0][:, None, None] * eye[None]).astype(BF16)
    sb_s = jnp.broadcast_to(b_s[:, 0][:, None, None], (A_GROUPS, A_CHUNK, A_GROUP_DIM))
    return (lng, lnb), (wm_p, sb_p), (wm_s, sb_s)


def _prep_gla(w_in, w_gate, b_gate, gn, w_out):
    dq, dv = GLA_HEADS * GLA_DK, GLA_HEADS * GLA_DV
    pad = LANES - GLA_GATE_RANK
    wgl = jnp.pad(w_in[:, 2 * dq + 2 * dv:], ((0, 0), (0, pad))).astype(BF16)
    wgate = jnp.pad(w_gate, ((0, pad), (0, 0))).astype(BF16)
    return (w_in.astype(BF16), wgl, wgate, b_gate.reshape(1, dq), gn.reshape(1, GLA_DV),
            w_out.astype(BF16))


def kernel(x_prompt, x_sample, cache_k, cache_v, page_table, state_gla, norm_mix, norm_ffn, norm_final,
           w_in_a, ln_a_g, ln_a_b, w_s_a, b_s_a, w_out_a, w_in_b, w_gate_b, b_gate_b, gn_b, w_out_b,
           w_in_c, w_out_c, rel_bias, w_up, w_down):
    n_pr, l_pr, _ = x_prompt.shape
    n_dec, l_dec, _ = x_sample.shape
    assert l_dec == 1 and l_pr % MOBA_BLOCK == 0 and l_pr % GLA_STEP_ROWS == 0
    depth = norm_mix.shape[0]
    past_len = page_table.shape[1] * PAGE_SIZE
    assert past_len % MOBA_BLOCK == 0
    tm_p, tm_s = 512, n_dec
    tm_ffn = 1024

    xp = x_prompt.reshape(n_pr * l_pr, D_MODEL)
    xs = x_sample.reshape(n_dec, D_MODEL)
    gfin = norm_final.reshape(1, D_MODEL)
    w_up_b, w_down_b = w_up.astype(BF16), w_down.astype(BF16)
    w_in_a_b, w_out_a_b = w_in_a.astype(BF16), w_out_a.astype(BF16)
    k_p, v_p, k_s, v_s, gla_p, gla_s, sgu_s = [], [], [], [], [], [], []

    for i in range(depth):
        kind, j = i % 3, i // 3
        gm = norm_mix[i].reshape(1, D_MODEL)
        gf = norm_ffn[i].reshape(1, D_MODEL)
        wup, wdn = (w_up_b, i), (w_down_b, i)
        last = gfin if i == depth - 1 else None
        proj_p = proj_s = None
        if kind == 0:
            (lng, lnb), mode_p, mode_s = _prep_sgu(ln_a_g[j], ln_a_b[j], w_s_a[j], b_s_a[j])
            win, wout = (w_in_a_b, j), (w_out_a_b, j)
            (xp,) = _sgu(xp, gm, win, lng, lnb, *mode_p, wout, tm_p, False)
            xs, v_rows = _sgu(xs, gm, win, lng, lnb, *mode_s, wout, tm_s, True)
            sgu_s.append(v_rows.reshape(n_dec, l_dec, A_HALF))
        elif kind == 1:
            wts = _prep_gla(w_in_b[j], w_gate_b[j], b_gate_b[j], gn_b[j], w_out_b[j])
            xp, sp = _gla_prompt(xp, gm, *wts, n_pr, l_pr)
            xs, ss = _gla_sample(xs, state_gla[j], gm, *wts)
            gla_p.append(sp)
            gla_s.append(ss)
        else:
            w_in = w_in_c[j].astype(BF16)
            bias_d, bias_1, bias_dec = _bias_tables(rel_bias, past_len)
            qp, kp, vp, kpt, vpt = _moba_qkv(xp, gm, w_in, tm_p, seq_len=l_pr)
            op = _moba_prompt(rel_bias, qp, kp, vp, bias_d, bias_1, n_pr, l_pr)
            qs, kn, vn = _moba_qkv(xs, gm, w_in, tm_s)
            os_ = _moba_decode(page_table, qs, kn, vn, rel_bias[0].reshape(MOBA_HEADS, 1),
                               cache_k[j].transpose(0, 2, 3, 1), cache_v[j].transpose(0, 2, 3, 1),
                               bias_dec.reshape(MOBA_HEADS, -1, PAGE_SIZE).transpose(1, 0, 2))
            wo_c = w_out_c[j].astype(BF16)
            proj_p, proj_s = (op, wo_c), (os_, wo_c)
            heads_last = lambda t: t.reshape(n_pr, MOBA_HEADS, MOBA_DH, l_pr).transpose(0, 3, 1, 2)
            k_p.append(heads_last(kpt))
            v_p.append(heads_last(vpt))
            k_s.append(kn.reshape(n_dec, l_dec, MOBA_HEADS, MOBA_DH))
            v_s.append(vn.reshape(n_dec, l_dec, MOBA_HEADS, MOBA_DH))
        if proj_p is None:
            xp = _ffn(xp, gf, wup, wdn, tm_ffn, gfinal=last)
            xs = _ffn(xs, gf, wup, wdn, tm_s, gfinal=last)
        else:
            xp = _ffn(xp, gf, wup, wdn, tm_ffn, a=proj_p[0], wo=proj_p[1], gfinal=last)
            xs = _ffn(xs, gf, wup, wdn, tm_s, a=proj_s[0], wo=proj_s[1], gfinal=last)

    return (xp.reshape(n_pr, l_pr, D_MODEL), xs.reshape(n_dec, l_dec, D_MODEL),
            jnp.stack(k_p), jnp.stack(v_p), jnp.stack(k_s), jnp.stack(v_s),
            jnp.stack(gla_p), jnp.stack(gla_s), jnp.stack(sgu_s))
```

```python
import functools
import math

import numpy as np
import jax
import jax.numpy as jnp
from jax import lax
from jax.experimental import pallas as pl
from jax.experimental.pallas import tpu as pltpu

F32 = jnp.float32
BF16 = jnp.bfloat16

D_MODEL = 1024
EPS = 1e-6
NEG = -1e30

LANES = 128
SUBLANES = 8
MXU_DIM = 256
VMEM_LIMIT = 56 * 1024 * 1024

A_CHUNK = 128
A_HALF = 3 * D_MODEL
A_GROUPS = 8
A_GROUP_DIM = A_HALF // A_GROUPS
A_PANEL = 4 * A_GROUP_DIM
A_NPANEL = A_HALF // A_PANEL

GLA_HEADS = 4
GLA_DK = 128
GLA_DV = 256
GLA_GATE_RANK = 16
GLA_GATE_NORM = 16.0
GLA_C = 256
GLA_STEP_ROWS = 2 * GLA_C

MOBA_HEADS = 16
MOBA_DH = 64
MOBA_BLOCK = 256
MOBA_TOPK = 3
MOBA_HG = MXU_DIM // MOBA_DH
MOBA_ONES_ROWS = 2 * SUBLANES
MOBA_LOOKAHEAD = 1
LOG2E = math.log2(math.e)
PAGE_SIZE = 128
REL_BUCKETS = 32
REL_MAX_DIST = 128

D_FF = 2816
FF_CHUNK = MXU_DIM
FF_NCHUNK = D_FF // FF_CHUNK

NT_DIMS = (((1,), (1,)), ((), ()))
TN_DIMS = (((0,), (0,)), ((), ()))


def _cparams(sem, **kwargs):
    return pltpu.CompilerParams(dimension_semantics=sem, vmem_limit_bytes=VMEM_LIMIT, **kwargs)


def _resident(w):
    if isinstance(w, tuple):
        arr, layer = w
        nd = arr.ndim - 1
        return pl.BlockSpec((None,) + arr.shape[1:], lambda *_: (layer,) + (0,) * nd,
                            pipeline_mode=pl.Buffered(1))
    nd = w.ndim
    return pl.BlockSpec(w.shape, lambda *_: (0,) * nd, pipeline_mode=pl.Buffered(1))


def _operand(w):
    return w[0] if isinstance(w, tuple) else w


def _rms(x, g):
    return x * lax.rsqrt(jnp.mean(x * x, axis=-1, keepdims=True) + EPS) * g


def _gelu(x):
    return 0.5 * x * (1.0 + lax.erf(x * (1.0 / math.sqrt(2.0))))


def _silu(x):
    return x * jax.nn.sigmoid(x)


def _dot(a, b):
    return jnp.dot(a, b, preferred_element_type=F32)


def _bf16_pieces(x, n):
    pieces = []
    for _ in range(n):
        p = x.astype(BF16)
        pieces.append(p)
        x = x - p.astype(F32)
    return pieces


def _ffn_body(*refs, has_proj, final):
    refs = list(refs)
    x_ref = refs.pop(0)
    a_ref = wo_ref = gf_ref = None
    if has_proj:
        a_ref = refs.pop(0)
        wo_ref = refs.pop(0)
    g_ref = refs.pop(0)
    wup_ref = refs.pop(0)
    wdn_ref = refs.pop(0)
    if final:
        gf_ref = refs.pop(0)
    o_ref = refs.pop(0)
    act_scr = refs.pop(0)

    x = x_ref[...]
    if has_proj:
        x = x + _dot(a_ref[...].astype(BF16), wo_ref[...])
    h = _rms(x, g_ref[...]).astype(BF16)

    for c in range(FF_NCHUNK):
        cols = slice(c * FF_CHUNK, (c + 1) * FF_CHUNK)
        gate = _dot(h, wup_ref[:, cols])
        up = _dot(h, wup_ref[:, D_FF + c * FF_CHUNK:D_FF + (c + 1) * FF_CHUNK])
        act_scr[:, cols] = (_silu(gate) * up).astype(BF16)
    y = x + _dot(act_scr[...], wdn_ref[...])
    o_ref[...] = _rms(y, gf_ref[...]) if final else y


def _ffn(x, g, wup, wdn, tm, a=None, wo=None, gfinal=None):
    rows = x.shape[0]
    row_spec = pl.BlockSpec((tm, D_MODEL), lambda i: (i, 0))
    args = [x]
    specs = [row_spec]
    if a is not None:
        args += [a, wo]
        specs += [row_spec, _resident(wo)]
    args += [g, _operand(wup), _operand(wdn)]
    specs += [_resident(g), _resident(wup), _resident(wdn)]
    if gfinal is not None:
        args.append(gfinal)
        specs.append(_resident(gfinal))
    return pl.pallas_call(
        functools.partial(_ffn_body, has_proj=a is not None, final=gfinal is not None),
        grid=(rows // tm,),
        in_specs=specs,
        out_specs=row_spec,
        out_shape=jax.ShapeDtypeStruct((rows, D_MODEL), F32),
        scratch_shapes=[pltpu.VMEM((tm, D_FF), BF16)],
        compiler_params=_cparams(("parallel",)),
        name="ffn",
    )(*args)


def _sgu_body(*refs, tm, emit_v):
    (x_ref, g_ref, win_ref, lng_ref, lnb_ref, wm_ref, sb_ref, wout_ref) = refs[:8]
    if emit_v:
        o_ref, vout_ref, vbuf, ubuf, sbuf = refs[8:]
    else:
        o_ref, vbuf, ubuf, sbuf = refs[8:]
        vout_ref = None
    panel = lambda p, base=0: slice(base + p * A_PANEL, base + (p + 1) * A_PANEL)

    x = x_ref[...]
    h = _rms(x, g_ref[...]).astype(BF16)

    rsum = jnp.zeros((tm, 1), F32)
    for p in range(A_NPANEL):
        v = _gelu(_dot(h, win_ref[:, panel(p, A_HALF)]))
        vbuf[p] = v
        rsum = rsum + jnp.sum(v, axis=-1, keepdims=True)
    for p in range(A_NPANEL):
        ubuf[p] = _gelu(_dot(h, win_ref[:, panel(p)]))
    mean = rsum * (1.0 / A_HALF)
    ssq = jnp.zeros((tm, 1), F32)
    for p in range(A_NPANEL):
        d = vbuf[p] - mean
        ssq = ssq + jnp.sum(d * d, axis=-1, keepdims=True)
    rstd = lax.rsqrt(ssq * (1.0 / A_HALF) + EPS)

    acc = x
    for p in range(A_NPANEL):
        vn = (vbuf[p] - mean) * rstd * lng_ref[:, panel(p)] + lnb_ref[:, panel(p)]
        if emit_v:
            vout_ref[:, p * A_PANEL:(p + 1) * A_PANEL] = vn
        vnb = vn.astype(BF16)
        for c in range(tm // A_CHUNK):
            for gg in range(A_PANEL // A_GROUP_DIM):
                grp = p * (A_PANEL // A_GROUP_DIM) + gg
                rs = slice(c * A_CHUNK, (c + 1) * A_CHUNK)
                cs = slice(gg * A_GROUP_DIM, (gg + 1) * A_GROUP_DIM)
                sbuf[rs, cs] = _dot(wm_ref[grp], vnb[rs, cs]) + sb_ref[grp]
        acc = acc + _dot((ubuf[p] * sbuf[...]).astype(BF16), wout_ref[panel(p), :])
    o_ref[...] = acc


def _sgu(x, g, win, lng, lnb, wm, sb, wout, tm, emit_v):
    rows = x.shape[0]
    row_spec = pl.BlockSpec((tm, D_MODEL), lambda i: (i, 0))
    weights = [g, win, lng, lnb, wm, sb, wout]
    out_shape = [jax.ShapeDtypeStruct((rows, D_MODEL), F32)]
    out_specs = [row_spec]
    if emit_v:
        out_shape.append(jax.ShapeDtypeStruct((rows, A_HALF), F32))
        out_specs.append(pl.BlockSpec((tm, A_HALF), lambda i: (i, 0)))
    return pl.pallas_call(
        functools.partial(_sgu_body, tm=tm, emit_v=emit_v),
        grid=(rows // tm,),
        in_specs=[row_spec] + [_resident(w) for w in weights],
        out_specs=out_specs,
        out_shape=out_shape,
        scratch_shapes=[pltpu.VMEM((A_NPANEL, tm, A_PANEL), F32),
                        pltpu.VMEM((A_NPANEL, tm, A_PANEL), F32),
                        pltpu.VMEM((tm, A_PANEL), F32)],
        compiler_params=_cparams(("parallel",)),
        name="sgu",
    )(x, *[_operand(w) for w in weights])


def _log_sigmoid(z):
    return jnp.minimum(z, 0.0) - jnp.log1p(jnp.exp(-jnp.abs(z)))


def _gla_intra(qh, kh, cum, mask_ref):
    c = qh.shape[0]
    r = lax.broadcasted_iota(jnp.int32, (c, 1), 0)
    a = jnp.zeros((c, c), F32)
    m, level = c, 0
    while m >= 2:
        half = m // 2
        pos = r & (m - 1)
        upper = pos >= half
        if m >= 2 * SUBLANES:
            pieces = []
            for b in range(c // m):
                row = b * m + half - 1
                pieces.append(jnp.broadcast_to(cum[row:row + 1, :], (m, cum.shape[1])))
            ref = pieces[0] if len(pieces) == 1 else jnp.concatenate(pieces, axis=0)
        else:
            c8 = cum.reshape(c // SUBLANES, SUBLANES, cum.shape[1])
            sub = lax.broadcasted_iota(jnp.int32, (1, SUBLANES, 1), 1)
            ref = None
            for b in reversed(range(SUBLANES // m)):
                row = jnp.broadcast_to(c8[:, b * m + half - 1:b * m + half, :], c8.shape)
                ref = row if ref is None else jnp.where(sub < (b + 1) * m, row, ref)
            ref = ref.reshape(c, cum.shape[1])
        decay = jnp.exp(jnp.where(upper, cum - ref, ref - cum))
        x = (jnp.where(upper, qh, kh) * decay).astype(BF16)
        a = a + lax.dot_general(x, x, NT_DIMS, preferred_element_type=F32) * mask_ref[level]
        m = half
        level += 1
    return a


def _gla_level_masks(c):
    rowi = lax.broadcasted_iota(jnp.int32, (c, c), 0)
    coli = lax.broadcasted_iota(jnp.int32, (c, c), 1)
    masks = []
    m = c
    while m >= 2:
        sh, half = int(math.log2(m)), m // 2
        keep = ((rowi >> sh) == (coli >> sh)) & ((rowi & (m - 1)) >= half) & ((coli & (m - 1)) < half)
        masks.append(jnp.where(keep, 1.0, 0.0))
        m = half
    return masks


def _gla_project(h, win_ref, wgl_ref, wgate_ref, bgate_ref):
    dq, dv = GLA_HEADS * GLA_DK, GLA_HEADS * GLA_DV
    q = _dot(h, win_ref[:, :dq]) * (GLA_DK ** -0.5)
    k = _dot(h, win_ref[:, dq:2 * dq])
    v = _dot(h, win_ref[:, 2 * dq:2 * dq + dv])
    g = _dot(h, win_ref[:, 2 * dq + dv:2 * dq + 2 * dv])
    gl = _dot(h, wgl_ref[...])
    z = _dot(gl.astype(BF16), wgate_ref[...]) + bgate_ref[...]
    return q, k, v, g, z


def _gla_prompt_body(x_ref, gm_ref, win_ref, wgl_ref, wgate_ref, bgate_ref,
                     gn_ref, wo_ref, o_ref, st_ref, s_scr, mask_scr):
    ci = pl.program_id(1)
    c = GLA_C

    @pl.when((pl.program_id(0) == 0) & (ci == 0))
    def _():
        for level, mask in enumerate(_gla_level_masks(c)):
            mask_scr[level] = mask

    @pl.when(ci == 0)
    def _():
        s_scr[...] = jnp.zeros_like(s_scr)

    x = x_ref[...]
    h = _rms(x, gm_ref[...]).astype(BF16)
    q, k, v, g, z = _gla_project(h, win_ref, wgl_ref, wgate_ref, bgate_ref)
    la = _log_sigmoid(z) * (1.0 / GLA_GATE_NORM)

    rowi = lax.broadcasted_iota(jnp.int32, (c, c), 0)
    coli = lax.broadcasted_iota(jnp.int32, (c, c), 1)
    tril = (coli <= rowi).astype(BF16)

    ys = []
    for sc in range(x.shape[0] // c):
        rs = slice(sc * c, (sc + 1) * c)
        cum = functools.reduce(lambda x, y: x + y,
                               [_dot(tril, piece) for piece in _bf16_pieces(la[rs], 3)])
        outs = []
        for hd in range(GLA_HEADS):
            ks = slice(hd * GLA_DK, (hd + 1) * GLA_DK)
            vs = slice(hd * GLA_DV, (hd + 1) * GLA_DV)
            qh, kh, cumh, vh = q[rs, ks], k[rs, ks], cum[:, ks], v[rs, vs]
            vhb = vh.astype(BF16)
            total = cumh[c - 1:c, :]
            st = s_scr[hd]
            o = lax.dot_general((qh * jnp.exp(cumh)).astype(BF16), st.astype(BF16), NT_DIMS,
                                preferred_element_type=F32)
            a = _gla_intra(qh, kh, cumh, mask_scr)
            o = o + _dot(a.astype(BF16), vhb)
            o = o + jnp.sum(qh * kh, axis=-1, keepdims=True) * vh
            kdec = (kh * jnp.exp(total - cumh)).astype(BF16)
            st_new = st * jnp.exp(total) + lax.dot_general(vhb, kdec, TN_DIMS,
                                                           preferred_element_type=F32)
            s_scr[hd] = st_new
            o = o * lax.rsqrt(jnp.mean(o * o, axis=-1, keepdims=True) + EPS) * gn_ref[...]
            outs.append(o)
        ys.append(jnp.concatenate(outs, axis=1))
    y = jnp.concatenate(ys, axis=0) * _silu(g)
    o_ref[...] = x + _dot(y.astype(BF16), wo_ref[...])

    @pl.when(ci == pl.num_programs(1) - 1)
    def _():
        for hd in range(GLA_HEADS):
            st_ref[0, hd] = s_scr[hd].T


def _gla_prompt(x, gm, win, wgl, wgate, bgate, gn, wo, n_seq, seq_len):
    n_chunk = seq_len // GLA_STEP_ROWS
    weights = [gm, win, wgl, wgate, bgate, gn, wo]
    row_spec = pl.BlockSpec((GLA_STEP_ROWS, D_MODEL), lambda b, c: (b * n_chunk + c, 0))
    return pl.pallas_call(
        _gla_prompt_body,
        grid=(n_seq, n_chunk),
        in_specs=[row_spec] + [_resident(w) for w in weights],
        out_specs=[row_spec,
                   pl.BlockSpec((1, GLA_HEADS, GLA_DK, GLA_DV), lambda b, c: (b, 0, 0, 0))],
        out_shape=[jax.ShapeDtypeStruct(x.shape, F32),
                   jax.ShapeDtypeStruct((n_seq, GLA_HEADS, GLA_DK, GLA_DV), F32)],
        scratch_shapes=[pltpu.VMEM((GLA_HEADS, GLA_DV, GLA_DK), F32),
                        pltpu.VMEM((int(math.log2(GLA_C)), GLA_C, GLA_C), F32)],
        compiler_params=_cparams(("arbitrary", "arbitrary")),
        name="gla_prompt",
    )(x, *weights)


def _gla_sample_proj_body(x_ref, gm_ref, win_ref, wgl_ref, wgate_ref, bgate_ref,
                          q_ref, k_ref, a_ref, v_ref, g_ref):
    h = _rms(x_ref[...], gm_ref[...]).astype(BF16)
    q_ref[...], k_ref[...], v_ref[...], g_ref[...], z = _gla_project(
        h, win_ref, wgl_ref, wgate_ref, bgate_ref)
    a_ref[...] = jnp.exp(_log_sigmoid(z) * (1.0 / GLA_GATE_NORM))


GLA_SEQ_PER_STEP = SUBLANES


def _gla_sample_rec_body(s_ref, qc_ref, kc_ref, ac_ref, v_ref, so_ref, o_ref):
    for j in range(GLA_SEQ_PER_STEP):
        for hd in range(GLA_HEADS):
            ks = slice(hd * GLA_DK, (hd + 1) * GLA_DK)
            vs = slice(hd * GLA_DV, (hd + 1) * GLA_DV)
            acol = ac_ref[0, ks, j:j + 1]
            kcol = kc_ref[0, ks, j:j + 1]
            qcol = qc_ref[0, ks, j:j + 1]
            vrow = v_ref[j:j + 1, vs]
            s_new = s_ref[j, hd] * acol + kcol * vrow
            so_ref[j, hd] = s_new
            o_ref[j:j + 1, vs] = jnp.sum(s_new * qcol, axis=0, keepdims=True)


def _gla_sample_out_body(x_ref, o_ref, g_ref, gn_ref, wo_ref, y_ref):
    outs = []
    for hd in range(GLA_HEADS):
        o = o_ref[:, hd * GLA_DV:(hd + 1) * GLA_DV]
        outs.append(o * lax.rsqrt(jnp.mean(o * o, axis=-1, keepdims=True) + EPS) * gn_ref[...])
    y = jnp.concatenate(outs, axis=1) * _silu(g_ref[...])
    y_ref[...] = x_ref[...] + _dot(y.astype(BF16), wo_ref[...])


def _gla_sample(x, state, gm, win, wgl, wgate, bgate, gn, wo):
    n = x.shape[0]
    dq = GLA_HEADS * GLA_DK
    weights = [gm, win, wgl, wgate, bgate]
    q, k, a, v, g = pl.pallas_call(
        _gla_sample_proj_body,
        out_shape=[jax.ShapeDtypeStruct((n, dq), F32)] * 3
        + [jax.ShapeDtypeStruct((n, D_MODEL), F32)] * 2,
        compiler_params=pltpu.CompilerParams(vmem_limit_bytes=VMEM_LIMIT),
        name="gla_sample_proj",
    )(x, *weights)

    nstep = n // GLA_SEQ_PER_STEP

    def cols(t):
        return t.reshape(nstep, GLA_SEQ_PER_STEP, dq).transpose(0, 2, 1)

    col_spec = pl.BlockSpec((1, dq, GLA_SEQ_PER_STEP), lambda i: (i, 0, 0))
    st_spec = pl.BlockSpec((GLA_SEQ_PER_STEP, GLA_HEADS, GLA_DK, GLA_DV), lambda i: (i, 0, 0, 0))
    row_spec = pl.BlockSpec((GLA_SEQ_PER_STEP, D_MODEL), lambda i: (i, 0))
    s_new, o = pl.pallas_call(
        _gla_sample_rec_body,
        grid=(nstep,),
        in_specs=[st_spec, col_spec, col_spec, col_spec, row_spec],
        out_specs=[st_spec, row_spec],
        out_shape=[jax.ShapeDtypeStruct(state.shape, F32), jax.ShapeDtypeStruct((n, D_MODEL), F32)],
        compiler_params=_cparams(("parallel",)),
        name="gla_sample_rec",
    )(state, cols(q), cols(k), cols(a), v)

    y = pl.pallas_call(
        _gla_sample_out_body,
        out_shape=jax.ShapeDtypeStruct((n, D_MODEL), F32),
        compiler_params=pltpu.CompilerParams(vmem_limit_bytes=VMEM_LIMIT),
        name="gla_sample_out",
    )(x, o, g, gn, wo)
    return y, s_new


def _t5_bucket_np(dist):
    max_exact = REL_BUCKETS // 2
    n = np.maximum(dist, 0)
    nf = np.maximum(n, max_exact).astype(np.float32)
    large = max_exact + (np.log(nf / np.float32(max_exact)) / np.float32(math.log(REL_MAX_DIST / max_exact))
                         * np.float32(REL_BUCKETS - max_exact)).astype(np.int32)
    return np.where(n < max_exact, n, np.minimum(large, REL_BUCKETS - 1)).astype(np.int32)


def _bias_tables_body(rb_ref, bd_ref, b1_ref, bdec_ref, od_ref, o1_ref, odec_ref):
    h = pl.program_id(0)
    td, t1, tdec = bd_ref[...], b1_ref[...], bdec_ref[...]
    ad = jnp.zeros(td.shape, F32)
    a1 = jnp.zeros(t1.shape, F32)
    adec = jnp.zeros(tdec.shape, F32)
    for b in range(REL_BUCKETS):
        val = rb_ref[b, h]
        ad = jnp.where(td == b, val, ad)
        a1 = jnp.where(t1 == b, val, a1)
        adec = jnp.where(tdec == b, val, adec)
    od_ref[0] = ad * LOG2E
    o1_ref[0] = a1 * LOG2E
    odec_ref[0] = adec


def _bias_tables(rel_bias, past_len):
    kc = np.arange(MOBA_BLOCK)[:, None]
    qr = np.arange(MOBA_BLOCK)[None, :]
    bd = _t5_bucket_np(qr - kc)
    b1 = _t5_bucket_np(MOBA_BLOCK + qr - kc)
    nblk = past_len // MOBA_BLOCK
    kpos = np.arange(nblk)[:, None] * MOBA_BLOCK + np.arange(MOBA_BLOCK)[None, :]
    bdec = _t5_bucket_np(past_len - kpos)
    full = lambda a: pl.BlockSpec(a.shape, lambda h: (0,) * a.ndim)
    return pl.pallas_call(
        _bias_tables_body,
        grid=(MOBA_HEADS,),
        in_specs=[pl.BlockSpec(memory_space=pltpu.SMEM), full(bd), full(b1), full(bdec)],
        out_specs=[pl.BlockSpec((1,) + bd.shape, lambda h: (h, 0, 0)),
                   pl.BlockSpec((1,) + b1.shape, lambda h: (h, 0, 0)),
                   pl.BlockSpec((1,) + bdec.shape, lambda h: (h, 0, 0))],
        out_shape=[jax.ShapeDtypeStruct((MOBA_HEADS,) + bd.shape, F32),
                   jax.ShapeDtypeStruct((MOBA_HEADS,) + b1.shape, F32),
                   jax.ShapeDtypeStruct((MOBA_HEADS,) + bdec.shape, F32)],
        compiler_params=_cparams(("parallel",)),
        name="moba_bias_tables",
    )(rel_bias, jnp.asarray(bd), jnp.asarray(b1), jnp.asarray(bdec))


def _moba_qkv_body(x_ref, g_ref, w_ref, q_ref, k_ref, v_ref, *t_refs):
    h = _rms(x_ref[...], g_ref[...]).astype(BF16)
    qkv = _dot(h, w_ref[...])
    q_ref[...] = qkv[:, :D_MODEL] * (MOBA_DH ** -0.5)
    k = qkv[:, D_MODEL:2 * D_MODEL]
    v = qkv[:, 2 * D_MODEL:]
    k_ref[...] = k
    v_ref[...] = v
    if t_refs:
        t_refs[0][0] = k.T
        t_refs[1][0] = v.T


def _moba_qkv(x, g, w, tm, seq_len=None):
    rows = x.shape[0]
    row_spec = pl.BlockSpec((tm, D_MODEL), lambda i: (i, 0))
    out_specs = [row_spec] * 3
    out_shape = [jax.ShapeDtypeStruct((rows, D_MODEL), F32)] * 3
    if seq_len is not None:
        per_seq = seq_len // tm
        t_spec = pl.BlockSpec((1, D_MODEL, tm), lambda i: (i // per_seq, 0, i % per_seq))
        out_specs += [t_spec] * 2
        out_shape += [jax.ShapeDtypeStruct((rows // seq_len, D_MODEL, seq_len), F32)] * 2
    return pl.pallas_call(
        _moba_qkv_body,
        grid=(rows // tm,),
        in_specs=[row_spec, _resident(g), _resident(w)],
        out_specs=out_specs,
        out_shape=out_shape,
        compiler_params=_cparams(("parallel",)),
        name="moba_qkv",
    )(x, g, w)


def _topk_select(gate, valid, axis):
    n = gate.shape[axis]
    idx = lax.broadcasted_iota(jnp.int32, gate.shape, axis)
    cnt = jnp.zeros(gate.shape, jnp.int32)
    for m in range(n):
        gm = lax.slice_in_dim(gate, m, m + 1, axis=axis)
        beats = (gm > gate) | ((gm == gate) & (m < idx))
        cnt = cnt + jnp.where(beats, valid(m), 0)
    return jnp.where(cnt < MOBA_TOPK, 1.0, 0.0)


def _moba_prompt_body(rb_ref, q_ref, k_ref, v_ref, bd_ref, b1_ref, o_ref,
                      kb_scr, vt_scr, km_scr):
    hg = pl.program_id(1)
    qi = pl.program_id(2)
    nblk = kb_scr.shape[0]
    blk = MOBA_BLOCK
    dh = MOBA_DH
    width = MOBA_HG * dh

    lane = lax.broadcasted_iota(jnp.int32, (1, width), 1)
    in_head = [(lane >= hh * dh) & (lane < (hh + 1) * dh) for hh in range(MOBA_HG)]

    @pl.when(qi == 0)
    def _():
        ones = jnp.ones((MOBA_ONES_ROWS, blk), BF16)
        means = []
        for n in range(nblk):
            kt = k_ref[n * blk:(n + 1) * blk, :]
            kb_scr[n] = kt.astype(BF16)
            means.append(jnp.mean(kt, axis=0, keepdims=True))
            vt = v_ref[n * blk:(n + 1) * blk, :].T.astype(BF16)
            for hh in range(MOBA_HG):
                vt_scr[n, hh, :dh, :] = vt[hh * dh:(hh + 1) * dh, :]
                vt_scr[n, hh, dh:, :] = ones
        km = jnp.concatenate(means, axis=0)
        per_head = [jnp.where(in_head[hh], km, 0.0) for hh in range(MOBA_HG)]
        pad = jnp.zeros((LANES - MOBA_HG * nblk, width), F32)
        km_t = jnp.concatenate(per_head + [pad], axis=0).T
        km_scr[0], km_scr[1] = _bf16_pieces(km_t, 2)

    q2 = q_ref[...]
    ki_idx = lax.broadcasted_iota(jnp.int32, (blk, blk), 0)
    qi_idx = lax.broadcasted_iota(jnp.int32, (blk, blk), 1)
    blk_idx = lax.broadcasted_iota(jnp.int32, (nblk, blk), 0)

    hs = range(MOBA_HG)
    qhb, far_bias, neg, tile_of = [], [], [], []
    q_hi, q_lo = _bf16_pieces(q2, 2)
    gates = (_dot(q_hi, km_scr[0]) + _dot(q_hi, km_scr[1]) + _dot(q_lo, km_scr[0])).T
    for hh in hs:
        tile_of.append(slice(hh * dh // LANES * LANES, (hh * dh // LANES + 1) * LANES))
        qhb.append((jnp.where(in_head[hh], q2, 0.0)[:, tile_of[hh]] * LOG2E).astype(BF16))
        far_bias.append(rb_ref[REL_BUCKETS - 1, MOBA_HG * hg + hh] * LOG2E)
        gate = gates[hh * nblk:(hh + 1) * nblk, :]
        keep = _topk_select(gate, lambda m: (m < qi).astype(jnp.int32), 0)
        neg.append(jnp.where((keep > 0.0) & (blk_idx < qi), 0.0, NEG))

    def scores(k0, n):
        kt = kb_scr[k0:k0 + n].reshape(n * blk, width)
        return [lax.dot_general(kt[:, tile_of[hh]], qhb[hh], NT_DIMS, preferred_element_type=F32)
                for hh in hs]

    def values(ki, p):
        return [_dot(vt_scr[ki, hh], p[hh]) for hh in hs]

    def weights(x):
        return jnp.exp2(x.astype(BF16))

    def step(c):
        units = [(c - 1, 2)] if c >= 1 else [(0, 1)]
        t = 0
        while t < c - 1:
            n = min(2, c - 1 - t)
            units.append((t, n))
            t += n
        m = [None] * MOBA_HG
        acc = [None] * MOBA_HG
        ahead = [scores(*u) for u in units[:MOBA_LOOKAHEAD]]
        for ui, (k0, n) in enumerate(units):
            s = ahead.pop(0)
            if ui + MOBA_LOOKAHEAD < len(units):
                ahead.append(scores(*units[ui + MOBA_LOOKAHEAD]))
            p = [[] for _ in range(n)]
            alpha = [None] * MOBA_HG
            for hh in hs:
                sh, rows, tile_max = [], [], []
                for t in range(n):
                    kb = k0 + t
                    x = s[hh][t * blk:(t + 1) * blk, :]
                    if kb == c:
                        x, r = jnp.where(ki_idx <= qi_idx, x + bd_ref[hh], NEG), None
                    elif kb == c - 1:
                        x, r = x + b1_ref[hh], neg[hh][kb:kb + 1, :]
                    else:
                        r = neg[hh][kb:kb + 1, :] + far_bias[hh]
                    mx = jnp.max(x, axis=0, keepdims=True)
                    sh.append(x)
                    rows.append(r)
                    tile_max.append(mx if r is None else mx + r)
                m_new = functools.reduce(jnp.maximum, tile_max if m[hh] is None else [m[hh]] + tile_max)
                for t in range(n):
                    p[t].append(weights(sh[t] - (m_new if rows[t] is None else m_new - rows[t])))
                if m[hh] is not None:
                    alpha[hh] = jnp.exp2(m[hh] - m_new)
                m[hh] = m_new
            pv = [values(k0 + t, p[t]) for t in range(n)]
            for hh in hs:
                tot = functools.reduce(lambda x, y: x + y, [pv[t][hh] for t in range(n)])
                acc[hh] = tot if acc[hh] is None else alpha[hh] * acc[hh] + tot
        ot = jnp.concatenate([acc[hh][:dh, :] / acc[hh][dh:dh + 1, :] for hh in hs], axis=0)
        o_ref[...] = ot.T

    for c in range(nblk):
        pl.when(qi == c)(functools.partial(step, c))


def _moba_prompt(rel_bias, q, k, v, bias_d, bias_1, n_seq, seq_len):
    nblk = seq_len // MOBA_BLOCK
    hgrp = MOBA_HG
    width = hgrp * MOBA_DH
    q_spec = pl.BlockSpec((MOBA_BLOCK, width), lambda b, hg, qi: (b * nblk + qi, hg))
    kv_spec = pl.BlockSpec((seq_len, width), lambda b, hg, qi: (b, hg))
    bias_spec = pl.BlockSpec((hgrp, MOBA_BLOCK, MOBA_BLOCK), lambda b, hg, qi: (hg, 0, 0))
    return pl.pallas_call(
        _moba_prompt_body,
        grid=(n_seq, MOBA_HEADS // hgrp, nblk),
        in_specs=[pl.BlockSpec(memory_space=pltpu.SMEM), q_spec, kv_spec, kv_spec, bias_spec, bias_spec],
        out_specs=q_spec,
        out_shape=jax.ShapeDtypeStruct(q.shape, F32),
        scratch_shapes=[
            pltpu.VMEM((nblk, MOBA_BLOCK, width), BF16),
            pltpu.VMEM((nblk, hgrp, MOBA_DH + MOBA_ONES_ROWS, MOBA_BLOCK), BF16),
            pltpu.VMEM((2, width, LANES), BF16),
        ],
        compiler_params=_cparams(("parallel", "parallel", "arbitrary")),
        name="moba_prompt",
    )(rel_bias, q, k, v, bias_d, bias_1)


def _col_to_row(col):
    n = col.shape[0]
    r = lax.broadcasted_iota(jnp.int32, (n, n), 0)
    c = lax.broadcasted_iota(jnp.int32, (n, n), 1)
    return jnp.sum(jnp.where(r == c, jnp.broadcast_to(col, (n, n)), 0.0), axis=0, keepdims=True)


def _moba_decode_body(pt_ref, q_ref, qt_ref, kn_ref, vnt_ref, rb0_ref, bdec_ref, *refs, npg):
    del pt_ref
    ck, cv = refs[:npg], refs[npg:2 * npg]
    o_ref, qb_scr, s_scr, p_scr, t_scr = refs[2 * npg:]
    heads, dh = MOBA_HEADS, MOBA_DH
    ppb = MOBA_BLOCK // PAGE_SIZE
    nb = npg // ppb

    qt = qt_ref[0]
    for h in range(heads):
        qb_scr[h] = jnp.broadcast_to(qt[:, h:h + 1], (dh, PAGE_SIZE))

    def k_step(h, carry):
        qb = qb_scr[h]
        for pg in range(npg):
            s_scr[pg, pl.ds(h, 1), :] = jnp.sum(ck[pg][0, h] * qb, axis=0, keepdims=True)
        return carry

    lax.fori_loop(0, heads, k_step, 0, unroll=2)

    z, m_blk = [], []
    blk_lane = lax.broadcasted_iota(jnp.int32, (heads, nb), 1)
    gate = jnp.zeros((heads, nb), F32)
    for b in range(nb):
        zb = []
        g = jnp.zeros((heads, 1), F32)
        for j in range(ppb):
            s = s_scr[ppb * b + j]
            g = g + jnp.sum(s, axis=-1, keepdims=True)
            zb.append(s + bdec_ref[ppb * b + j])
        gate = jnp.where(blk_lane == b, g, gate)
        m_blk.append(functools.reduce(jnp.maximum, [jnp.max(t, axis=-1, keepdims=True) for t in zb]))
        z += zb
    keep = _topk_select(gate, lambda m_: 1, 1)

    l_new = jnp.sum(q_ref[0] * kn_ref[0], axis=-1, keepdims=True) + rb0_ref[...]
    m_all = l_new
    for b in range(nb):
        m_all = jnp.maximum(m_all, jnp.where(keep[:, b:b + 1] > 0.0, m_blk[b], NEG))
    w_new = jnp.exp(l_new - m_all)
    den = w_new
    for b in range(nb):
        kept = keep[:, b:b + 1] > 0.0
        for j in range(ppb):
            p = jnp.exp(jnp.where(kept, z[ppb * b + j] - m_all, NEG))
            den = den + jnp.sum(p, axis=-1, keepdims=True)
            p_scr[ppb * b + j] = p

    def v_step(h, carry):
        acc = cv[0][0, h] * p_scr[0, pl.ds(h, 1), :]
        for pg in range(1, npg):
            acc = acc + cv[pg][0, h] * p_scr[pg, pl.ds(h, 1), :]
        t_scr[h] = jnp.broadcast_to(jnp.sum(acc, axis=-1, keepdims=True), (dh, PAGE_SIZE))
        return carry

    lax.fori_loop(0, heads, v_step, 0, unroll=4)

    head_lane = lax.broadcasted_iota(jnp.int32, (dh, heads), 1)
    tot = jnp.zeros((dh, heads), F32)
    for h in range(heads):
        tot = jnp.where(head_lane == h, t_scr[h][:, :heads], tot)
    o_ref[0] = (tot + vnt_ref[0] * _col_to_row(w_new)) / _col_to_row(den)


def _moba_decode(page_table, q, k_new, v_new, rb0, cache_k, cache_v, bias_dec):
    n_dec, n_pages = page_table.shape
    heads, dh = MOBA_HEADS, MOBA_DH
    rows = lambda t: t.reshape(n_dec, heads, dh)
    cols = lambda t: t.reshape(n_dec, heads, dh).transpose(0, 2, 1)
    row_spec = pl.BlockSpec((1, heads, dh), lambda b, pt: (b, 0, 0))
    col_spec = pl.BlockSpec((1, dh, heads), lambda b, pt: (b, 0, 0))
    page_specs = [pl.BlockSpec((1, heads, dh, PAGE_SIZE), lambda b, pt, j=j: (pt[b, j], 0, 0, 0))
                  for j in range(n_pages)]
    grid_spec = pltpu.PrefetchScalarGridSpec(
        num_scalar_prefetch=1,
        grid=(n_dec,),
        in_specs=[row_spec, col_spec, row_spec, col_spec,
                  pl.BlockSpec((heads, 1), lambda b, pt: (0, 0)),
                  pl.BlockSpec(bias_dec.shape, lambda b, pt: (0, 0, 0))] + page_specs + page_specs,
        out_specs=col_spec,
        scratch_shapes=[pltpu.VMEM((heads, dh, PAGE_SIZE), F32),
                        pltpu.VMEM((n_pages, heads, PAGE_SIZE), F32),
                        pltpu.VMEM((n_pages, heads, PAGE_SIZE), F32),
                        pltpu.VMEM((heads, dh, PAGE_SIZE), F32)],
    )
    out = pl.pallas_call(
        functools.partial(_moba_decode_body, npg=n_pages),
        grid_spec=grid_spec,
        out_shape=jax.ShapeDtypeStruct((n_dec, dh, heads), F32),
        compiler_params=_cparams(("parallel",)),
        name="moba_decode",
    )(page_table, rows(q), cols(q), rows(k_new), cols(v_new), rb0, bias_dec,
      *([cache_k] * n_pages), *([cache_v] * n_pages))
    return out.transpose(0, 2, 1).reshape(n_dec, heads * dh)


def _prep_sgu(ln_g, ln_b, w_s, b_s):
    lng = ln_g.reshape(1, A_HALF)
    lnb = ln_b.reshape(1, A_HALF)
    causal = jnp.tril(jnp.ones((A_CHUNK, A_CHUNK), dtype=bool))
    wm_p = jnp.where(causal[None], w_s, 0.0).astype(BF16)
    sb_p = jnp.broadcast_to(b_s[:, :, None], (A_GROUPS, A_CHUNK, A_GROUP_DIM))
    eye = jnp.eye(A_CHUNK, dtype=F32)
    wm_s = (w_s[:, 0, 0][:, None, None] * eye[None]).astype(BF16)
    sb_s = jnp.broadcast_to(b_s[:, 0][:, None, None], (A_GROUPS, A_CHUNK, A_GROUP_DIM))
    return (lng, lnb), (wm_p, sb_p), (wm_s, sb_s)


def _prep_gla(w_in, w_gate, b_gate, gn, w_out):
    dq, dv = GLA_HEADS * GLA_DK, GLA_HEADS * GLA_DV
    pad = LANES - GLA_GATE_RANK
    wgl = jnp.pad(w_in[:, 2 * dq + 2 * dv:], ((0, 0), (0, pad))).astype(BF16)
    wgate = jnp.pad(w_gate, ((0, pad), (0, 0))).astype(BF16)
    return (w_in.astype(BF16), wgl, wgate, b_gate.reshape(1, dq), gn.reshape(1, GLA_DV),
            w_out.astype(BF16))


def kernel(x_prompt, x_sample, cache_k, cache_v, page_table, state_gla, norm_mix, norm_ffn, norm_final,
           w_in_a, ln_a_g, ln_a_b, w_s_a, b_s_a, w_out_a, w_in_b, w_gate_b, b_gate_b, gn_b, w_out_b,
           w_in_c, w_out_c, rel_bias, w_up, w_down):
    n_pr, l_pr, _ = x_prompt.shape
    n_dec, l_dec, _ = x_sample.shape
    assert l_dec == 1 and l_pr % MOBA_BLOCK == 0 and l_pr % GLA_STEP_ROWS == 0
    depth = norm_mix.shape[0]
    past_len = page_table.shape[1] * PAGE_SIZE
    assert past_len % MOBA_BLOCK == 0
    tm_p, tm_s = 512, n_dec

    xp = x_prompt.reshape(n_pr * l_pr, D_MODEL)
    xs = x_sample.reshape(n_dec, D_MODEL)
    gfin = norm_final.reshape(1, D_MODEL)
    w_up_b, w_down_b = w_up.astype(BF16), w_down.astype(BF16)
    w_in_a_b, w_out_a_b = w_in_a.astype(BF16), w_out_a.astype(BF16)
    k_p, v_p, k_s, v_s, gla_p, gla_s, sgu_s = [], [], [], [], [], [], []

    for i in range(depth):
        kind, j = i % 3, i // 3
        gm = norm_mix[i].reshape(1, D_MODEL)
        gf = norm_ffn[i].reshape(1, D_MODEL)
        wup, wdn = (w_up_b, i), (w_down_b, i)
        last = gfin if i == depth - 1 else None
        proj_p = proj_s = None
        if kind == 0:
            (lng, lnb), mode_p, mode_s = _prep_sgu(ln_a_g[j], ln_a_b[j], w_s_a[j], b_s_a[j])
            win, wout = (w_in_a_b, j), (w_out_a_b, j)
            (xp,) = _sgu(xp, gm, win, lng, lnb, *mode_p, wout, tm_p, False)
            xs, v_rows = _sgu(xs, gm, win, lng, lnb, *mode_s, wout, tm_s, True)
            sgu_s.append(v_rows.reshape(n_dec, l_dec, A_HALF))
        elif kind == 1:
            wts = _prep_gla(w_in_b[j], w_gate_b[j], b_gate_b[j], gn_b[j], w_out_b[j])
            xp, sp = _gla_prompt(xp, gm, *wts, n_pr, l_pr)
            xs, ss = _gla_sample(xs, state_gla[j], gm, *wts)
            gla_p.append(sp)
            gla_s.append(ss)
        else:
            w_in = w_in_c[j].astype(BF16)
            bias_d, bias_1, bias_dec = _bias_tables(rel_bias, past_len)
            qp, kp, vp, kpt, vpt = _moba_qkv(xp, gm, w_in, tm_p, seq_len=l_pr)
            op = _moba_prompt(rel_bias, qp, kp, vp, bias_d, bias_1, n_pr, l_pr)
            qs, kn, vn = _moba_qkv(xs, gm, w_in, tm_s)
            os_ = _moba_decode(page_table, qs, kn, vn, rel_bias[0].reshape(MOBA_HEADS, 1),
                               cache_k[j].transpose(0, 2, 3, 1), cache_v[j].transpose(0, 2, 3, 1),
                               bias_dec.reshape(MOBA_HEADS, -1, PAGE_SIZE).transpose(1, 0, 2))
            wo_c = w_out_c[j].astype(BF16)
            proj_p, proj_s = (op, wo_c), (os_, wo_c)
            heads_last = lambda t: t.reshape(n_pr, MOBA_HEADS, MOBA_DH, l_pr).transpose(0, 3, 1, 2)
            k_p.append(heads_last(kpt))
            v_p.append(heads_last(vpt))
            k_s.append(kn.reshape(n_dec, l_dec, MOBA_HEADS, MOBA_DH))
            v_s.append(vn.reshape(n_dec, l_dec, MOBA_HEADS, MOBA_DH))
        if proj_p is None:
            xp = _ffn(xp, gf, wup, wdn, tm_p, gfinal=last)
            xs = _ffn(xs, gf, wup, wdn, tm_s, gfinal=last)
        else:
            xp = _ffn(xp, gf, wup, wdn, tm_p, a=proj_p[0], wo=proj_p[1], gfinal=last)
            xs = _ffn(xs, gf, wup, wdn, tm_s, a=proj_s[0], wo=proj_s[1], gfinal=last)

    return (xp.reshape(n_pr, l_pr, D_MODEL), xs.reshape(n_dec, l_dec, D_MODEL),
            jnp.stack(k_p), jnp.stack(v_p), jnp.stack(k_s), jnp.stack(v_s),
            jnp.stack(gla_p), jnp.stack(gla_s), jnp.stack(sgu_s))
```
